```python
import math
import numpy as np
import jax
import jax.numpy as jnp
from jax import lax

D_MODEL = 1024
BATCH = 8
SEQ = 4096
DEPTH = 2

GRID_W = 64
CTX_LEN = 256
EPS = 1e-6

A_HEADS = 4
A_QK_DIM = 64
A_V_DIM = 2 * A_QK_DIM
B_HEADS = 8
B_HEAD_DIM = 64
A_Q_W = A_HEADS * 2 * A_QK_DIM
A_V_W = A_HEADS * A_V_DIM
B_W = B_HEADS * B_HEAD_DIM
ATTN_SPLITS = [A_Q_W, 2 * A_Q_W, 2 * A_Q_W + A_V_W, 2 * A_Q_W + A_V_W + B_W, 2 * A_Q_W + A_V_W + 2 * B_W]
ATTN_IN_W = 2 * A_Q_W + A_V_W + 3 * B_W
ATTN_OUT_W = A_V_W + B_W
Q_BLOCK = 128
NA_ROWS = 8
NA_COLS = 16
ROPE_BASE = 10000.0

CHUNK = 128
C_GROUPS = 8
C_WIDTH = D_MODEL
C_GROUP_W = C_WIDTH // C_GROUPS

PEER_HEADS = 8
N_KEYS = 128
N_EXPERTS = N_KEYS * N_KEYS
PEER_KEY_DIM = 256
PEER_HALF = PEER_KEY_DIM // 2
PEER_TOPK = 16
PEER_BLOCK = 128

N_ATTN_LAYERS = (DEPTH + 1) // 2
N_CHUNK_LAYERS = DEPTH // 2

kernel_name = "hybrid_diffattn_natten_chunkgmlp_peer_prefix"


def rms_norm(x):
    xf = x.astype(jnp.float32)
    y = xf * lax.rsqrt(jnp.mean(xf * xf, axis=-1, keepdims=True) + EPS)
    return y.astype(x.dtype)


def layer_norm(x, g, b):
    xf = x.astype(jnp.float32)
    mu = jnp.mean(xf, axis=-1, keepdims=True)
    var = jnp.mean(jnp.square(xf - mu), axis=-1, keepdims=True)
    return ((xf - mu) * lax.rsqrt(var + EPS)).astype(x.dtype) * g + b


def modulate(h, shift, scale):
    return rms_norm(h) * (1 + scale) + shift


def adaln(cond, w_mod, b_mod):
    return jnp.split(jax.nn.silu(cond) @ w_mod + b_mod, 6, axis=-1)


def axial_rope_tables(n_tokens, dim):
    axis_dim = dim // 2
    inv_freq = 1.0 / (ROPE_BASE ** (jnp.arange(0, axis_dim, 2, dtype=jnp.float32) / axis_dim))
    t = jnp.arange(n_tokens)
    row = (t // GRID_W).astype(jnp.float32)
    col = (t % GRID_W).astype(jnp.float32)
    ang = jnp.stack([row[:, None] * inv_freq, col[:, None] * inv_freq], axis=1)
    return jnp.cos(ang), jnp.sin(ang)


def apply_axial_rope(x, cos, sin):
    shp = x.shape
    xr = x.reshape(shp[:-1] + (2, 2, shp[-1] // 4)).astype(jnp.float32)
    x1, x2 = xr[..., 0, :], xr[..., 1, :]
    out = jnp.stack([x1 * cos - x2 * sin, x2 * cos + x1 * sin], axis=-2)
    return out.reshape(shp).astype(x.dtype)


def split_heads_ab(z):
    b, n, _ = z.shape
    qa, ka, va, qb, kb, vb = jnp.split(z, ATTN_SPLITS, axis=-1)
    qa = qa.reshape(b, n, A_HEADS, 2, A_QK_DIM).transpose(0, 2, 3, 1, 4)
    ka = ka.reshape(b, n, A_HEADS, 2, A_QK_DIM).transpose(0, 2, 3, 1, 4)
    va = va.reshape(b, n, A_HEADS, A_V_DIM).transpose(0, 2, 1, 3)
    qb = qb.reshape(b, n, B_HEADS, B_HEAD_DIM).transpose(0, 2, 1, 3)
    kb = kb.reshape(b, n, B_HEADS, B_HEAD_DIM).transpose(0, 2, 1, 3)
    vb = vb.reshape(b, n, B_HEADS, B_HEAD_DIM).transpose(0, 2, 1, 3)
    return qa, ka, va, qb, kb, vb


def merge_heads_ab(oa, ob, w_out):
    b, _, n, _ = oa.shape
    oa = oa.transpose(0, 2, 1, 3).reshape(b, n, A_V_W)
    ob = ob.transpose(0, 2, 1, 3).reshape(b, n, B_W)
    return jnp.concatenate([oa, ob], axis=-1) @ w_out


def diff_attention(q, k, v, lam, lam_init, subln_g):
    b, h, _, nq, d = q.shape
    scale = d ** -0.5
    qb = jnp.moveaxis(q.reshape(b, h, 2, nq // Q_BLOCK, Q_BLOCK, d), 3, 0)

    def one_block(q_blk):
        s = jnp.einsum("bhmqd,bhmkd->bhmqk", q_blk, k).astype(jnp.float32) * scale
        p = jax.nn.softmax(s, axis=-1)
        a = p[:, :, 0] - lam * p[:, :, 1]
        return jnp.einsum("bhqk,bhkd->bhqd", a.astype(v.dtype), v)

    o = lax.map(one_block, qb)
    o = jnp.moveaxis(o, 0, 2).reshape(b, h, nq, 2 * d)
    return rms_norm(o) * subln_g * (1.0 - lam_init)


def neighbourhood_attention(q, k, v, k_ctx, v_ctx, rpb):
    b, h, n, d = q.shape
    rows = n // GRID_W
    wr = min(NA_ROWS, rows)
    nk = wr * NA_COLS
    scale = d ** -0.5
    qg = q.reshape(b, h, rows, GRID_W, d)
    kg = k.reshape(b, h, rows, GRID_W, d)
    vg = v.reshape(b, h, rows, GRID_W, d)
    cols = np.arange(GRID_W)
    col_start = np.clip(cols - NA_COLS // 2, 0, GRID_W - NA_COLS)
    col_idx = col_start[:, None] + np.arange(NA_COLS)[None, :]
    rpb_cols = rpb[:, :, col_idx - cols[:, None] + (NA_COLS - 1)]

    def one_row(r):
        r0 = jnp.clip(r - wr // 2, 0, rows - wr)
        k_nb = lax.dynamic_slice_in_dim(kg, r0, wr, axis=2)[:, :, :, col_idx]
        v_nb = lax.dynamic_slice_in_dim(vg, r0, wr, axis=2)[:, :, :, col_idx]
        q_r = lax.dynamic_index_in_dim(qg, r, axis=2, keepdims=False)
        bias = jnp.take(rpb_cols, r0 + jnp.arange(wr) - r + (NA_ROWS - 1), axis=1)
        s_loc = (jnp.einsum("bhqd,bhrqwd->bhqrw", q_r, k_nb).astype(jnp.float32) * scale
                 + jnp.transpose(bias, (0, 2, 1, 3)).astype(jnp.float32))
        s_ctx = jnp.einsum("bhqd,bhkd->bhqk", q_r, k_ctx).astype(jnp.float32) * scale
        s = jnp.concatenate([s_loc.reshape(b, h, GRID_W, nk), s_ctx], axis=-1)
        p = jax.nn.softmax(s, axis=-1).astype(v.dtype)
        p_loc = p[..., :nk].reshape(b, h, GRID_W, wr, NA_COLS)
        return (jnp.einsum("bhqrw,bhrqwd->bhqd", p_loc, v_nb)
                + jnp.einsum("bhqk,bhkd->bhqd", p[..., nk:], v_ctx))

    o = lax.map(one_row, jnp.arange(rows))
    return jnp.transpose(o, (1, 2, 0, 3, 4)).reshape(b, h, n, d)


def dense_attention(q, k, v):
    s = jnp.einsum("bhqd,bhkd->bhqk", q, k).astype(jnp.float32) * (q.shape[-1] ** -0.5)
    return jnp.einsum("bhqk,bhkd->bhqd", jax.nn.softmax(s, axis=-1).astype(v.dtype), v)


def mixer_diff_na(xl, xc, w_in, w_out, lam, lam_init, subln_g, rpb, need_ctx):
    qa_l, ka_l, va_l, qb_l, kb_l, vb_l = split_heads_ab(xl @ w_in)
    qa_c, ka_c, va_c, qb_c, kb_c, vb_c = split_heads_ab(xc @ w_in)
    cos, sin = axial_rope_tables(xl.shape[1], A_QK_DIM)
    qa_l = apply_axial_rope(qa_l, cos, sin)
    ka_l = apply_axial_rope(ka_l, cos, sin)
    ka_all = jnp.concatenate([ka_l, ka_c], axis=3)
    va_all = jnp.concatenate([va_l, va_c], axis=2)
    oa_l = diff_attention(qa_l, ka_all, va_all, lam, lam_init, subln_g)
    ob_l = neighbourhood_attention(qb_l, kb_l, vb_l, kb_c, vb_c, rpb)
    out_l = merge_heads_ab(oa_l, ob_l, w_out)
    if not need_ctx:
        return out_l, None
    oa_c = diff_attention(qa_c, ka_c, va_c, lam, lam_init, subln_g)
    ob_c = dense_attention(qb_c, kb_c, vb_c)
    return out_l, merge_heads_ab(oa_c, ob_c, w_out)


def chunk_spatial_gating(x, w_in, b_in, ln_g, ln_b, w_s, b_s, w_out):
    b, n, _ = x.shape
    z = jax.nn.gelu(x @ w_in + b_in)
    u, v = jnp.split(z, 2, axis=-1)
    v = layer_norm(v, ln_g, ln_b).reshape(b, n // CHUNK, CHUNK, C_GROUPS, C_GROUP_W)
    sv = jnp.einsum("gpq,bnqgc->bnpgc", w_s, v) + b_s.T[:, :, None]
    return (u * sv.reshape(b, n, C_WIDTH)) @ w_out


def peer(x, w_query, sub_keys, u_emb, v_emb):
    b, n, d = x.shape
    xt = x.reshape(b * n // PEER_BLOCK, PEER_BLOCK, d)

    def one_block(xb):
        q = (xb @ w_query).reshape(PEER_BLOCK, PEER_HEADS, 2, PEER_HALF)
        s = jnp.einsum("thpd,hpkd->thpk", q, sub_keys).astype(jnp.float32)
        s1, i1 = lax.top_k(s[:, :, 0], PEER_TOPK)
        s2, i2 = lax.top_k(s[:, :, 1], PEER_TOPK)
        cand = (s1[..., :, None] + s2[..., None, :]).reshape(PEER_BLOCK, PEER_HEADS, PEER_TOPK * PEER_TOPK)
        sc, flat = lax.top_k(cand, PEER_TOPK)
        e = (jnp.take_along_axis(i1, flat // PEER_TOPK, axis=-1) * N_KEYS
             + jnp.take_along_axis(i2, flat % PEER_TOPK, axis=-1))
        g = jax.nn.softmax(sc, axis=-1)
        a = jax.nn.gelu(jnp.einsum("thkd,td->thk", u_emb[e], xb).astype(jnp.float32))
        return jnp.einsum("thk,thkd->td", (g * a).astype(xb.dtype), v_emb[e])

    return lax.map(one_block, xt).reshape(b, n, d)


def setup_inputs(seed: int = 0) -> dict:
    key = jax.random.key(seed)
    ks = jax.random.split(key, 26)
    D = D_MODEL
    La, Lc = N_ATTN_LAYERS, N_CHUNK_LAYERS

    def nrm(k, shape, s):
        return jax.random.normal(k, shape, jnp.float32) * s

    return {
        "x": nrm(ks[0], (BATCH, SEQ, D), 1.0),
        "c": nrm(ks[1], (BATCH, D), 1.0),
        "ctx": nrm(ks[2], (BATCH, CTX_LEN, D), 1.0),
        "c_ctx": nrm(ks[3], (D,), 1.0),
        "w_mod": nrm(ks[4], (DEPTH, D, 6 * D), D ** -0.5),
        "b_mod": nrm(ks[5], (DEPTH, 6 * D), 0.02),
        "attn_w_in": nrm(ks[6], (La, D, ATTN_IN_W), D ** -0.5),
        "attn_w_out": nrm(ks[7], (La, ATTN_OUT_W, D), ATTN_OUT_W ** -0.5),
        "lambda_q1": nrm(ks[8], (La, A_QK_DIM), 0.1),
        "lambda_k1": nrm(ks[9], (La, A_QK_DIM), 0.1),
        "lambda_q2": nrm(ks[10], (La, A_QK_DIM), 0.1),
        "lambda_k2": nrm(ks[11], (La, A_QK_DIM), 0.1),
        "subln_g": 1.0 + nrm(ks[12], (La, A_V_DIM), 0.05),
        "na_rpb": nrm(ks[13], (La, B_HEADS, 2 * NA_ROWS - 1, 2 * NA_COLS - 1), 0.5),
        "chunk_w_in": nrm(ks[14], (Lc, D, 2 * C_WIDTH), D ** -0.5),
        "chunk_b_in": nrm(ks[15], (Lc, 2 * C_WIDTH), 0.02),
        "chunk_ln_g": 1.0 + nrm(ks[16], (Lc, C_WIDTH), 0.05),
        "chunk_ln_b": nrm(ks[17], (Lc, C_WIDTH), 0.02),
        "chunk_w_s": nrm(ks[18], (Lc, C_GROUPS, CHUNK, CHUNK), CHUNK ** -0.5),
        "chunk_b_s": 1.0 + nrm(ks[19], (Lc, C_GROUPS, CHUNK), 0.05),
        "chunk_w_out": nrm(ks[20], (Lc, C_WIDTH, D), C_WIDTH ** -0.5),
        "peer_w_query": nrm(ks[21], (DEPTH, D, PEER_HEADS * PEER_KEY_DIM), D ** -0.5),
        "peer_sub_keys": nrm(ks[22], (DEPTH, PEER_HEADS, 2, N_KEYS, PEER_HALF), PEER_HALF ** -0.5),
        "peer_u": nrm(ks[23], (DEPTH, N_EXPERTS, D), D ** -0.5),
        "peer_v": nrm(ks[24], (DEPTH, N_EXPERTS, D), PEER_HEADS ** -0.5),
        "final_norm_g": 1.0 + nrm(ks[25], (D,), 0.05),
    }


def reference(x, c, ctx, c_ctx, w_mod, b_mod, attn_w_in, attn_w_out, lambda_q1, lambda_k1,
              lambda_q2, lambda_k2, subln_g, na_rpb, chunk_w_in, chunk_b_in, chunk_ln_g, chunk_ln_b,
              chunk_w_s, chunk_b_s, chunk_w_out, peer_w_query, peer_sub_keys, peer_u, peer_v,
              final_norm_g):
    h, hc = x, ctx
    for i in range(DEPTH):
        is_attn = i % 2 == 0
        need_ctx = any(j % 2 == 0 for j in range(i + 1, DEPTH))
        sh1, sc1, g1, sh2, sc2, g2 = [m[:, None, :] for m in adaln(c, w_mod[i], b_mod[i])]
        if is_attn or need_ctx:
            csh1, csc1, cg1, csh2, csc2, cg2 = adaln(c_ctx, w_mod[i], b_mod[i])
        if is_attn:
            a = i // 2
            lam_init = 0.8 - 0.6 * math.exp(-0.3 * i)
            lam = (jnp.exp(jnp.sum(lambda_q1[a] * lambda_k1[a]).astype(jnp.float32))
                   - jnp.exp(jnp.sum(lambda_q2[a] * lambda_k2[a]).astype(jnp.float32)) + lam_init)
            out_l, out_c = mixer_diff_na(modulate(h, sh1, sc1), modulate(hc, csh1, csc1),
                                         attn_w_in[a], attn_w_out[a], lam, lam_init, subln_g[a],
                                         na_rpb[a], need_ctx)
        else:
            m = i // 2
            cw = (chunk_w_in[m], chunk_b_in[m], chunk_ln_g[m], chunk_ln_b[m], chunk_w_s[m], chunk_b_s[m], chunk_w_out[m])
            out_l = chunk_spatial_gating(modulate(h, sh1, sc1), *cw)
            out_c = chunk_spatial_gating(modulate(hc, csh1, csc1), *cw) if need_ctx else None
        h = h + g1 * out_l
        h = h + g2 * peer(modulate(h, sh2, sc2), peer_w_query[i], peer_sub_keys[i], peer_u[i], peer_v[i])
        if need_ctx:
            hc = hc + cg1 * out_c
            hc = hc + cg2 * peer(modulate(hc, csh2, csc2), peer_w_query[i], peer_sub_keys[i], peer_u[i], peer_v[i])
    return rms_norm(h) * final_norm_g
```

```python
import functools
import math

import numpy as np
import jax
import jax.numpy as jnp
from jax import lax
from jax.experimental import pallas as pl
from jax.experimental.pallas import tpu as pltpu

F32 = jnp.float32
BF16 = jnp.bfloat16

EPS = 1e-6
GRID_W = 64
NA_ROWS = 8
NA_COLS = 16
ROPE_BASE = 10000.0
A_HEADS = 4
A_QK_DIM = 64
B_HEADS = 8
B_HEAD_DIM = 64
PEER_HEADS = 8
N_KEYS = 128
PEER_TOPK = 16
PEER_SEL = PEER_HEADS * PEER_TOPK
CHUNK = 128
C_GROUPS = 8
MASK_VALUE = -1e30

VMEM_LIMIT = 56 * 1024 * 1024


def _cparams(*sem):
    return pltpu.CompilerParams(dimension_semantics=sem, vmem_limit_bytes=VMEM_LIMIT)


def _rms(x):
    return x * lax.rsqrt(jnp.mean(x * x, axis=-1, keepdims=True) + EPS)


def _gelu(x):
    return jax.nn.gelu(x)


def _adaln_body(c_ref, w_ref, b_ref, o_ref):
    cnd = c_ref[...]
    a = (cnd * jax.nn.sigmoid(cnd)).astype(BF16)
    o_ref[...] = jnp.dot(a, w_ref[...].astype(BF16), preferred_element_type=F32) + b_ref[...]


def _adaln(cond, w, b):
    m, d = cond.shape
    n = w.shape[1]
    tn = 1024
    return pl.pallas_call(
        _adaln_body,
        grid=(n // tn,),
        in_specs=[pl.BlockSpec((m, d), lambda j: (0, 0)),
                  pl.BlockSpec((d, tn), lambda j: (0, j)),
                  pl.BlockSpec((1, tn), lambda j: (0, j))],
        out_specs=pl.BlockSpec((m, tn), lambda j: (0, j)),
        out_shape=jax.ShapeDtypeStruct((m, n), F32),
        compiler_params=_cparams("parallel"),
        name="adaln",
    )(cond, w, b.reshape(1, n))


def _mod_mm_body(*refs, has_delta, has_bias, rope_chunks, scale_chunks, act, emit_xm, tn, n_out):
    it = iter(refs)
    h_ref = next(it)
    delta_ref = gate_ref = None
    if has_delta:
        delta_ref, gate_ref = next(it), next(it)
    shift_ref, scale_ref, w_ref = next(it), next(it), next(it)
    b_ref = next(it) if has_bias else None
    cos_ref = sin_ref = None
    if rope_chunks:
        cos_ref, sin_ref = next(it), next(it)
    y_ref = next(it)
    hn_ref = next(it) if has_delta else None
    xm_ref = next(it) if emit_xm else None

    h = h_ref[...]
    if has_delta:
        h = h + gate_ref[0] * delta_ref[...]
        hn_ref[...] = h
    xm = _rms(h) * (1.0 + scale_ref[0]) + shift_ref[0]
    if emit_xm:
        xm_ref[...] = xm
    xb = xm.astype(BF16)
    for c in range(n_out // tn):
        cols = slice(c * tn, (c + 1) * tn)
        y = jnp.dot(xb, w_ref[:, cols], preferred_element_type=F32)
        if has_bias:
            y = y + b_ref[:, cols]
        if c in rope_chunks:
            lane = lax.broadcasted_iota(jnp.int32, y.shape, 1)
            partner = jnp.where((lane & 16) == 0, pltpu.roll(y, tn - 16, 1), pltpu.roll(y, 16, 1))
            y = y * cos_ref[...] + partner * sin_ref[...]
        if c in scale_chunks:
            y = y * 0.125
        if act == "gelu":
            y = _gelu(y)
        y_ref[:, cols] = y


def _mod_mm(h, shift, scale, w, *, rows_per_group, tm, delta=None, gate=None, bias=None, rope=None,
            rope_chunks=(), scale_chunks=(), act="none", emit_xm=False, name="mod_mm"):
    t, d = h.shape
    n_out = w.shape[1]
    tn = 512
    groups = shift.shape[0]
    assert t % tm == 0 and rows_per_group % tm == 0 and n_out % tn == 0

    def gidx(i):
        return ((i * tm) // rows_per_group if groups > 1 else 0, 0, 0)

    row_spec = pl.BlockSpec((tm, d), lambda i: (i, 0))
    mod_spec = pl.BlockSpec((1, 1, d), gidx)
    args, specs = [h], [row_spec]
    if delta is not None:
        args += [delta, gate]
        specs += [row_spec, mod_spec]
    args += [shift, scale, w]
    specs += [mod_spec, mod_spec, pl.BlockSpec((d, n_out), lambda i: (0, 0))]
    if bias is not None:
        args.append(bias.reshape(1, n_out))
        specs.append(pl.BlockSpec((1, n_out), lambda i: (0, 0)))
    if rope_chunks:
        cos, sin = rope
        pos_blocks = cos.shape[0] // tm
        args += [cos, sin]
        specs += [pl.BlockSpec((tm, tn), lambda i: (i % pos_blocks, 0))] * 2
    out_shape = [jax.ShapeDtypeStruct((t, n_out), F32)]
    out_specs = [pl.BlockSpec((tm, n_out), lambda i: (i, 0))]
    if delta is not None:
        out_shape.append(jax.ShapeDtypeStruct((t, d), F32))
        out_specs.append(row_spec)
    if emit_xm:
        out_shape.append(jax.ShapeDtypeStruct((t, d), F32))
        out_specs.append(row_spec)
    body = functools.partial(_mod_mm_body, has_delta=delta is not None, has_bias=bias is not None,
                             rope_chunks=tuple(rope_chunks), scale_chunks=tuple(scale_chunks), act=act,
                             emit_xm=emit_xm, tn=tn, n_out=n_out)
    return pl.pallas_call(body, grid=(t // tm,), in_specs=specs, out_specs=out_specs, out_shape=out_shape,
                          compiler_params=_cparams("parallel"), name=name)(*args)


def _rope_tables(n_tokens, width):
    axis_dim = A_QK_DIM // 2
    inv_freq = 1.0 / (ROPE_BASE ** (jnp.arange(0, axis_dim, 2, dtype=F32) / axis_dim))
    t = jnp.arange(n_tokens)
    row = (t // GRID_W).astype(F32)
    col = (t % GRID_W).astype(F32)
    ang_r = row[:, None] * inv_freq
    ang_c = col[:, None] * inv_freq
    ang = jnp.concatenate([ang_r, ang_r, ang_c, ang_c], axis=1)
    sign = jnp.tile(jnp.concatenate([-jnp.ones(16, F32), jnp.ones(16, F32)]), 2)
    reps = width // A_QK_DIM
    return jnp.tile(jnp.cos(ang), (1, reps)), jnp.tile(jnp.sin(ang) * sign, (1, reps))


def _diff_attn_body(lam_ref, q_ref, k_ref, v_ref, kc_ref, vc_ref, g_ref, o_ref, *, out_scale):
    lam = lam_ref[0, 0]
    q = q_ref[0]
    k = k_ref[0].astype(BF16)
    kc = kc_ref[0].astype(BF16)
    lane = lax.broadcasted_iota(jnp.int32, q.shape, 1)
    nt = (((1,), (1,)), ((), ()))
    parts = []
    for m in range(2):
        keep = (lane < A_QK_DIM) if m == 0 else (lane >= A_QK_DIM)
        qm = jnp.where(keep, q, 0.0).astype(BF16)
        s_l = lax.dot_general(qm, k, nt, preferred_element_type=F32)
        s_c = lax.dot_general(qm, kc, nt, preferred_element_type=F32)
        mx = jnp.maximum(jnp.max(s_l, axis=-1, keepdims=True), jnp.max(s_c, axis=-1, keepdims=True))
        e_l = jnp.exp(s_l - mx)
        e_c = jnp.exp(s_c - mx)
        inv = 1.0 / (jnp.sum(e_l, axis=-1, keepdims=True) + jnp.sum(e_c, axis=-1, keepdims=True))
        parts.append((e_l, e_c, inv))
    (e1l, e1c, inv1), (e2l, e2c, inv2) = parts
    w2 = lam * inv2
    a_l = (e1l * inv1 - e2l * w2).astype(BF16)
    a_c = (e1c * inv1 - e2c * w2).astype(BF16)
    o = (jnp.dot(a_l, v_ref[0].astype(BF16), preferred_element_type=F32)
         + jnp.dot(a_c, vc_ref[0].astype(BF16), preferred_element_type=F32))
    o_ref[0] = _rms(o) * g_ref[...] * out_scale


def _diff_attn(z, zc, lam, subln_g, lam_init, *, tq=256):
    b, n, _ = z.shape
    c = zc.shape[1]
    kcol, vcol = A_HEADS, 2 * A_HEADS
    return pl.pallas_call(
        functools.partial(_diff_attn_body, out_scale=1.0 - lam_init),
        grid=(b, A_HEADS, n // tq),
        in_specs=[pl.BlockSpec((1, 1), lambda bi, h, i: (0, 0), memory_space=pltpu.SMEM),
                  pl.BlockSpec((1, tq, 128), lambda bi, h, i: (bi, i, h)),
                  pl.BlockSpec((1, n, 128), lambda bi, h, i: (bi, 0, kcol + h)),
                  pl.BlockSpec((1, n, 128), lambda bi, h, i: (bi, 0, vcol + h)),
                  pl.BlockSpec((1, c, 128), lambda bi, h, i: (bi, 0, kcol + h)),
                  pl.BlockSpec((1, c, 128), lambda bi, h, i: (bi, 0, vcol + h)),
                  pl.BlockSpec((1, 128), lambda bi, h, i: (0, 0))],
        out_specs=pl.BlockSpec((1, tq, 128), lambda bi, h, i: (bi, i, h)),
        out_shape=jax.ShapeDtypeStruct((b, n, A_HEADS * 128), F32),
        compiler_params=_cparams("parallel", "parallel", "arbitrary"),
        name="diff_attn",
    )(lam.reshape(1, 1), z, z, z, zc, zc, subln_g.reshape(1, 128))


NA_QROWS = 8
NA_KROWS = 16
NA_SEG = 4 * GRID_W


def _na_window_start(j, rows):
    return jnp.clip(NA_QROWS * j - NA_ROWS // 2, 0, rows - NA_KROWS)


def _na_bias_tables(rpb, rows):
    n_blocks = rows // NA_QROWS
    tabs_idx, tabs_ok = [], []
    for j in (0, 1, n_blocks - 1):
        ks = int(np.clip(NA_QROWS * j - NA_ROWS // 2, 0, rows - NA_KROWS))
        r = (NA_QROWS * j + np.arange(NA_QROWS))[:, None, None, None]
        cq = np.arange(GRID_W)[None, :, None, None]
        kr = (ks + np.arange(NA_KROWS))[None, None, :, None]
        kc = np.arange(GRID_W)[None, None, None, :]
        r0 = np.clip(r - NA_ROWS // 2, 0, rows - NA_ROWS)
        c0 = np.clip(cq - NA_COLS // 2, 0, GRID_W - NA_COLS)
        ok = (kr >= r0) & (kr < r0 + NA_ROWS) & (kc >= c0) & (kc < c0 + NA_COLS)
        ir = np.clip(kr - r + (NA_ROWS - 1), 0, 2 * NA_ROWS - 2)
        ic = np.clip(kc - cq + (NA_COLS - 1), 0, 2 * NA_COLS - 2)
        flat = np.broadcast_to(ir * (2 * NA_COLS - 1) + ic, ok.shape)
        tabs_idx.append(flat.reshape(NA_QROWS * GRID_W, NA_KROWS * GRID_W))
        tabs_ok.append(ok.reshape(NA_QROWS * GRID_W, NA_KROWS * GRID_W))
    idx = np.stack(tabs_idx).astype(np.int32)
    ok = np.stack(tabs_ok)
    h = rpb.shape[0]
    vals = rpb.reshape(h, -1)[:, idx]
    return jnp.where(ok[None], vals, MASK_VALUE)


def _na_body(q_ref, k0, k1, k2, k3, v0, v1, v2, v3, kc_ref, vc_ref, bias_ref, o_ref):
    q = q_ref[0]
    lane = lax.broadcasted_iota(jnp.int32, q.shape, 1)
    nt = (((1,), (1,)), ((), ()))
    ks = [r[0].astype(BF16) for r in (k0, k1, k2, k3)]
    vs = [r[0].astype(BF16) for r in (v0, v1, v2, v3)]
    kc = kc_ref[0].astype(BF16)
    vc = vc_ref[0].astype(BF16)
    outs = []
    for hh in range(2):
        keep = (lane < B_HEAD_DIM) if hh == 0 else (lane >= B_HEAD_DIM)
        qh = jnp.where(keep, q, 0.0).astype(BF16)
        segs = [lax.dot_general(qh, ks[i], nt, preferred_element_type=F32)
                + bias_ref[hh, 0, :, i * NA_SEG:(i + 1) * NA_SEG] for i in range(4)]
        s_c = lax.dot_general(qh, kc, nt, preferred_element_type=F32)
        mx = jnp.max(s_c, axis=-1, keepdims=True)
        for s in segs:
            mx = jnp.maximum(mx, jnp.max(s, axis=-1, keepdims=True))
        e_c = jnp.exp(s_c - mx)
        den = jnp.sum(e_c, axis=-1, keepdims=True)
        es = []
        for s in segs:
            e = jnp.exp(s - mx)
            den = den + jnp.sum(e, axis=-1, keepdims=True)
            es.append(e)
        inv = 1.0 / den
        o = jnp.dot((e_c * inv).astype(BF16), vc, preferred_element_type=F32)
        for e, v in zip(es, vs):
            o = o + jnp.dot((e * inv).astype(BF16), v, preferred_element_type=F32)
        outs.append(o)
    o_ref[0] = jnp.where(lane < B_HEAD_DIM, outs[0], outs[1])


def _na_attn(z, zc, bias_tabs):
    b, n, _ = z.shape
    c = zc.shape[1]
    rows = n // GRID_W
    n_blocks = rows // NA_QROWS
    tq = NA_QROWS * GRID_W
    qcol, kcol, vcol = 12, 16, 20
    seg_rows = NA_SEG // GRID_W

    def kv_spec(col0, i):
        return pl.BlockSpec((1, NA_SEG, 128),
                            lambda hp, j, bi: (bi, _na_window_start(j, rows) // seg_rows + i, col0 + hp))

    def variant(j):
        return jnp.where(j == 0, 0, jnp.where(j == n_blocks - 1, 2, 1))

    in_specs = [pl.BlockSpec((1, tq, 128), lambda hp, j, bi: (bi, j, qcol + hp))]
    in_specs += [kv_spec(kcol, i) for i in range(4)]
    in_specs += [kv_spec(vcol, i) for i in range(4)]
    in_specs += [pl.BlockSpec((1, c, 128), lambda hp, j, bi: (bi, 0, kcol + hp)),
                 pl.BlockSpec((1, c, 128), lambda hp, j, bi: (bi, 0, vcol + hp)),
                 pl.BlockSpec((2, 1, tq, NA_KROWS * GRID_W), lambda hp, j, bi: (hp, variant(j), 0, 0))]
    return pl.pallas_call(
        _na_body,
        grid=(B_HEADS // 2, n_blocks, b),
        in_specs=in_specs,
        out_specs=pl.BlockSpec((1, tq, 128), lambda hp, j, bi: (bi, j, hp)),
        out_shape=jax.ShapeDtypeStruct((b, n, B_HEADS * B_HEAD_DIM), F32),
        compiler_params=_cparams("parallel", "parallel", "arbitrary"),
        name="na_attn",
    )(z, z, z, z, z, z, z, z, z, zc, zc, bias_tabs)


def _out_proj_body(h_ref, gate_ref, xa_ref, xb_ref, wa_ref, wb_ref, o_ref):
    y = (jnp.dot(xa_ref[...].astype(BF16), wa_ref[...], preferred_element_type=F32)
         + jnp.dot(xb_ref[...].astype(BF16), wb_ref[...], preferred_element_type=F32))
    o_ref[...] = h_ref[...] + gate_ref[0] * y


def _out_proj(h, gate, xa, xb, wa, wb, *, rows_per_group, tm=512):
    t, d = h.shape
    ka, kb = xa.shape[1], xb.shape[1]
    return pl.pallas_call(
        _out_proj_body,
        grid=(t // tm,),
        in_specs=[pl.BlockSpec((tm, d), lambda i: (i, 0)),
                  pl.BlockSpec((1, 1, d), lambda i: ((i * tm) // rows_per_group, 0, 0)),
                  pl.BlockSpec((tm, ka), lambda i: (i, 0)),
                  pl.BlockSpec((tm, kb), lambda i: (i, 0)),
                  pl.BlockSpec((ka, d), lambda i: (0, 0)),
                  pl.BlockSpec((kb, d), lambda i: (0, 0))],
        out_specs=pl.BlockSpec((tm, d), lambda i: (i, 0)),
        out_shape=jax.ShapeDtypeStruct((t, d), F32),
        compiler_params=_cparams("parallel"),
        name="out_proj",
    )(h, gate, xa, xb, wa, wb)


def _gmlp_tail_body(z_ref, h_ref, gate_ref, lng_ref, lnb_ref, ws_ref, bs_ref, wo_ref, o_ref, gated_ref, *, tm, width):
    gw = width // C_GROUPS
    for ci in range(tm // CHUNK):
        rows = slice(ci * CHUNK, (ci + 1) * CHUNK)
        v = z_ref[rows, width:]
        mu = jnp.mean(v, axis=-1, keepdims=True)
        var = jnp.mean(jnp.square(v - mu), axis=-1, keepdims=True)
        vn = ((v - mu) * lax.rsqrt(var + EPS)) * lng_ref[...] + lnb_ref[...]
        vb = vn.astype(BF16)
        for g in range(C_GROUPS):
            cols = slice(g * gw, (g + 1) * gw)
            sv = jnp.dot(ws_ref[g], vb[:, cols], preferred_element_type=F32) + bs_ref[g]
            gated_ref[rows, cols] = (z_ref[rows, cols] * sv).astype(BF16)
    y = jnp.dot(gated_ref[...], wo_ref[...], preferred_element_type=F32)
    o_ref[...] = h_ref[...] + gate_ref[0] * y


def _gmlp_tail(z, h, gate, ln_g, ln_b, w_s, b_s, w_out, *, rows_per_group, tm=512):
    t, d = h.shape
    width = z.shape[1] // 2
    gw = width // C_GROUPS
    bsb = jnp.broadcast_to(b_s[:, :, None], (C_GROUPS, CHUNK, gw))
    return pl.pallas_call(
        functools.partial(_gmlp_tail_body, tm=tm, width=width),
        grid=(t // tm,),
        in_specs=[pl.BlockSpec((tm, 2 * width), lambda i: (i, 0)),
                  pl.BlockSpec((tm, d), lambda i: (i, 0)),
                  pl.BlockSpec((1, 1, d), lambda i: ((i * tm) // rows_per_group, 0, 0)),
                  pl.BlockSpec((1, width), lambda i: (0, 0)),
                  pl.BlockSpec((1, width), lambda i: (0, 0)),
                  pl.BlockSpec((C_GROUPS, CHUNK, CHUNK), lambda i: (0, 0, 0)),
                  pl.BlockSpec((C_GROUPS, CHUNK, gw), lambda i: (0, 0, 0)),
                  pl.BlockSpec((width, d), lambda i: (0, 0))],
        out_specs=pl.BlockSpec((tm, d), lambda i: (i, 0)),
        out_shape=jax.ShapeDtypeStruct((t, d), F32),
        scratch_shapes=[pltpu.VMEM((tm, width), BF16)],
        compiler_params=_cparams("parallel"),
        name="gmlp_tail",
    )(z, h, gate, ln_g.reshape(1, width), ln_b.reshape(1, width), w_s.astype(BF16), bsb, w_out)


def _topk_rows(s, payload, k):
    n_rows = s.shape[0]
    riota = lax.broadcasted_iota(jnp.int32, s.shape, 0)
    vals, pays = [], []
    for _ in range(k):
        m = jnp.max(s, axis=0, keepdims=True)
        first = jnp.min(jnp.where(s == m, riota, n_rows), axis=0, keepdims=True)
        sel = riota == first
        vals.append(m)
        pays.append(jnp.max(jnp.where(sel, payload, -1), axis=0, keepdims=True))
        s = jnp.where(sel, -jnp.inf, s)
    return jnp.concatenate(vals, axis=0), jnp.concatenate(pays, axis=0)


def _peer_topk_body(q_ref, keys_ref, e_ref, g_ref, *, tt):
    nt = (((1,), (1,)), ((), ()))
    kiota = lax.broadcasted_iota(jnp.int32, (N_KEYS, tt), 0)
    for h in range(PEER_HEADS):
        halves = []
        for p in range(2):
            hp = 2 * h + p
            qhp = q_ref[:, hp * 128:(hp + 1) * 128].astype(BF16)
            s = lax.dot_general(keys_ref[hp], qhp, nt, preferred_element_type=F32)
            halves.append(_topk_rows(s, kiota, PEER_TOPK))
        (s1, i1), (s2, i2) = halves
        cand = jnp.concatenate([s1[a:a + 1] + s2 for a in range(PEER_TOPK)], axis=0)
        expert = jnp.concatenate([i1[a:a + 1] * N_KEYS + i2 for a in range(PEER_TOPK)], axis=0)
        sc, e = _topk_rows(cand, expert, PEER_TOPK)
        ex = jnp.exp(sc - sc[0:1])
        g = ex / jnp.sum(ex, axis=0, keepdims=True)
        e_ref[h * PEER_TOPK:(h + 1) * PEER_TOPK, :] = e
        g_ref[h * PEER_TOPK:(h + 1) * PEER_TOPK, :] = g


def _peer_topk(q, sub_keys, *, tt=128):
    t, qw = q.shape
    keys = sub_keys.reshape(2 * PEER_HEADS, N_KEYS, qw // (2 * PEER_HEADS)).astype(BF16)
    return pl.pallas_call(
        functools.partial(_peer_topk_body, tt=tt),
        grid=(t // tt,),
        in_specs=[pl.BlockSpec((tt, qw), lambda i: (i, 0)),
                  pl.BlockSpec(keys.shape, lambda i: (0, 0, 0))],
        out_specs=[pl.BlockSpec((PEER_SEL, tt), lambda i: (0, i)),
                   pl.BlockSpec((PEER_SEL, tt), lambda i: (0, i))],
        out_shape=[jax.ShapeDtypeStruct((PEER_SEL, t), jnp.int32),
                   jax.ShapeDtypeStruct((PEER_SEL, t), F32)],
        compiler_params=_cparams("parallel"),
        name="peer_topk",
    )(q, keys)


PEER_TB = 8


def _peer_expert_body(e_cur, e_nxt, g_ref, x_ref, u_hbm, v_hbm, o_ref, ubuf, vbuf, sem, *, n_steps):
    i = pl.program_id(0)
    n_rows = PEER_TB * PEER_SEL
    slot = i % 2

    def row_copies(e_ref, r, dst_slot):
        idx = e_ref[r // PEER_SEL, r % PEER_SEL]
        return (pltpu.make_async_copy(u_hbm.at[pl.ds(idx, 1)], ubuf.at[dst_slot, pl.ds(r, 1)], sem.at[0, dst_slot]),
                pltpu.make_async_copy(v_hbm.at[pl.ds(idx, 1)], vbuf.at[dst_slot, pl.ds(r, 1)], sem.at[1, dst_slot]))

    def issue(e_ref, dst_slot):
        def one(r, carry):
            cu, cv = row_copies(e_ref, r, dst_slot)
            cu.start()
            cv.start()
            return carry
        lax.fori_loop(0, n_rows, one, 0)

    def drain(e_ref, dst_slot):
        def one(r, carry):
            cu, cv = row_copies(e_ref, r, dst_slot)
            cu.wait()
            cv.wait()
            return carry
        lax.fori_loop(0, n_rows, one, 0)

    @pl.when(i == 0)
    def _():
        issue(e_cur, slot)

    @pl.when(i + 1 < n_steps)
    def _():
        issue(e_nxt, 1 - slot)

    drain(e_cur, slot)

    for t in range(PEER_TB):
        rows = slice(t * PEER_SEL, (t + 1) * PEER_SEL)
        xt = x_ref[t:t + 1, :]
        s = jnp.sum(ubuf[slot, rows, :] * xt, axis=-1, keepdims=True)
        w = g_ref[t] * _gelu(s)
        o_ref[t:t + 1, :] = jnp.sum(w * vbuf[slot, rows, :], axis=0, keepdims=True)


def _peer_experts(e, g, xm, u, v):
    t, d = xm.shape
    n_steps = t // PEER_TB
    g3 = g.reshape(t, PEER_SEL, 1)
    last = n_steps - 1
    return pl.pallas_call(
        functools.partial(_peer_expert_body, n_steps=n_steps),
        grid=(n_steps,),
        in_specs=[pl.BlockSpec((PEER_TB, PEER_SEL), lambda i: (i, 0), memory_space=pltpu.SMEM),
                  pl.BlockSpec((PEER_TB, PEER_SEL), lambda i: (jnp.minimum(i + 1, last), 0), memory_space=pltpu.SMEM),
                  pl.BlockSpec((PEER_TB, PEER_SEL, 1), lambda i: (i, 0, 0)),
                  pl.BlockSpec((PEER_TB, d), lambda i: (i, 0)),
                  pl.BlockSpec(memory_space=pl.ANY),
                  pl.BlockSpec(memory_space=pl.ANY)],
        out_specs=pl.BlockSpec((PEER_TB, d), lambda i: (i, 0)),
        out_shape=jax.ShapeDtypeStruct((t, d), F32),
        scratch_shapes=[pltpu.VMEM((2, PEER_TB * PEER_SEL, d), F32),
                        pltpu.VMEM((2, PEER_TB * PEER_SEL, d), F32),
                        pltpu.SemaphoreType.DMA((2, 2))],
        compiler_params=_cparams("arbitrary"),
        name="peer_experts",
    )(e, e, g3, xm, u, v)


def _peer(q, xm, sub_keys, u, v):
    e_t, g_t = _peer_topk(q, sub_keys)
    return _peer_experts(e_t.T, g_t.T, xm, u, v)


def _final_body(h_ref, delta_ref, gate_ref, g_ref, o_ref):
    h = h_ref[...] + gate_ref[0] * delta_ref[...]
    o_ref[...] = _rms(h) * g_ref[...]


def _final(h, delta, gate, gain, *, rows_per_group, tm=512):
    t, d = h.shape
    row = pl.BlockSpec((tm, d), lambda i: (i, 0))
    return pl.pallas_call(
        _final_body,
        grid=(t // tm,),
        in_specs=[row, row, pl.BlockSpec((1, 1, d), lambda i: ((i * tm) // rows_per_group, 0, 0)),
                  pl.BlockSpec((1, d), lambda i: (0, 0))],
        out_specs=row,
        out_shape=jax.ShapeDtypeStruct((t, d), F32),
        compiler_params=_cparams("parallel"),
        name="final_norm",
    )(h, delta, gate, gain.reshape(1, d))


def kernel(x, c, ctx, c_ctx, w_mod, b_mod, attn_w_in, attn_w_out, lambda_q1, lambda_k1, lambda_q2, lambda_k2,
           subln_g, na_rpb, chunk_w_in, chunk_b_in, chunk_ln_g, chunk_ln_b, chunk_w_s, chunk_b_s, chunk_w_out,
           peer_w_query, peer_sub_keys, peer_u, peer_v, final_norm_g):
    b, n, d = x.shape
    n_ctx = ctx.shape[1]
    t = b * n
    x2 = x.reshape(t, d)

    cond = jnp.concatenate([c, c_ctx[None], jnp.zeros((-(b + 1) % 8, d), F32)], axis=0)
    mods = [_adaln(cond, w_mod[i], b_mod[i]) for i in range(2)]

    def lat(m, k):
        return m[:b, k * d:(k + 1) * d].reshape(b, 1, d)

    def cx(m, k):
        return m[b:b + 1, k * d:(k + 1) * d].reshape(1, 1, d)

    m0 = mods[0]
    w_in = attn_w_in[0].astype(BF16)
    rope = _rope_tables(n, 512)
    z = _mod_mm(x2, lat(m0, 0), lat(m0, 1), w_in, rows_per_group=n, tm=256, rope=rope, rope_chunks=(0, 1),
                scale_chunks=(0, 3), name="attn_in")[0]
    zc = _mod_mm(ctx.reshape(b * n_ctx, d), cx(m0, 0), cx(m0, 1), w_in, rows_per_group=n_ctx, tm=256,
                 name="attn_in_ctx")[0]
    z3 = z.reshape(b, n, -1)
    zc3 = zc.reshape(b, n_ctx, -1)
    lam_init = 0.8 - 0.6 * math.exp(-0.3 * 0)
    lam = (jnp.exp(jnp.sum(lambda_q1[0] * lambda_k1[0])) - jnp.exp(jnp.sum(lambda_q2[0] * lambda_k2[0])) + lam_init)
    oa = _diff_attn(z3, zc3, lam.astype(F32), subln_g[0], lam_init)
    ob = _na_attn(z3, zc3, _na_bias_tables(na_rpb[0], n // GRID_W))
    w_out = attn_w_out[0].astype(BF16)
    ka = oa.shape[-1]
    h1 = _out_proj(x2, lat(m0, 2), oa.reshape(t, ka), ob.reshape(t, -1), w_out[:ka], w_out[ka:], rows_per_group=n)
    q0, xm0 = _mod_mm(h1, lat(m0, 3), lat(m0, 4), peer_w_query[0].astype(BF16), rows_per_group=n, tm=256,
                      emit_xm=True, name="peer_q0")
    p0 = _peer(q0, xm0, peer_sub_keys[0], peer_u[0], peer_v[0])

    m1 = mods[1]
    zg, h1b = _mod_mm(h1, lat(m1, 0), lat(m1, 1), chunk_w_in[0].astype(BF16), rows_per_group=n, tm=256,
                      delta=p0, gate=lat(m0, 5), bias=chunk_b_in[0], act="gelu", name="gmlp_in")
    h2 = _gmlp_tail(zg, h1b, lat(m1, 2), chunk_ln_g[0], chunk_ln_b[0], chunk_w_s[0], chunk_b_s[0],
                    chunk_w_out[0].astype(BF16), rows_per_group=n)
    q1, xm1 = _mod_mm(h2, lat(m1, 3), lat(m1, 4), peer_w_query[1].astype(BF16), rows_per_group=n, tm=256,
                      emit_xm=True, name="peer_q1")
    p1 = _peer(q1, xm1, peer_sub_keys[1], peer_u[1], peer_v[1])
    out = _final(h2, p1, lat(m1, 5), final_norm_g, rows_per_group=n)
    return out.reshape(b, n, d)
```

```python
import functools
import math

import numpy as np
import jax
import jax.numpy as jnp
from jax import lax
from jax.experimental import pallas as pl
from jax.experimental.pallas import tpu as pltpu
from jax.experimental.pallas import tpu_sc as plsc

F32 = jnp.float32
BF16 = jnp.bfloat16

EPS = 1e-6
GRID_W = 64
NA_ROWS = 8
NA_COLS = 16
ROPE_BASE = 10000.0
A_HEADS = 4
A_QK_DIM = 64
B_HEADS = 8
B_HEAD_DIM = 64
PEER_HEADS = 8
N_KEYS = 128
PEER_TOPK = 16
PEER_SEL = PEER_HEADS * PEER_TOPK
CHUNK = 128
C_GROUPS = 8
MASK_VALUE = -1e30

VMEM_LIMIT = 56 * 1024 * 1024


def _cparams(*sem):
    return pltpu.CompilerParams(dimension_semantics=sem, vmem_limit_bytes=VMEM_LIMIT)


def _rms(x):
    return x * lax.rsqrt(jnp.mean(x * x, axis=-1, keepdims=True) + EPS)


def _gelu(x):
    return jax.nn.gelu(x)


def _adaln_body(c_ref, w_ref, b_ref, o_ref):
    cnd = c_ref[...]
    a = (cnd * jax.nn.sigmoid(cnd)).astype(BF16)
    o_ref[...] = jnp.dot(a, w_ref[...].astype(BF16), preferred_element_type=F32) + b_ref[...]


def _adaln(cond, w, b):
    m, d = cond.shape
    n = w.shape[1]
    tn = 1024
    return pl.pallas_call(
        _adaln_body,
        grid=(n // tn,),
        in_specs=[pl.BlockSpec((m, d), lambda j: (0, 0)),
                  pl.BlockSpec((d, tn), lambda j: (0, j)),
                  pl.BlockSpec((1, tn), lambda j: (0, j))],
        out_specs=pl.BlockSpec((m, tn), lambda j: (0, j)),
        out_shape=jax.ShapeDtypeStruct((m, n), F32),
        compiler_params=_cparams("parallel"),
        name="adaln",
    )(cond, w, b.reshape(1, n))


def _mod_mm_body(*refs, has_delta, has_bias, rope_chunks, scale_chunks, act, emit_xm, tn, n_out):
    it = iter(refs)
    h_ref = next(it)
    delta_ref = gate_ref = None
    if has_delta:
        delta_ref, gate_ref = next(it), next(it)
    shift_ref, scale_ref, w_ref = next(it), next(it), next(it)
    b_ref = next(it) if has_bias else None
    cos_ref = sin_ref = None
    if rope_chunks:
        cos_ref, sin_ref = next(it), next(it)
    y_ref = next(it)
    hn_ref = next(it) if has_delta else None
    xm_ref = next(it) if emit_xm else None

    h = h_ref[...]
    if has_delta:
        h = h + gate_ref[0] * delta_ref[...]
        hn_ref[...] = h
    xm = _rms(h) * (1.0 + scale_ref[0]) + shift_ref[0]
    if emit_xm:
        xm_ref[...] = xm
    xb = xm.astype(BF16)
    for c in range(n_out // tn):
        cols = slice(c * tn, (c + 1) * tn)
        y = jnp.dot(xb, w_ref[:, cols], preferred_element_type=F32)
        if has_bias:
            y = y + b_ref[:, cols]
        if c in rope_chunks:
            lane = lax.broadcasted_iota(jnp.int32, y.shape, 1)
            partner = jnp.where((lane & 16) == 0, pltpu.roll(y, tn - 16, 1), pltpu.roll(y, 16, 1))
            y = y * cos_ref[...] + partner * sin_ref[...]
        if c in scale_chunks:
            y = y * 0.125
        if act == "gelu":
            y = _gelu(y)
        y_ref[:, cols] = y


def _mod_mm(h, shift, scale, w, *, rows_per_group, tm, delta=None, gate=None, bias=None, rope=None,
            rope_chunks=(), scale_chunks=(), act="none", emit_xm=False, name="mod_mm"):
    t, d = h.shape
    n_out = w.shape[1]
    tn = 512
    groups = shift.shape[0]
    assert t % tm == 0 and rows_per_group % tm == 0 and n_out % tn == 0

    def gidx(i):
        return ((i * tm) // rows_per_group if groups > 1 else 0, 0, 0)

    row_spec = pl.BlockSpec((tm, d), lambda i: (i, 0))
    mod_spec = pl.BlockSpec((1, 1, d), gidx)
    args, specs = [h], [row_spec]
    if delta is not None:
        args += [delta, gate]
        specs += [row_spec, mod_spec]
    args += [shift, scale, w]
    specs += [mod_spec, mod_spec, pl.BlockSpec((d, n_out), lambda i: (0, 0))]
    if bias is not None:
        args.append(bias.reshape(1, n_out))
        specs.append(pl.BlockSpec((1, n_out), lambda i: (0, 0)))
    if rope_chunks:
        cos, sin = rope
        pos_blocks = cos.shape[0] // tm
        args += [cos, sin]
        specs += [pl.BlockSpec((tm, tn), lambda i: (i % pos_blocks, 0))] * 2
    out_shape = [jax.ShapeDtypeStruct((t, n_out), F32)]
    out_specs = [pl.BlockSpec((tm, n_out), lambda i: (i, 0))]
    if delta is not None:
        out_shape.append(jax.ShapeDtypeStruct((t, d), F32))
        out_specs.append(row_spec)
    if emit_xm:
        out_shape.append(jax.ShapeDtypeStruct((t, d), F32))
        out_specs.append(row_spec)
    body = functools.partial(_mod_mm_body, has_delta=delta is not None, has_bias=bias is not None,
                             rope_chunks=tuple(rope_chunks), scale_chunks=tuple(scale_chunks), act=act,
                             emit_xm=emit_xm, tn=tn, n_out=n_out)
    return pl.pallas_call(body, grid=(t // tm,), in_specs=specs, out_specs=out_specs, out_shape=out_shape,
                          compiler_params=_cparams("parallel"), name=name)(*args)


def _rope_tables(n_tokens, width):
    axis_dim = A_QK_DIM // 2
    inv_freq = 1.0 / (ROPE_BASE ** (jnp.arange(0, axis_dim, 2, dtype=F32) / axis_dim))
    t = jnp.arange(n_tokens)
    row = (t // GRID_W).astype(F32)
    col = (t % GRID_W).astype(F32)
    ang_r = row[:, None] * inv_freq
    ang_c = col[:, None] * inv_freq
    ang = jnp.concatenate([ang_r, ang_r, ang_c, ang_c], axis=1)
    sign = jnp.tile(jnp.concatenate([-jnp.ones(16, F32), jnp.ones(16, F32)]), 2)
    reps = width // A_QK_DIM
    return jnp.tile(jnp.cos(ang), (1, reps)), jnp.tile(jnp.sin(ang) * sign, (1, reps))


def _diff_attn_body(lam_ref, q_ref, k_ref, v_ref, kc_ref, vc_ref, g_ref, o_ref, *, out_scale):
    lam = lam_ref[0, 0]
    q = q_ref[0]
    k = k_ref[0].astype(BF16)
    kc = kc_ref[0].astype(BF16)
    lane = lax.broadcasted_iota(jnp.int32, q.shape, 1)
    nt = (((1,), (1,)), ((), ()))
    parts = []
    for m in range(2):
        keep = (lane < A_QK_DIM) if m == 0 else (lane >= A_QK_DIM)
        qm = jnp.where(keep, q, 0.0).astype(BF16)
        s_l = lax.dot_general(qm, k, nt, preferred_element_type=F32)
        s_c = lax.dot_general(qm, kc, nt, preferred_element_type=F32)
        mx = jnp.maximum(jnp.max(s_l, axis=-1, keepdims=True), jnp.max(s_c, axis=-1, keepdims=True))
        e_l = jnp.exp(s_l - mx)
        e_c = jnp.exp(s_c - mx)
        inv = 1.0 / (jnp.sum(e_l, axis=-1, keepdims=True) + jnp.sum(e_c, axis=-1, keepdims=True))
        parts.append((e_l, e_c, inv))
    (e1l, e1c, inv1), (e2l, e2c, inv2) = parts
    w2 = lam * inv2
    a_l = (e1l * inv1 - e2l * w2).astype(BF16)
    a_c = (e1c * inv1 - e2c * w2).astype(BF16)
    o = (jnp.dot(a_l, v_ref[0].astype(BF16), preferred_element_type=F32)
         + jnp.dot(a_c, vc_ref[0].astype(BF16), preferred_element_type=F32))
    o_ref[0] = _rms(o) * g_ref[...] * out_scale


def _diff_attn(z, zc, lam, subln_g, lam_init, *, tq=256):
    b, n, _ = z.shape
    c = zc.shape[1]
    kcol, vcol = A_HEADS, 2 * A_HEADS
    return pl.pallas_call(
        functools.partial(_diff_attn_body, out_scale=1.0 - lam_init),
        grid=(b, A_HEADS, n // tq),
        in_specs=[pl.BlockSpec((1, 1), lambda bi, h, i: (0, 0), memory_space=pltpu.SMEM),
                  pl.BlockSpec((1, tq, 128), lambda bi, h, i: (bi, i, h)),
                  pl.BlockSpec((1, n, 128), lambda bi, h, i: (bi, 0, kcol + h)),
                  pl.BlockSpec((1, n, 128), lambda bi, h, i: (bi, 0, vcol + h)),
                  pl.BlockSpec((1, c, 128), lambda bi, h, i: (bi, 0, kcol + h)),
                  pl.BlockSpec((1, c, 128), lambda bi, h, i: (bi, 0, vcol + h)),
                  pl.BlockSpec((1, 128), lambda bi, h, i: (0, 0))],
        out_specs=pl.BlockSpec((1, tq, 128), lambda bi, h, i: (bi, i, h)),
        out_shape=jax.ShapeDtypeStruct((b, n, A_HEADS * 128), F32),
        compiler_params=_cparams("parallel", "parallel", "arbitrary"),
        name="diff_attn",
    )(lam.reshape(1, 1), z, z, z, zc, zc, subln_g.reshape(1, 128))


NA_QROWS = 8
NA_KROWS = 16
NA_SEG = 4 * GRID_W


def _na_window_start(j, rows):
    return jnp.clip(NA_QROWS * j - NA_ROWS // 2, 0, rows - NA_KROWS)


def _na_bias_tables(rpb, rows):
    n_blocks = rows // NA_QROWS
    tabs_idx, tabs_ok = [], []
    for j in (0, 1, n_blocks - 1):
        ks = int(np.clip(NA_QROWS * j - NA_ROWS // 2, 0, rows - NA_KROWS))
        r = (NA_QROWS * j + np.arange(NA_QROWS))[:, None, None, None]
        cq = np.arange(GRID_W)[None, :, None, None]
        kr = (ks + np.arange(NA_KROWS))[None, None, :, None]
        kc = np.arange(GRID_W)[None, None, None, :]
        r0 = np.clip(r - NA_ROWS // 2, 0, rows - NA_ROWS)
        c0 = np.clip(cq - NA_COLS // 2, 0, GRID_W - NA_COLS)
        ok = (kr >= r0) & (kr < r0 + NA_ROWS) & (kc >= c0) & (kc < c0 + NA_COLS)
        ir = np.clip(kr - r + (NA_ROWS - 1), 0, 2 * NA_ROWS - 2)
        ic = np.clip(kc - cq + (NA_COLS - 1), 0, 2 * NA_COLS - 2)
        flat = np.broadcast_to(ir * (2 * NA_COLS - 1) + ic, ok.shape)
        tabs_idx.append(flat.reshape(NA_QROWS * GRID_W, NA_KROWS * GRID_W))
        tabs_ok.append(ok.reshape(NA_QROWS * GRID_W, NA_KROWS * GRID_W))
    idx = np.stack(tabs_idx).astype(np.int32)
    ok = np.stack(tabs_ok)
    h = rpb.shape[0]
    vals = rpb.reshape(h, -1)[:, idx]
    return jnp.where(ok[None], vals, MASK_VALUE)


def _na_body(q_ref, k0, k1, k2, k3, v0, v1, v2, v3, kc_ref, vc_ref, bias_ref, o_ref):
    q = q_ref[0]
    lane = lax.broadcasted_iota(jnp.int32, q.shape, 1)
    nt = (((1,), (1,)), ((), ()))
    ks = [r[0].astype(BF16) for r in (k0, k1, k2, k3)]
    vs = [r[0].astype(BF16) for r in (v0, v1, v2, v3)]
    kc = kc_ref[0].astype(BF16)
    vc = vc_ref[0].astype(BF16)
    outs = []
    for hh in range(2):
        keep = (lane < B_HEAD_DIM) if hh == 0 else (lane >= B_HEAD_DIM)
        qh = jnp.where(keep, q, 0.0).astype(BF16)
        segs = [lax.dot_general(qh, ks[i], nt, preferred_element_type=F32)
                + bias_ref[hh, 0, :, i * NA_SEG:(i + 1) * NA_SEG] for i in range(4)]
        s_c = lax.dot_general(qh, kc, nt, preferred_element_type=F32)
        mx = jnp.max(s_c, axis=-1, keepdims=True)
        for s in segs:
            mx = jnp.maximum(mx, jnp.max(s, axis=-1, keepdims=True))
        e_c = jnp.exp(s_c - mx)
        den = jnp.sum(e_c, axis=-1, keepdims=True)
        es = []
        for s in segs:
            e = jnp.exp(s - mx)
            den = den + jnp.sum(e, axis=-1, keepdims=True)
            es.append(e)
        inv = 1.0 / den
        o = jnp.dot((e_c * inv).astype(BF16), vc, preferred_element_type=F32)
        for e, v in zip(es, vs):
            o = o + jnp.dot((e * inv).astype(BF16), v, preferred_element_type=F32)
        outs.append(o)
    o_ref[0] = jnp.where(lane < B_HEAD_DIM, outs[0], outs[1])


def _na_attn(z, zc, bias_tabs):
    b, n, _ = z.shape
    c = zc.shape[1]
    rows = n // GRID_W
    n_blocks = rows // NA_QROWS
    tq = NA_QROWS * GRID_W
    qcol, kcol, vcol = 12, 16, 20
    seg_rows = NA_SEG // GRID_W

    def kv_spec(col0, i):
        return pl.BlockSpec((1, NA_SEG, 128),
                            lambda hp, j, bi: (bi, _na_window_start(j, rows) // seg_rows + i, col0 + hp))

    def variant(j):
        return jnp.where(j == 0, 0, jnp.where(j == n_blocks - 1, 2, 1))

    in_specs = [pl.BlockSpec((1, tq, 128), lambda hp, j, bi: (bi, j, qcol + hp))]
    in_specs += [kv_spec(kcol, i) for i in range(4)]
    in_specs += [kv_spec(vcol, i) for i in range(4)]
    in_specs += [pl.BlockSpec((1, c, 128), lambda hp, j, bi: (bi, 0, kcol + hp)),
                 pl.BlockSpec((1, c, 128), lambda hp, j, bi: (bi, 0, vcol + hp)),
                 pl.BlockSpec((2, 1, tq, NA_KROWS * GRID_W), lambda hp, j, bi: (hp, variant(j), 0, 0))]
    return pl.pallas_call(
        _na_body,
        grid=(B_HEADS // 2, n_blocks, b),
        in_specs=in_specs,
        out_specs=pl.BlockSpec((1, tq, 128), lambda hp, j, bi: (bi, j, hp)),
        out_shape=jax.ShapeDtypeStruct((b, n, B_HEADS * B_HEAD_DIM), F32),
        compiler_params=_cparams("parallel", "parallel", "arbitrary"),
        name="na_attn",
    )(z, z, z, z, z, z, z, z, z, zc, zc, bias_tabs)


def _out_proj_body(h_ref, gate_ref, xa_ref, xb_ref, wa_ref, wb_ref, o_ref):
    y = (jnp.dot(xa_ref[...].astype(BF16), wa_ref[...], preferred_element_type=F32)
         + jnp.dot(xb_ref[...].astype(BF16), wb_ref[...], preferred_element_type=F32))
    o_ref[...] = h_ref[...] + gate_ref[0] * y


def _out_proj(h, gate, xa, xb, wa, wb, *, rows_per_group, tm=512):
    t, d = h.shape
    ka, kb = xa.shape[1], xb.shape[1]
    return pl.pallas_call(
        _out_proj_body,
        grid=(t // tm,),
        in_specs=[pl.BlockSpec((tm, d), lambda i: (i, 0)),
                  pl.BlockSpec((1, 1, d), lambda i: ((i * tm) // rows_per_group, 0, 0)),
                  pl.BlockSpec((tm, ka), lambda i: (i, 0)),
                  pl.BlockSpec((tm, kb), lambda i: (i, 0)),
                  pl.BlockSpec((ka, d), lambda i: (0, 0)),
                  pl.BlockSpec((kb, d), lambda i: (0, 0))],
        out_specs=pl.BlockSpec((tm, d), lambda i: (i, 0)),
        out_shape=jax.ShapeDtypeStruct((t, d), F32),
        compiler_params=_cparams("parallel"),
        name="out_proj",
    )(h, gate, xa, xb, wa, wb)


def _gmlp_tail_body(z_ref, h_ref, gate_ref, lng_ref, lnb_ref, ws_ref, bs_ref, wo_ref, o_ref, gated_ref, *, tm, width):
    gw = width // C_GROUPS
    for ci in range(tm // CHUNK):
        rows = slice(ci * CHUNK, (ci + 1) * CHUNK)
        v = z_ref[rows, width:]
        mu = jnp.mean(v, axis=-1, keepdims=True)
        var = jnp.mean(jnp.square(v - mu), axis=-1, keepdims=True)
        vn = ((v - mu) * lax.rsqrt(var + EPS)) * lng_ref[...] + lnb_ref[...]
        vb = vn.astype(BF16)
        for g in range(C_GROUPS):
            cols = slice(g * gw, (g + 1) * gw)
            sv = jnp.dot(ws_ref[g], vb[:, cols], preferred_element_type=F32) + bs_ref[g]
            gated_ref[rows, cols] = (z_ref[rows, cols] * sv).astype(BF16)
    y = jnp.dot(gated_ref[...], wo_ref[...], preferred_element_type=F32)
    o_ref[...] = h_ref[...] + gate_ref[0] * y


def _gmlp_tail(z, h, gate, ln_g, ln_b, w_s, b_s, w_out, *, rows_per_group, tm=512):
    t, d = h.shape
    width = z.shape[1] // 2
    gw = width // C_GROUPS
    bsb = jnp.broadcast_to(b_s[:, :, None], (C_GROUPS, CHUNK, gw))
    return pl.pallas_call(
        functools.partial(_gmlp_tail_body, tm=tm, width=width),
        grid=(t // tm,),
        in_specs=[pl.BlockSpec((tm, 2 * width), lambda i: (i, 0)),
                  pl.BlockSpec((tm, d), lambda i: (i, 0)),
                  pl.BlockSpec((1, 1, d), lambda i: ((i * tm) // rows_per_group, 0, 0)),
                  pl.BlockSpec((1, width), lambda i: (0, 0)),
                  pl.BlockSpec((1, width), lambda i: (0, 0)),
                  pl.BlockSpec((C_GROUPS, CHUNK, CHUNK), lambda i: (0, 0, 0)),
                  pl.BlockSpec((C_GROUPS, CHUNK, gw), lambda i: (0, 0, 0)),
                  pl.BlockSpec((width, d), lambda i: (0, 0))],
        out_specs=pl.BlockSpec((tm, d), lambda i: (i, 0)),
        out_shape=jax.ShapeDtypeStruct((t, d), F32),
        scratch_shapes=[pltpu.VMEM((tm, width), BF16)],
        compiler_params=_cparams("parallel"),
        name="gmlp_tail",
    )(z, h, gate, ln_g.reshape(1, width), ln_b.reshape(1, width), w_s.astype(BF16), bsb, w_out)


def _topk_rows(s, payload, k):
    n_rows = s.shape[0]
    riota = lax.broadcasted_iota(jnp.int32, s.shape, 0)
    vals, pays = [], []
    for _ in range(k):
        m = jnp.max(s, axis=0, keepdims=True)
        first = jnp.min(jnp.where(s == m, riota, n_rows), axis=0, keepdims=True)
        sel = riota == first
        vals.append(m)
        pays.append(jnp.max(jnp.where(sel, payload, -1), axis=0, keepdims=True))
        s = jnp.where(sel, -jnp.inf, s)
    return jnp.concatenate(vals, axis=0), jnp.concatenate(pays, axis=0)


def _peer_topk_body(q_ref, keys_ref, e_ref, g_ref, *, tt):
    nt = (((1,), (1,)), ((), ()))
    kiota = lax.broadcasted_iota(jnp.int32, (N_KEYS, tt), 0)
    for h in range(PEER_HEADS):
        halves = []
        for p in range(2):
            hp = 2 * h + p
            qhp = q_ref[:, hp * 128:(hp + 1) * 128].astype(BF16)
            s = lax.dot_general(keys_ref[hp], qhp, nt, preferred_element_type=F32)
            halves.append(_topk_rows(s, kiota, PEER_TOPK))
        (s1, i1), (s2, i2) = halves
        cand = jnp.concatenate([s1[a:a + 1] + s2 for a in range(PEER_TOPK)], axis=0)
        expert = jnp.concatenate([i1[a:a + 1] * N_KEYS + i2 for a in range(PEER_TOPK)], axis=0)
        sc, e = _topk_rows(cand, expert, PEER_TOPK)
        ex = jnp.exp(sc - sc[0:1])
        g = ex / jnp.sum(ex, axis=0, keepdims=True)
        e_ref[h * PEER_TOPK:(h + 1) * PEER_TOPK, :] = e
        g_ref[h * PEER_TOPK:(h + 1) * PEER_TOPK, :] = g


def _peer_topk(q, sub_keys, *, tt=128):
    t, qw = q.shape
    keys = sub_keys.reshape(2 * PEER_HEADS, N_KEYS, qw // (2 * PEER_HEADS)).astype(BF16)
    return pl.pallas_call(
        functools.partial(_peer_topk_body, tt=tt),
        grid=(t // tt,),
        in_specs=[pl.BlockSpec((tt, qw), lambda i: (i, 0)),
                  pl.BlockSpec(keys.shape, lambda i: (0, 0, 0))],
        out_specs=[pl.BlockSpec((PEER_SEL, tt), lambda i: (0, i)),
                   pl.BlockSpec((PEER_SEL, tt), lambda i: (0, i))],
        out_shape=[jax.ShapeDtypeStruct((PEER_SEL, t), jnp.int32),
                   jax.ShapeDtypeStruct((PEER_SEL, t), F32)],
        compiler_params=_cparams("parallel"),
        name="peer_topk",
    )(q, keys)


SC_CORES = 2
SC_SUBCORES = 16
SC_LANES = 16
SC_TILES = SC_CORES * SC_SUBCORES
ROW_SUB = 8
ROW_LANE = 128
VECS_PER_SUB = ROW_LANE // SC_LANES
D_BLOCKS = 4
VECS_PER_BLOCK = ROW_SUB * VECS_PER_SUB // D_BLOCKS


def _sc_gelu(s):
    z = 0.7978845608028654 * (s + 0.044715 * (s * s * s))
    tanh_z = 1.0 - 2.0 / (jnp.exp(2.0 * z) + 1.0)
    return 0.5 * s * (1.0 + tanh_z)


def _vec_slot(jb, k):
    v = jb * VECS_PER_BLOCK + k
    return v // VECS_PER_SUB, pl.ds((v % VECS_PER_SUB) * SC_LANES, SC_LANES)


def _peer_sc_body(e_hbm, g_hbm, x_hbm, u_hbm, v_hbm, o_hbm, idx_v, g_v, x_v, out_v, acc_v, w_v, ubuf, vbuf,
                  sem_u, sem_v, *, tok_per_tile):
    wid = lax.axis_index("s") * SC_CORES + lax.axis_index("c")
    base = wid * tok_per_tile
    lane = lax.iota(jnp.int32, SC_LANES)
    n_groups = PEER_SEL // SC_LANES

    @pl.loop(0, tok_per_tile)
    def _(ti):
        tok = base + ti
        pltpu.sync_copy(e_hbm.at[tok], idx_v)
        pltpu.sync_copy(g_hbm.at[tok], g_v)
        pltpu.sync_copy(x_hbm.at[tok], x_v)
        for j in range(ROW_SUB):
            for l in range(VECS_PER_SUB):
                out_v[j, pl.ds(l * SC_LANES, SC_LANES)] = jnp.zeros((SC_LANES,), F32)

        @pl.loop(0, n_groups)
        def _(c):
            cu = pltpu.async_copy(u_hbm.at[idx_v.at[c]], ubuf, sem_u)
            cv = pltpu.async_copy(v_hbm.at[idx_v.at[c]], vbuf, sem_v)
            cu.wait()
            for jb in range(D_BLOCKS):
                xs = [x_v[_vec_slot(jb, k)] for k in range(VECS_PER_BLOCK)]

                @pl.loop(0, SC_LANES)
                def _(r):
                    a = jnp.zeros((SC_LANES,), F32) if jb == 0 else acc_v[r, :]
                    for k in range(VECS_PER_BLOCK):
                        a = a + xs[k] * ubuf[(r,) + _vec_slot(jb, k)]
                    acc_v[r, :] = a
            s = jnp.zeros((SC_LANES,), F32)
            for l in range(SC_LANES):
                s = s + plsc.load_gather(acc_v, [lane, jnp.full((SC_LANES,), l, jnp.int32)])
            w_v[...] = g_v[c, :] * _sc_gelu(s)
            cv.wait()
            for jb in range(D_BLOCKS):
                init = tuple(out_v[_vec_slot(jb, k)] for k in range(VECS_PER_BLOCK))

                def add_row(r, os, jb=jb):
                    wb = plsc.load_gather(w_v, [jnp.full((SC_LANES,), r, jnp.int32)])
                    return tuple(os[k] + wb * vbuf[(r,) + _vec_slot(jb, k)] for k in range(VECS_PER_BLOCK))

                os = lax.fori_loop(0, SC_LANES, add_row, init)
                for k in range(VECS_PER_BLOCK):
                    out_v[_vec_slot(jb, k)] = os[k]

        pltpu.sync_copy(out_v, o_hbm.at[tok])


def _peer_experts(e, g, xm, u, v):
    t, d = xm.shape
    assert d == ROW_SUB * ROW_LANE and t % SC_TILES == 0
    n_groups = PEER_SEL // SC_LANES
    n_e = u.shape[0]
    call = pl.kernel(
        functools.partial(_peer_sc_body, tok_per_tile=t // SC_TILES),
        out_type=jax.ShapeDtypeStruct((t, ROW_SUB, ROW_LANE), F32),
        mesh=plsc.VectorSubcoreMesh(core_axis_name="c", subcore_axis_name="s"),
        scratch_types=[pltpu.VMEM((n_groups, SC_LANES), jnp.int32),
                       pltpu.VMEM((n_groups, SC_LANES), F32),
                       pltpu.VMEM((ROW_SUB, ROW_LANE), F32),
                       pltpu.VMEM((ROW_SUB, ROW_LANE), F32),
                       pltpu.VMEM((SC_LANES, SC_LANES), F32),
                       pltpu.VMEM((SC_LANES,), F32),
                       pltpu.VMEM((SC_LANES, ROW_SUB, ROW_LANE), F32),
                       pltpu.VMEM((SC_LANES, ROW_SUB, ROW_LANE), F32),
                       pltpu.SemaphoreType.DMA,
                       pltpu.SemaphoreType.DMA],
        compiler_params=pltpu.CompilerParams(needs_layout_passes=False),
        name="peer_experts_sc",
    )
    out = call(e.reshape(t, n_groups, SC_LANES), g.reshape(t, n_groups, SC_LANES), xm.reshape(t, ROW_SUB, ROW_LANE),
               u.reshape(n_e, ROW_SUB, ROW_LANE), v.reshape(n_e, ROW_SUB, ROW_LANE))
    return out.reshape(t, d)


def _peer(q, xm, sub_keys, u, v):
    e_t, g_t = _peer_topk(q, sub_keys)
    return _peer_experts(e_t.T, g_t.T, xm, u, v)


def _final_body(h_ref, delta_ref, gate_ref, g_ref, o_ref):
    h = h_ref[...] + gate_ref[0] * delta_ref[...]
    o_ref[...] = _rms(h) * g_ref[...]


def _final(h, delta, gate, gain, *, rows_per_group, tm=512):
    t, d = h.shape
    row = pl.BlockSpec((tm, d), lambda i: (i, 0))
    return pl.pallas_call(
        _final_body,
        grid=(t // tm,),
        in_specs=[row, row, pl.BlockSpec((1, 1, d), lambda i: ((i * tm) // rows_per_group, 0, 0)),
                  pl.BlockSpec((1, d), lambda i: (0, 0))],
        out_specs=row,
        out_shape=jax.ShapeDtypeStruct((t, d), F32),
        compiler_params=_cparams("parallel"),
        name="final_norm",
    )(h, delta, gate, gain.reshape(1, d))


def kernel(x, c, ctx, c_ctx, w_mod, b_mod, attn_w_in, attn_w_out, lambda_q1, lambda_k1, lambda_q2, lambda_k2,
           subln_g, na_rpb, chunk_w_in, chunk_b_in, chunk_ln_g, chunk_ln_b, chunk_w_s, chunk_b_s, chunk_w_out,
           peer_w_query, peer_sub_keys, peer_u, peer_v, final_norm_g):
    b, n, d = x.shape
    n_ctx = ctx.shape[1]
    t = b * n
    x2 = x.reshape(t, d)

    cond = jnp.concatenate([c, c_ctx[None], jnp.zeros((-(b + 1) % 8, d), F32)], axis=0)
    mods = [_adaln(cond, w_mod[i], b_mod[i]) for i in range(2)]

    def lat(m, k):
        return m[:b, k * d:(k + 1) * d].reshape(b, 1, d)

    def cx(m, k):
        return m[b:b + 1, k * d:(k + 1) * d].reshape(1, 1, d)

    m0 = mods[0]
    w_in = attn_w_in[0].astype(BF16)
    rope = _rope_tables(n, 512)
    z = _mod_mm(x2, lat(m0, 0), lat(m0, 1), w_in, rows_per_group=n, tm=256, rope=rope, rope_chunks=(0, 1),
                scale_chunks=(0, 3), name="attn_in")[0]
    zc = _mod_mm(ctx.reshape(b * n_ctx, d), cx(m0, 0), cx(m0, 1), w_in, rows_per_group=n_ctx, tm=256,
                 name="attn_in_ctx")[0]
    z3 = z.reshape(b, n, -1)
    zc3 = zc.reshape(b, n_ctx, -1)
    lam_init = 0.8 - 0.6 * math.exp(-0.3 * 0)
    lam = (jnp.exp(jnp.sum(lambda_q1[0] * lambda_k1[0])) - jnp.exp(jnp.sum(lambda_q2[0] * lambda_k2[0])) + lam_init)
    oa = _diff_attn(z3, zc3, lam.astype(F32), subln_g[0], lam_init)
    ob = _na_attn(z3, zc3, _na_bias_tables(na_rpb[0], n // GRID_W))
    w_out = attn_w_out[0].astype(BF16)
    ka = oa.shape[-1]
    h1 = _out_proj(x2, lat(m0, 2), oa.reshape(t, ka), ob.reshape(t, -1), w_out[:ka], w_out[ka:], rows_per_group=n)
    q0, xm0 = _mod_mm(h1, lat(m0, 3), lat(m0, 4), peer_w_query[0].astype(BF16), rows_per_group=n, tm=256,
                      emit_xm=True, name="peer_q0")
    p0 = _peer(q0, xm0, peer_sub_keys[0], peer_u[0], peer_v[0])

    m1 = mods[1]
    zg, h1b = _mod_mm(h1, lat(m1, 0), lat(m1, 1), chunk_w_in[0].astype(BF16), rows_per_group=n, tm=256,
                      delta=p0, gate=lat(m0, 5), bias=chunk_b_in[0], act="gelu", name="gmlp_in")
    h2 = _gmlp_tail(zg, h1b, lat(m1, 2), chunk_ln_g[0], chunk_ln_b[0], chunk_w_s[0], chunk_b_s[0],
                    chunk_w_out[0].astype(BF16), rows_per_group=n)
    q1, xm1 = _mod_mm(h2, lat(m1, 3), lat(m1, 4), peer_w_query[1].astype(BF16), rows_per_group=n, tm=256,
                      emit_xm=True, name="peer_q1")
    p1 = _peer(q1, xm1, peer_sub_keys[1], peer_u[1], peer_v[1])
    out = _final(h2, p1, lat(m1, 5), final_norm_g, rows_per_group=n)
    return out.reshape(b, n, d)
```

```python
import functools
import math

import numpy as np
import jax
import jax.numpy as jnp
from jax import lax
from jax.experimental import pallas as pl
from jax.experimental.pallas import tpu as pltpu
from jax.experimental.pallas import tpu_sc as plsc

F32 = jnp.float32
BF16 = jnp.bfloat16

EPS = 1e-6
GRID_W = 64
NA_ROWS = 8
NA_COLS = 16
ROPE_BASE = 10000.0
A_HEADS = 4
A_QK_DIM = 64
B_HEADS = 8
B_HEAD_DIM = 64
PEER_HEADS = 8
N_KEYS = 128
PEER_TOPK = 16
PEER_SEL = PEER_HEADS * PEER_TOPK
CHUNK = 128
C_GROUPS = 8
MASK_VALUE = -1e30

VMEM_LIMIT = 56 * 1024 * 1024


def _cparams(*sem):
    return pltpu.CompilerParams(dimension_semantics=sem, vmem_limit_bytes=VMEM_LIMIT)


def _rms(x):
    return x * lax.rsqrt(jnp.mean(x * x, axis=-1, keepdims=True) + EPS)


def _gelu(x):
    return jax.nn.gelu(x)


def _adaln_body(c_ref, w_ref, b_ref, o_ref):
    cnd = c_ref[...]
    a = (cnd * jax.nn.sigmoid(cnd)).astype(BF16)
    o_ref[...] = jnp.dot(a, w_ref[...].astype(BF16), preferred_element_type=F32) + b_ref[...]


def _adaln(cond, w, b):
    m, d = cond.shape
    n = w.shape[1]
    tn = 1024
    return pl.pallas_call(
        _adaln_body,
        grid=(n // tn,),
        in_specs=[pl.BlockSpec((m, d), lambda j: (0, 0)),
                  pl.BlockSpec((d, tn), lambda j: (0, j)),
                  pl.BlockSpec((1, tn), lambda j: (0, j))],
        out_specs=pl.BlockSpec((m, tn), lambda j: (0, j)),
        out_shape=jax.ShapeDtypeStruct((m, n), F32),
        compiler_params=_cparams("parallel"),
        name="adaln",
    )(cond, w, b.reshape(1, n))


def _mod_mm_body(*refs, has_delta, has_bias, rope_chunks, scale_chunks, act, emit_xm, tn, n_out):
    it = iter(refs)
    h_ref = next(it)
    delta_ref = gate_ref = None
    if has_delta:
        delta_ref, gate_ref = next(it), next(it)
    shift_ref, scale_ref, w_ref = next(it), next(it), next(it)
    b_ref = next(it) if has_bias else None
    cos_ref = sin_ref = None
    if rope_chunks:
        cos_ref, sin_ref = next(it), next(it)
    y_ref = next(it)
    hn_ref = next(it) if has_delta else None
    xm_ref = next(it) if emit_xm else None

    h = h_ref[...]
    if has_delta:
        h = h + gate_ref[0] * delta_ref[...]
        hn_ref[...] = h
    xm = _rms(h) * (1.0 + scale_ref[0]) + shift_ref[0]
    if emit_xm:
        xm_ref[...] = xm
    xb = xm.astype(BF16)
    for c in range(n_out // tn):
        cols = slice(c * tn, (c + 1) * tn)
        y = jnp.dot(xb, w_ref[:, cols], preferred_element_type=F32)
        if has_bias:
            y = y + b_ref[:, cols]
        if c in rope_chunks:
            lane = lax.broadcasted_iota(jnp.int32, y.shape, 1)
            partner = jnp.where((lane & 16) == 0, pltpu.roll(y, tn - 16, 1), pltpu.roll(y, 16, 1))
            y = y * cos_ref[...] + partner * sin_ref[...]
        if c in scale_chunks:
            y = y * 0.125
        if act == "gelu":
            y = _gelu(y)
        y_ref[:, cols] = y


def _mod_mm(h, shift, scale, w, *, rows_per_group, tm, delta=None, gate=None, bias=None, rope=None,
            rope_chunks=(), scale_chunks=(), act="none", emit_xm=False, name="mod_mm"):
    t, d = h.shape
    n_out = w.shape[1]
    tn = 512
    groups = shift.shape[0]
    assert t % tm == 0 and rows_per_group % tm == 0 and n_out % tn == 0

    def gidx(i):
        return ((i * tm) // rows_per_group if groups > 1 else 0, 0, 0)

    row_spec = pl.BlockSpec((tm, d), lambda i: (i, 0))
    mod_spec = pl.BlockSpec((1, 1, d), gidx)
    args, specs = [h], [row_spec]
    if delta is not None:
        args += [delta, gate]
        specs += [row_spec, mod_spec]
    args += [shift, scale, w]
    specs += [mod_spec, mod_spec, pl.BlockSpec((d, n_out), lambda i: (0, 0))]
    if bias is not None:
        args.append(bias.reshape(1, n_out))
        specs.append(pl.BlockSpec((1, n_out), lambda i: (0, 0)))
    if rope_chunks:
        cos, sin = rope
        pos_blocks = cos.shape[0] // tm
        args += [cos, sin]
        specs += [pl.BlockSpec((tm, tn), lambda i: (i % pos_blocks, 0))] * 2
    out_shape = [jax.ShapeDtypeStruct((t, n_out), F32)]
    out_specs = [pl.BlockSpec((tm, n_out), lambda i: (i, 0))]
    if delta is not None:
        out_shape.append(jax.ShapeDtypeStruct((t, d), F32))
        out_specs.append(row_spec)
    if emit_xm:
        out_shape.append(jax.ShapeDtypeStruct((t, d), F32))
        out_specs.append(row_spec)
    body = functools.partial(_mod_mm_body, has_delta=delta is not None, has_bias=bias is not None,
                             rope_chunks=tuple(rope_chunks), scale_chunks=tuple(scale_chunks), act=act,
                             emit_xm=emit_xm, tn=tn, n_out=n_out)
    return pl.pallas_call(body, grid=(t // tm,), in_specs=specs, out_specs=out_specs, out_shape=out_shape,
                          compiler_params=_cparams("parallel"), name=name)(*args)


def _rope_tables(n_tokens, width):
    axis_dim = A_QK_DIM // 2
    inv_freq = 1.0 / (ROPE_BASE ** (jnp.arange(0, axis_dim, 2, dtype=F32) / axis_dim))
    t = jnp.arange(n_tokens)
    row = (t // GRID_W).astype(F32)
    col = (t % GRID_W).astype(F32)
    ang_r = row[:, None] * inv_freq
    ang_c = col[:, None] * inv_freq
    ang = jnp.concatenate([ang_r, ang_r, ang_c, ang_c], axis=1)
    sign = jnp.tile(jnp.concatenate([-jnp.ones(16, F32), jnp.ones(16, F32)]), 2)
    reps = width // A_QK_DIM
    return jnp.tile(jnp.cos(ang), (1, reps)), jnp.tile(jnp.sin(ang) * sign, (1, reps))


def _diff_attn_body(lam_ref, q_ref, k_ref, v_ref, kc_ref, vc_ref, g_ref, o_ref, *, out_scale):
    lam = lam_ref[0, 0]
    q = q_ref[0]
    k = k_ref[0].astype(BF16)
    kc = kc_ref[0].astype(BF16)
    lane = lax.broadcasted_iota(jnp.int32, q.shape, 1)
    nt = (((1,), (1,)), ((), ()))
    parts = []
    for m in range(2):
        keep = (lane < A_QK_DIM) if m == 0 else (lane >= A_QK_DIM)
        qm = jnp.where(keep, q, 0.0).astype(BF16)
        s_l = lax.dot_general(qm, k, nt, preferred_element_type=F32)
        s_c = lax.dot_general(qm, kc, nt, preferred_element_type=F32)
        mx = jnp.maximum(jnp.max(s_l, axis=-1, keepdims=True), jnp.max(s_c, axis=-1, keepdims=True))
        e_l = jnp.exp(s_l - mx)
        e_c = jnp.exp(s_c - mx)
        inv = 1.0 / (jnp.sum(e_l, axis=-1, keepdims=True) + jnp.sum(e_c, axis=-1, keepdims=True))
        parts.append((e_l, e_c, inv))
    (e1l, e1c, inv1), (e2l, e2c, inv2) = parts
    w2 = lam * inv2
    a_l = (e1l * inv1 - e2l * w2).astype(BF16)
    a_c = (e1c * inv1 - e2c * w2).astype(BF16)
    o = (jnp.dot(a_l, v_ref[0].astype(BF16), preferred_element_type=F32)
         + jnp.dot(a_c, vc_ref[0].astype(BF16), preferred_element_type=F32))
    o_ref[0] = _rms(o) * g_ref[...] * out_scale


def _diff_attn(z, zc, lam, subln_g, lam_init, *, tq=256):
    b, n, _ = z.shape
    c = zc.shape[1]
    kcol, vcol = A_HEADS, 2 * A_HEADS
    return pl.pallas_call(
        functools.partial(_diff_attn_body, out_scale=1.0 - lam_init),
        grid=(b, A_HEADS, n // tq),
        in_specs=[pl.BlockSpec((1, 1), lambda bi, h, i: (0, 0), memory_space=pltpu.SMEM),
                  pl.BlockSpec((1, tq, 128), lambda bi, h, i: (bi, i, h)),
                  pl.BlockSpec((1, n, 128), lambda bi, h, i: (bi, 0, kcol + h)),
                  pl.BlockSpec((1, n, 128), lambda bi, h, i: (bi, 0, vcol + h)),
                  pl.BlockSpec((1, c, 128), lambda bi, h, i: (bi, 0, kcol + h)),
                  pl.BlockSpec((1, c, 128), lambda bi, h, i: (bi, 0, vcol + h)),
                  pl.BlockSpec((1, 128), lambda bi, h, i: (0, 0))],
        out_specs=pl.BlockSpec((1, tq, 128), lambda bi, h, i: (bi, i, h)),
        out_shape=jax.ShapeDtypeStruct((b, n, A_HEADS * 128), F32),
        compiler_params=_cparams("parallel", "parallel", "arbitrary"),
        name="diff_attn",
    )(lam.reshape(1, 1), z, z, z, zc, zc, subln_g.reshape(1, 128))


NA_QROWS = 8
NA_KROWS = 16
NA_SEG = 4 * GRID_W


def _na_window_start(j, rows):
    return jnp.clip(NA_QROWS * j - NA_ROWS // 2, 0, rows - NA_KROWS)


def _na_bias_tables(rpb, rows):
    n_blocks = rows // NA_QROWS
    h = rpb.shape[0]
    ic = np.clip(np.arange(GRID_W)[None, :] - np.arange(GRID_W)[:, None] + (NA_COLS - 1), 0, 2 * NA_COLS - 2)
    toep = jnp.pad(rpb[:, :, ic], ((0, 0), (NA_KROWS, NA_KROWS), (0, 0), (0, 0)))
    tabs, oks = [], []
    for j in (0, 1, n_blocks - 1):
        ks = int(np.clip(NA_QROWS * j - NA_ROWS // 2, 0, rows - NA_KROWS))
        slabs = []
        for rq in range(NA_QROWS):
            dr0 = ks - (NA_QROWS * j + rq) + (NA_ROWS - 1) + NA_KROWS
            slab = toep[:, dr0:dr0 + NA_KROWS]
            slabs.append(slab.transpose(0, 2, 1, 3).reshape(h, GRID_W, NA_KROWS * GRID_W))
        tabs.append(jnp.concatenate(slabs, axis=1))
        r = (NA_QROWS * j + np.arange(NA_QROWS))[:, None, None, None]
        cq = np.arange(GRID_W)[None, :, None, None]
        kr = (ks + np.arange(NA_KROWS))[None, None, :, None]
        kc = np.arange(GRID_W)[None, None, None, :]
        r0 = np.clip(r - NA_ROWS // 2, 0, rows - NA_ROWS)
        c0 = np.clip(cq - NA_COLS // 2, 0, GRID_W - NA_COLS)
        ok = (kr >= r0) & (kr < r0 + NA_ROWS) & (kc >= c0) & (kc < c0 + NA_COLS)
        oks.append(ok.reshape(NA_QROWS * GRID_W, NA_KROWS * GRID_W))
    return jnp.where(np.stack(oks)[None], jnp.stack(tabs, axis=1), MASK_VALUE)


def _na_body(q_ref, k0, k1, k2, k3, v0, v1, v2, v3, kc_ref, vc_ref, bias_ref, o_ref):
    q = q_ref[0]
    lane = lax.broadcasted_iota(jnp.int32, q.shape, 1)
    nt = (((1,), (1,)), ((), ()))
    ks = [r[0].astype(BF16) for r in (k0, k1, k2, k3)]
    vs = [r[0].astype(BF16) for r in (v0, v1, v2, v3)]
    kc = kc_ref[0].astype(BF16)
    vc = vc_ref[0].astype(BF16)
    outs = []
    for hh in range(2):
        keep = (lane < B_HEAD_DIM) if hh == 0 else (lane >= B_HEAD_DIM)
        qh = jnp.where(keep, q, 0.0).astype(BF16)
        segs = [lax.dot_general(qh, ks[i], nt, preferred_element_type=F32)
                + bias_ref[hh, 0, :, i * NA_SEG:(i + 1) * NA_SEG] for i in range(4)]
        s_c = lax.dot_general(qh, kc, nt, preferred_element_type=F32)
        mx = jnp.max(s_c, axis=-1, keepdims=True)
        for s in segs:
            mx = jnp.maximum(mx, jnp.max(s, axis=-1, keepdims=True))
        e_c = jnp.exp(s_c - mx)
        den = jnp.sum(e_c, axis=-1, keepdims=True)
        es = []
        for s in segs:
            e = jnp.exp(s - mx)
            den = den + jnp.sum(e, axis=-1, keepdims=True)
            es.append(e)
        inv = 1.0 / den
        o = jnp.dot((e_c * inv).astype(BF16), vc, preferred_element_type=F32)
        for e, v in zip(es, vs):
            o = o + jnp.dot((e * inv).astype(BF16), v, preferred_element_type=F32)
        outs.append(o)
    o_ref[0] = jnp.where(lane < B_HEAD_DIM, outs[0], outs[1])


def _na_attn(z, zc, bias_tabs):
    b, n, _ = z.shape
    c = zc.shape[1]
    rows = n // GRID_W
    n_blocks = rows // NA_QROWS
    tq = NA_QROWS * GRID_W
    qcol, kcol, vcol = 12, 16, 20
    seg_rows = NA_SEG // GRID_W

    def kv_spec(col0, i):
        return pl.BlockSpec((1, NA_SEG, 128),
                            lambda hp, j, bi: (bi, _na_window_start(j, rows) // seg_rows + i, col0 + hp))

    def variant(j):
        return jnp.where(j == 0, 0, jnp.where(j == n_blocks - 1, 2, 1))

    in_specs = [pl.BlockSpec((1, tq, 128), lambda hp, j, bi: (bi, j, qcol + hp))]
    in_specs += [kv_spec(kcol, i) for i in range(4)]
    in_specs += [kv_spec(vcol, i) for i in range(4)]
    in_specs += [pl.BlockSpec((1, c, 128), lambda hp, j, bi: (bi, 0, kcol + hp)),
                 pl.BlockSpec((1, c, 128), lambda hp, j, bi: (bi, 0, vcol + hp)),
                 pl.BlockSpec((2, 1, tq, NA_KROWS * GRID_W), lambda hp, j, bi: (hp, variant(j), 0, 0))]
    return pl.pallas_call(
        _na_body,
        grid=(B_HEADS // 2, n_blocks, b),
        in_specs=in_specs,
        out_specs=pl.BlockSpec((1, tq, 128), lambda hp, j, bi: (bi, j, hp)),
        out_shape=jax.ShapeDtypeStruct((b, n, B_HEADS * B_HEAD_DIM), F32),
        compiler_params=_cparams("parallel", "parallel", "arbitrary"),
        name="na_attn",
    )(z, z, z, z, z, z, z, z, z, zc, zc, bias_tabs)


def _out_proj_body(h_ref, gate_ref, xa_ref, xb_ref, wa_ref, wb_ref, o_ref):
    y = (jnp.dot(xa_ref[...].astype(BF16), wa_ref[...], preferred_element_type=F32)
         + jnp.dot(xb_ref[...].astype(BF16), wb_ref[...], preferred_element_type=F32))
    o_ref[...] = h_ref[...] + gate_ref[0] * y


def _out_proj(h, gate, xa, xb, wa, wb, *, rows_per_group, tm=512):
    t, d = h.shape
    ka, kb = xa.shape[1], xb.shape[1]
    return pl.pallas_call(
        _out_proj_body,
        grid=(t // tm,),
        in_specs=[pl.BlockSpec((tm, d), lambda i: (i, 0)),
                  pl.BlockSpec((1, 1, d), lambda i: ((i * tm) // rows_per_group, 0, 0)),
                  pl.BlockSpec((tm, ka), lambda i: (i, 0)),
                  pl.BlockSpec((tm, kb), lambda i: (i, 0)),
                  pl.BlockSpec((ka, d), lambda i: (0, 0)),
                  pl.BlockSpec((kb, d), lambda i: (0, 0))],
        out_specs=pl.BlockSpec((tm, d), lambda i: (i, 0)),
        out_shape=jax.ShapeDtypeStruct((t, d), F32),
        compiler_params=_cparams("parallel"),
        name="out_proj",
    )(h, gate, xa, xb, wa, wb)


def _gmlp_tail_body(z_ref, h_ref, gate_ref, lng_ref, lnb_ref, ws_ref, bs_ref, wo_ref, o_ref, gated_ref, *, tm, width):
    gw = width // C_GROUPS
    for ci in range(tm // CHUNK):
        rows = slice(ci * CHUNK, (ci + 1) * CHUNK)
        v = z_ref[rows, width:]
        mu = jnp.mean(v, axis=-1, keepdims=True)
        var = jnp.mean(jnp.square(v - mu), axis=-1, keepdims=True)
        vn = ((v - mu) * lax.rsqrt(var + EPS)) * lng_ref[...] + lnb_ref[...]
        vb = vn.astype(BF16)
        for g in range(C_GROUPS):
            cols = slice(g * gw, (g + 1) * gw)
            sv = jnp.dot(ws_ref[g], vb[:, cols], preferred_element_type=F32) + bs_ref[g]
            gated_ref[rows, cols] = (z_ref[rows, cols] * sv).astype(BF16)
    y = jnp.dot(gated_ref[...], wo_ref[...], preferred_element_type=F32)
    o_ref[...] = h_ref[...] + gate_ref[0] * y


def _gmlp_tail(z, h, gate, ln_g, ln_b, w_s, b_s, w_out, *, rows_per_group, tm=512):
    t, d = h.shape
    width = z.shape[1] // 2
    gw = width // C_GROUPS
    bsb = jnp.broadcast_to(b_s[:, :, None], (C_GROUPS, CHUNK, gw))
    return pl.pallas_call(
        functools.partial(_gmlp_tail_body, tm=tm, width=width),
        grid=(t // tm,),
        in_specs=[pl.BlockSpec((tm, 2 * width), lambda i: (i, 0)),
                  pl.BlockSpec((tm, d), lambda i: (i, 0)),
                  pl.BlockSpec((1, 1, d), lambda i: ((i * tm) // rows_per_group, 0, 0)),
                  pl.BlockSpec((1, width), lambda i: (0, 0)),
                  pl.BlockSpec((1, width), lambda i: (0, 0)),
                  pl.BlockSpec((C_GROUPS, CHUNK, CHUNK), lambda i: (0, 0, 0)),
                  pl.BlockSpec((C_GROUPS, CHUNK, gw), lambda i: (0, 0, 0)),
                  pl.BlockSpec((width, d), lambda i: (0, 0))],
        out_specs=pl.BlockSpec((tm, d), lambda i: (i, 0)),
        out_shape=jax.ShapeDtypeStruct((t, d), F32),
        scratch_shapes=[pltpu.VMEM((tm, width), BF16)],
        compiler_params=_cparams("parallel"),
        name="gmlp_tail",
    )(z, h, gate, ln_g.reshape(1, width), ln_b.reshape(1, width), w_s.astype(BF16), bsb, w_out)


def _topk_rows(s, payload, k):
    n_rows = s.shape[0]
    riota = lax.broadcasted_iota(jnp.int32, s.shape, 0)
    vals, pays = [], []
    for _ in range(k):
        m = jnp.max(s, axis=0, keepdims=True)
        first = jnp.min(jnp.where(s == m, riota, n_rows), axis=0, keepdims=True)
        sel = riota == first
        vals.append(m)
        pays.append(first if payload is None else jnp.max(jnp.where(sel, payload, -1), axis=0, keepdims=True))
        s = jnp.where(sel, -jnp.inf, s)
    return jnp.concatenate(vals, axis=0), jnp.concatenate(pays, axis=0)


def _pair_candidates(x1, x2, combine, fill):
    k = PEER_TOPK
    sub = lax.broadcasted_iota(jnp.int32, (8, x1.shape[1]), 0)
    blocks = [combine(x1[0:1], x2)]
    for a in range(1, 8):
        blocks.append(jnp.where(sub < k // (a + 1), combine(x1[a:a + 1], x2[0:8]), fill))
    blocks.append(combine(x1[8:k], x2[0:1]))
    return jnp.concatenate(blocks, axis=0)


def _peer_topk_body(q_ref, keys_ref, e_ref, g_ref, *, tt):
    nt = (((1,), (1,)), ((), ()))
    for h in range(PEER_HEADS):
        halves = []
        for p in range(2):
            hp = 2 * h + p
            qhp = q_ref[:, hp * 128:(hp + 1) * 128].astype(BF16)
            s = lax.dot_general(keys_ref[hp], qhp, nt, preferred_element_type=F32)
            halves.append(_topk_rows(s, None, PEER_TOPK))
        (s1, i1), (s2, i2) = halves
        cand = _pair_candidates(s1, s2, lambda x, y: x + y, -jnp.inf)
        expert = _pair_candidates(i1, i2, lambda x, y: x * N_KEYS + y, -1)
        sc, e = _topk_rows(cand, expert, PEER_TOPK)
        ex = jnp.exp(sc - sc[0:1])
        g = ex / jnp.sum(ex, axis=0, keepdims=True)
        e_ref[h * PEER_TOPK:(h + 1) * PEER_TOPK, :] = e
        g_ref[h * PEER_TOPK:(h + 1) * PEER_TOPK, :] = g


def _peer_topk(q, sub_keys, *, tt=128):
    t, qw = q.shape
    keys = sub_keys.reshape(2 * PEER_HEADS, N_KEYS, qw // (2 * PEER_HEADS)).astype(BF16)
    return pl.pallas_call(
        functools.partial(_peer_topk_body, tt=tt),
        grid=(t // tt,),
        in_specs=[pl.BlockSpec((tt, qw), lambda i: (i, 0)),
                  pl.BlockSpec(keys.shape, lambda i: (0, 0, 0))],
        out_specs=[pl.BlockSpec((PEER_SEL, tt), lambda i: (0, i)),
                   pl.BlockSpec((PEER_SEL, tt), lambda i: (0, i))],
        out_shape=[jax.ShapeDtypeStruct((PEER_SEL, t), jnp.int32),
                   jax.ShapeDtypeStruct((PEER_SEL, t), F32)],
        compiler_params=_cparams("parallel"),
        name="peer_topk",
    )(q, keys)


SC_CORES = 2
SC_SUBCORES = 16
SC_LANES = 16
SC_TILES = SC_CORES * SC_SUBCORES
ROW_SUB = 8
ROW_LANE = 128
VECS_PER_SUB = ROW_LANE // SC_LANES
D_BLOCKS = 4
VECS_PER_BLOCK = ROW_SUB * VECS_PER_SUB // D_BLOCKS


def _sc_gelu(s):
    z = 0.7978845608028654 * (s + 0.044715 * (s * s * s))
    tanh_z = 1.0 - 2.0 / (jnp.exp(2.0 * z) + 1.0)
    return 0.5 * s * (1.0 + tanh_z)


def _vec_slot(jb, k):
    v = jb * VECS_PER_BLOCK + k
    return v // VECS_PER_SUB, pl.ds((v % VECS_PER_SUB) * SC_LANES, SC_LANES)


SC_TOKB = 8
PART_SUMS = 4
SC_RING = 3


def _peer_sc_body(e_hbm, g_hbm, x_hbm, u_hbm, v_hbm, o_hbm, idx_v, g_v, x_v, out_v, acc_v, w_v, ubuf, vbuf,
                  sem_u, sem_v, sem_o, *, tok_per_tile):
    wid = lax.axis_index("s") * SC_CORES + lax.axis_index("c")
    base = wid * tok_per_tile
    lane = lax.iota(jnp.int32, SC_LANES)
    n_groups = PEER_SEL // SC_LANES
    gpb = SC_TOKB * n_groups

    def gather(gi, slot):
        return (pltpu.make_async_copy(u_hbm.at[idx_v.at[gi]], ubuf.at[slot], sem_u.at[slot]),
                pltpu.make_async_copy(v_hbm.at[idx_v.at[gi]], vbuf.at[slot], sem_v.at[slot]))

    def start(gi, slot):
        for cp in gather(gi, slot):
            cp.start()

    def out_copy(tok, par):
        return pltpu.make_async_copy(out_v.at[par], o_hbm.at[tok], sem_o.at[par])

    def u_phase(slot, t, gi):
        for jb in range(D_BLOCKS):
            xs = [x_v[(t,) + _vec_slot(jb, k)] for k in range(VECS_PER_BLOCK)]

            @plsc.parallel_loop(0, SC_LANES, unroll=2)
            def _(r, jb=jb, xs=xs):
                parts = []
                per = VECS_PER_BLOCK // PART_SUMS
                for q in range(PART_SUMS):
                    p = xs[q * per] * ubuf[(slot, r) + _vec_slot(jb, q * per)]
                    for k in range(q * per + 1, (q + 1) * per):
                        p = p + xs[k] * ubuf[(slot, r) + _vec_slot(jb, k)]
                    parts.append(p)
                a = (parts[0] + parts[1]) + (parts[2] + parts[3])
                if jb > 0:
                    a = a + acc_v[r, :]
                acc_v[r, :] = a
        cols = [plsc.load_gather(acc_v, [lane, jnp.full((SC_LANES,), l, jnp.int32)]) for l in range(SC_LANES)]
        while len(cols) > 1:
            cols = [cols[i] + cols[i + 1] for i in range(0, len(cols), 2)]
        w_v[...] = g_v[gi, :] * _sc_gelu(cols[0])

    def v_phase(slot, par):
        for jb in range(D_BLOCKS):
            init = tuple(out_v[(par,) + _vec_slot(jb, k)] for k in range(VECS_PER_BLOCK))

            def add_row(r, os, jb=jb):
                wb = plsc.load_gather(w_v, [jnp.full((SC_LANES,), r, jnp.int32)])
                return tuple(os[k] + wb * vbuf[(slot, r) + _vec_slot(jb, k)] for k in range(VECS_PER_BLOCK))

            os = plsc.parallel_loop(0, SC_LANES, unroll=2, carry=init)(add_row)
            for k in range(VECS_PER_BLOCK):
                out_v[(par,) + _vec_slot(jb, k)] = os[k]

    @pl.loop(0, tok_per_tile // SC_TOKB)
    def _(bi):
        tok0 = base + bi * SC_TOKB
        pltpu.sync_copy(e_hbm.at[pl.ds(tok0 * n_groups, gpb)], idx_v)
        pltpu.sync_copy(g_hbm.at[pl.ds(tok0 * n_groups, gpb)], g_v)
        pltpu.sync_copy(x_hbm.at[pl.ds(tok0, SC_TOKB)], x_v)
        for gi0 in range(SC_RING - 1):
            start(gi0, gi0)

        def step(gi, slot):
            gi = jnp.asarray(gi, jnp.int32)
            t = gi // n_groups
            par = t % 2

            @pl.when(gi + SC_RING - 1 < gpb)
            def _():
                start(gi + SC_RING - 1, (slot + SC_RING - 1) % SC_RING)

            @pl.when(gi % n_groups == 0)
            def _():
                @pl.when(bi * SC_TOKB + t >= 2)
                def _():
                    out_copy(tok0 + t, par).wait()
                for j in range(ROW_SUB):
                    for l in range(VECS_PER_SUB):
                        out_v[par, j, pl.ds(l * SC_LANES, SC_LANES)] = jnp.zeros((SC_LANES,), F32)

            cu, cv = gather(gi, slot)
            cu.wait()
            u_phase(slot, t, gi)
            cv.wait()
            v_phase(slot, par)

            @pl.when(gi % n_groups == n_groups - 1)
            def _():
                out_copy(tok0 + t, par).start()

        @pl.loop(0, gpb // SC_RING)
        def _(it):
            for slot in range(SC_RING):
                step(it * SC_RING + slot, slot)

        for gi in range(gpb - gpb % SC_RING, gpb):
            step(gi, gi % SC_RING)

    for par in range(2):
        out_copy(base, par).wait()


def _peer_experts(e, g, xm, u, v):
    t, d = xm.shape
    assert d == ROW_SUB * ROW_LANE and t % (SC_TILES * SC_TOKB) == 0
    n_groups = PEER_SEL // SC_LANES
    gpb = SC_TOKB * n_groups
    n_e = u.shape[0]
    call = pl.kernel(
        functools.partial(_peer_sc_body, tok_per_tile=t // SC_TILES),
        out_type=jax.ShapeDtypeStruct((t, ROW_SUB, ROW_LANE), F32),
        mesh=plsc.VectorSubcoreMesh(core_axis_name="c", subcore_axis_name="s"),
        scratch_types=[pltpu.VMEM((gpb, SC_LANES), jnp.int32),
                       pltpu.VMEM((gpb, SC_LANES), F32),
                       pltpu.VMEM((SC_TOKB, ROW_SUB, ROW_LANE), F32),
                       pltpu.VMEM((2, ROW_SUB, ROW_LANE), F32),
                       pltpu.VMEM((SC_LANES, SC_LANES), F32),
                       pltpu.VMEM((SC_LANES,), F32),
                       pltpu.VMEM((SC_RING, SC_LANES, ROW_SUB, ROW_LANE), F32),
                       pltpu.VMEM((SC_RING, SC_LANES, ROW_SUB, ROW_LANE), F32),
                       pltpu.SemaphoreType.DMA((SC_RING,)),
                       pltpu.SemaphoreType.DMA((SC_RING,)),
                       pltpu.SemaphoreType.DMA((2,))],
        compiler_params=pltpu.CompilerParams(needs_layout_passes=False),
        name="peer_experts_sc",
    )
    out = call(e.reshape(t * n_groups, SC_LANES), g.reshape(t * n_groups, SC_LANES),
               xm.reshape(t, ROW_SUB, ROW_LANE), u.reshape(n_e, ROW_SUB, ROW_LANE), v.reshape(n_e, ROW_SUB, ROW_LANE))
    return out.reshape(t, d)


def _peer(q, xm, sub_keys, u, v):
    e_t, g_t = _peer_topk(q, sub_keys)
    return _peer_experts(e_t.T, g_t.T, xm, u, v)


def _final_body(h_ref, delta_ref, gate_ref, g_ref, o_ref):
    h = h_ref[...] + gate_ref[0] * delta_ref[...]
    o_ref[...] = _rms(h) * g_ref[...]


def _final(h, delta, gate, gain, *, rows_per_group, tm=512):
    t, d = h.shape
    row = pl.BlockSpec((tm, d), lambda i: (i, 0))
    return pl.pallas_call(
        _final_body,
        grid=(t // tm,),
        in_specs=[row, row, pl.BlockSpec((1, 1, d), lambda i: ((i * tm) // rows_per_group, 0, 0)),
                  pl.BlockSpec((1, d), lambda i: (0, 0))],
        out_specs=row,
        out_shape=jax.ShapeDtypeStruct((t, d), F32),
        compiler_params=_cparams("parallel"),
        name="final_norm",
    )(h, delta, gate, gain.reshape(1, d))


def kernel(x, c, ctx, c_ctx, w_mod, b_mod, attn_w_in, attn_w_out, lambda_q1, lambda_k1, lambda_q2, lambda_k2,
           subln_g, na_rpb, chunk_w_in, chunk_b_in, chunk_ln_g, chunk_ln_b, chunk_w_s, chunk_b_s, chunk_w_out,
           peer_w_query, peer_sub_keys, peer_u, peer_v, final_norm_g):
    b, n, d = x.shape
    n_ctx = ctx.shape[1]
    t = b * n
    x2 = x.reshape(t, d)

    cond = jnp.concatenate([c, c_ctx[None], jnp.zeros((-(b + 1) % 8, d), F32)], axis=0)
    mods = [_adaln(cond, w_mod[i], b_mod[i]) for i in range(2)]

    def lat(m, k):
        return m[:b, k * d:(k + 1) * d].reshape(b, 1, d)

    def cx(m, k):
        return m[b:b + 1, k * d:(k + 1) * d].reshape(1, 1, d)

    m0 = mods[0]
    w_in = attn_w_in[0].astype(BF16)
    rope = _rope_tables(n, 512)
    z = _mod_mm(x2, lat(m0, 0), lat(m0, 1), w_in, rows_per_group=n, tm=256, rope=rope, rope_chunks=(0, 1),
                scale_chunks=(0, 3), name="attn_in")[0]
    zc = _mod_mm(ctx.reshape(b * n_ctx, d), cx(m0, 0), cx(m0, 1), w_in, rows_per_group=n_ctx, tm=256,
                 name="attn_in_ctx")[0]
    z3 = z.reshape(b, n, -1)
    zc3 = zc.reshape(b, n_ctx, -1)
    lam_init = 0.8 - 0.6 * math.exp(-0.3 * 0)
    lam = (jnp.exp(jnp.sum(lambda_q1[0] * lambda_k1[0])) - jnp.exp(jnp.sum(lambda_q2[0] * lambda_k2[0])) + lam_init)
    oa = _diff_attn(z3, zc3, lam.astype(F32), subln_g[0], lam_init)
    ob = _na_attn(z3, zc3, _na_bias_tables(na_rpb[0], n // GRID_W))
    w_out = attn_w_out[0].astype(BF16)
    ka = oa.shape[-1]
    h1 = _out_proj(x2, lat(m0, 2), oa.reshape(t, ka), ob.reshape(t, -1), w_out[:ka], w_out[ka:], rows_per_group=n)
    q0, xm0 = _mod_mm(h1, lat(m0, 3), lat(m0, 4), peer_w_query[0].astype(BF16), rows_per_group=n, tm=256,
                      emit_xm=True, name="peer_q0")
    p0 = _peer(q0, xm0, peer_sub_keys[0], peer_u[0], peer_v[0])

    m1 = mods[1]
    zg, h1b = _mod_mm(h1, lat(m1, 0), lat(m1, 1), chunk_w_in[0].astype(BF16), rows_per_group=n, tm=256,
                      delta=p0, gate=lat(m0, 5), bias=chunk_b_in[0], act="gelu", name="gmlp_in")
    h2 = _gmlp_tail(zg, h1b, lat(m1, 2), chunk_ln_g[0], chunk_ln_b[0], chunk_w_s[0], chunk_b_s[0],
                    chunk_w_out[0].astype(BF16), rows_per_group=n)
    q1, xm1 = _mod_mm(h2, lat(m1, 3), lat(m1, 4), peer_w_query[1].astype(BF16), rows_per_group=n, tm=256,
                      emit_xm=True, name="peer_q1")
    p1 = _peer(q1, xm1, peer_sub_keys[1], peer_u[1], peer_v[1])
    out = _final(h2, p1, lat(m1, 5), final_norm_g, rows_per_group=n)
    return out.reshape(b, n, d)
```

```python
import functools
import math

import numpy as np
import jax
import jax.numpy as jnp
from jax import lax
from jax.experimental import pallas as pl
from jax.experimental.pallas import tpu as pltpu
from jax.experimental.pallas import tpu_sc as plsc

F32 = jnp.float32
BF16 = jnp.bfloat16

EPS = 1e-6
GRID_W = 64
NA_ROWS = 8
NA_COLS = 16
ROPE_BASE = 10000.0
A_HEADS = 4
A_QK_DIM = 64
B_HEADS = 8
B_HEAD_DIM = 64
PEER_HEADS = 8
N_KEYS = 128
PEER_TOPK = 16
PEER_SEL = PEER_HEADS * PEER_TOPK
CHUNK = 128
C_GROUPS = 8
MASK_VALUE = -1e30

VMEM_LIMIT = 56 * 1024 * 1024
PIPE_CHUNKS = 4


def _cparams(*sem):
    return pltpu.CompilerParams(dimension_semantics=sem, vmem_limit_bytes=VMEM_LIMIT)


def _rms(x):
    return x * lax.rsqrt(jnp.mean(x * x, axis=-1, keepdims=True) + EPS)


def _gelu(x):
    return jax.nn.gelu(x)


def _adaln_body(c_ref, w_ref, b_ref, o_ref):
    cnd = c_ref[...]
    a = (cnd * jax.nn.sigmoid(cnd)).astype(BF16)
    o_ref[...] = jnp.dot(a, w_ref[...].astype(BF16), preferred_element_type=F32) + b_ref[...]


def _adaln(cond, w, b):
    m, d = cond.shape
    n = w.shape[1]
    tn = 1024
    return pl.pallas_call(
        _adaln_body,
        grid=(n // tn,),
        in_specs=[pl.BlockSpec((m, d), lambda j: (0, 0)),
                  pl.BlockSpec((d, tn), lambda j: (0, j)),
                  pl.BlockSpec((1, tn), lambda j: (0, j))],
        out_specs=pl.BlockSpec((m, tn), lambda j: (0, j)),
        out_shape=jax.ShapeDtypeStruct((m, n), F32),
        compiler_params=_cparams("parallel"),
        name="adaln",
    )(cond, w, b.reshape(1, n))


def _mod_mm_body(*refs, has_delta, has_bias, rope_chunks, scale_chunks, act, emit_xm, tn, n_out):
    it = iter(refs)
    h_ref = next(it)
    delta_ref = gate_ref = None
    if has_delta:
        delta_ref, gate_ref = next(it), next(it)
    shift_ref, scale_ref, w_ref = next(it), next(it), next(it)
    b_ref = next(it) if has_bias else None
    cos_ref = sin_ref = None
    if rope_chunks:
        cos_ref, sin_ref = next(it), next(it)
    y_ref = next(it)
    hn_ref = next(it) if has_delta else None
    xm_ref = next(it) if emit_xm else None

    h = h_ref[...]
    if has_delta:
        h = h + gate_ref[0] * delta_ref[...]
        hn_ref[...] = h
    xm = _rms(h) * (1.0 + scale_ref[0]) + shift_ref[0]
    if emit_xm:
        xm_ref[...] = xm
    xb = xm.astype(BF16)
    for c in range(n_out // tn):
        cols = slice(c * tn, (c + 1) * tn)
        y = jnp.dot(xb, w_ref[:, cols], preferred_element_type=F32)
        if has_bias:
            y = y + b_ref[:, cols]
        if c in rope_chunks:
            lane = lax.broadcasted_iota(jnp.int32, y.shape, 1)
            partner = jnp.where((lane & 16) == 0, pltpu.roll(y, tn - 16, 1), pltpu.roll(y, 16, 1))
            y = y * cos_ref[...] + partner * sin_ref[...]
        if c in scale_chunks:
            y = y * 0.125
        if act == "gelu":
            y = _gelu(y)
        y_ref[:, cols] = y


def _mod_mm(h, shift, scale, w, *, rows_per_group, tm, delta=None, gate=None, bias=None, rope=None,
            rope_chunks=(), scale_chunks=(), act="none", emit_xm=False, name="mod_mm"):
    t, d = h.shape
    n_out = w.shape[1]
    tn = 512
    groups = shift.shape[0]
    assert t % tm == 0 and rows_per_group % tm == 0 and n_out % tn == 0

    def gidx(i):
        return ((i * tm) // rows_per_group if groups > 1 else 0, 0, 0)

    row_spec = pl.BlockSpec((tm, d), lambda i: (i, 0))
    mod_spec = pl.BlockSpec((1, 1, d), gidx)
    args, specs = [h], [row_spec]
    if delta is not None:
        args += [delta, gate]
        specs += [row_spec, mod_spec]
    args += [shift, scale, w]
    specs += [mod_spec, mod_spec, pl.BlockSpec((d, n_out), lambda i: (0, 0))]
    if bias is not None:
        args.append(bias.reshape(1, n_out))
        specs.append(pl.BlockSpec((1, n_out), lambda i: (0, 0)))
    if rope_chunks:
        cos, sin = rope
        pos_blocks = cos.shape[0] // tm
        args += [cos, sin]
        specs += [pl.BlockSpec((tm, tn), lambda i: (i % pos_blocks, 0))] * 2
    out_shape = [jax.ShapeDtypeStruct((t, n_out), F32)]
    out_specs = [pl.BlockSpec((tm, n_out), lambda i: (i, 0))]
    if delta is not None:
        out_shape.append(jax.ShapeDtypeStruct((t, d), F32))
        out_specs.append(row_spec)
    if emit_xm:
        out_shape.append(jax.ShapeDtypeStruct((t, d), F32))
        out_specs.append(row_spec)
    body = functools.partial(_mod_mm_body, has_delta=delta is not None, has_bias=bias is not None,
                             rope_chunks=tuple(rope_chunks), scale_chunks=tuple(scale_chunks), act=act,
                             emit_xm=emit_xm, tn=tn, n_out=n_out)
    return pl.pallas_call(body, grid=(t // tm,), in_specs=specs, out_specs=out_specs, out_shape=out_shape,
                          compiler_params=_cparams("parallel"), name=name)(*args)


def _rope_tables(n_tokens, width):
    axis_dim = A_QK_DIM // 2
    inv_freq = 1.0 / (ROPE_BASE ** (jnp.arange(0, axis_dim, 2, dtype=F32) / axis_dim))
    t = jnp.arange(n_tokens)
    row = (t // GRID_W).astype(F32)
    col = (t % GRID_W).astype(F32)
    ang_r = row[:, None] * inv_freq
    ang_c = col[:, None] * inv_freq
    ang = jnp.concatenate([ang_r, ang_r, ang_c, ang_c], axis=1)
    sign = jnp.tile(jnp.concatenate([-jnp.ones(16, F32), jnp.ones(16, F32)]), 2)
    reps = width // A_QK_DIM
    return jnp.tile(jnp.cos(ang), (1, reps)), jnp.tile(jnp.sin(ang) * sign, (1, reps))


def _diff_attn_body(lam_ref, q_ref, k_ref, v_ref, kc_ref, vc_ref, g_ref, o_ref, *, out_scale):
    lam = lam_ref[0, 0]
    q = q_ref[0]
    k = k_ref[0].astype(BF16)
    kc = kc_ref[0].astype(BF16)
    lane = lax.broadcasted_iota(jnp.int32, q.shape, 1)
    nt = (((1,), (1,)), ((), ()))
    parts = []
    for m in range(2):
        keep = (lane < A_QK_DIM) if m == 0 else (lane >= A_QK_DIM)
        qm = jnp.where(keep, q, 0.0).astype(BF16)
        s_l = lax.dot_general(qm, k, nt, preferred_element_type=F32)
        s_c = lax.dot_general(qm, kc, nt, preferred_element_type=F32)
        mx = jnp.maximum(jnp.max(s_l, axis=-1, keepdims=True), jnp.max(s_c, axis=-1, keepdims=True))
        e_l = jnp.exp(s_l - mx)
        e_c = jnp.exp(s_c - mx)
        inv = 1.0 / (jnp.sum(e_l, axis=-1, keepdims=True) + jnp.sum(e_c, axis=-1, keepdims=True))
        parts.append((e_l, e_c, inv))
    (e1l, e1c, inv1), (e2l, e2c, inv2) = parts
    w2 = lam * inv2
    a_l = (e1l * inv1 - e2l * w2).astype(BF16)
    a_c = (e1c * inv1 - e2c * w2).astype(BF16)
    o = (jnp.dot(a_l, v_ref[0].astype(BF16), preferred_element_type=F32)
         + jnp.dot(a_c, vc_ref[0].astype(BF16), preferred_element_type=F32))
    o_ref[0] = _rms(o) * g_ref[...] * out_scale


def _diff_attn(z, zc, lam, subln_g, lam_init, *, tq=256):
    b, n, _ = z.shape
    c = zc.shape[1]
    kcol, vcol = A_HEADS, 2 * A_HEADS
    return pl.pallas_call(
        functools.partial(_diff_attn_body, out_scale=1.0 - lam_init),
        grid=(b, A_HEADS, n // tq),
        in_specs=[pl.BlockSpec((1, 1), lambda bi, h, i: (0, 0), memory_space=pltpu.SMEM),
                  pl.BlockSpec((1, tq, 128), lambda bi, h, i: (bi, i, h)),
                  pl.BlockSpec((1, n, 128), lambda bi, h, i: (bi, 0, kcol + h)),
                  pl.BlockSpec((1, n, 128), lambda bi, h, i: (bi, 0, vcol + h)),
                  pl.BlockSpec((1, c, 128), lambda bi, h, i: (bi, 0, kcol + h)),
                  pl.BlockSpec((1, c, 128), lambda bi, h, i: (bi, 0, vcol + h)),
                  pl.BlockSpec((1, 128), lambda bi, h, i: (0, 0))],
        out_specs=pl.BlockSpec((1, tq, 128), lambda bi, h, i: (bi, i, h)),
        out_shape=jax.ShapeDtypeStruct((b, n, A_HEADS * 128), F32),
        compiler_params=_cparams("parallel", "parallel", "arbitrary"),
        name="diff_attn",
    )(lam.reshape(1, 1), z, z, z, zc, zc, subln_g.reshape(1, 128))


NA_QROWS = 8
NA_KROWS = 16
NA_SEG = 4 * GRID_W


def _na_window_start(j, rows):
    return jnp.clip(NA_QROWS * j - NA_ROWS // 2, 0, rows - NA_KROWS)


def _na_bias_tables(rpb, rows):
    n_blocks = rows // NA_QROWS
    h = rpb.shape[0]
    ic = np.clip(np.arange(GRID_W)[None, :] - np.arange(GRID_W)[:, None] + (NA_COLS - 1), 0, 2 * NA_COLS - 2)
    toep = jnp.pad(rpb[:, :, ic], ((0, 0), (NA_KROWS, NA_KROWS), (0, 0), (0, 0)))
    tabs, oks = [], []
    for j in (0, 1, n_blocks - 1):
        ks = int(np.clip(NA_QROWS * j - NA_ROWS // 2, 0, rows - NA_KROWS))
        slabs = []
        for rq in range(NA_QROWS):
            dr0 = ks - (NA_QROWS * j + rq) + (NA_ROWS - 1) + NA_KROWS
            slab = toep[:, dr0:dr0 + NA_KROWS]
            slabs.append(slab.transpose(0, 2, 1, 3).reshape(h, GRID_W, NA_KROWS * GRID_W))
        tabs.append(jnp.concatenate(slabs, axis=1))
        r = (NA_QROWS * j + np.arange(NA_QROWS))[:, None, None, None]
        cq = np.arange(GRID_W)[None, :, None, None]
        kr = (ks + np.arange(NA_KROWS))[None, None, :, None]
        kc = np.arange(GRID_W)[None, None, None, :]
        r0 = np.clip(r - NA_ROWS // 2, 0, rows - NA_ROWS)
        c0 = np.clip(cq - NA_COLS // 2, 0, GRID_W - NA_COLS)
        ok = (kr >= r0) & (kr < r0 + NA_ROWS) & (kc >= c0) & (kc < c0 + NA_COLS)
        oks.append(ok.reshape(NA_QROWS * GRID_W, NA_KROWS * GRID_W))
    return jnp.where(np.stack(oks)[None], jnp.stack(tabs, axis=1), MASK_VALUE)


def _na_body(q_ref, k0, k1, k2, k3, v0, v1, v2, v3, kc_ref, vc_ref, bias_ref, o_ref):
    q = q_ref[0]
    lane = lax.broadcasted_iota(jnp.int32, q.shape, 1)
    nt = (((1,), (1,)), ((), ()))
    ks = [r[0].astype(BF16) for r in (k0, k1, k2, k3)]
    vs = [r[0].astype(BF16) for r in (v0, v1, v2, v3)]
    kc = kc_ref[0].astype(BF16)
    vc = vc_ref[0].astype(BF16)
    outs = []
    for hh in range(2):
        keep = (lane < B_HEAD_DIM) if hh == 0 else (lane >= B_HEAD_DIM)
        qh = jnp.where(keep, q, 0.0).astype(BF16)
        segs = [lax.dot_general(qh, ks[i], nt, preferred_element_type=F32)
                + bias_ref[hh, 0, :, i * NA_SEG:(i + 1) * NA_SEG] for i in range(4)]
        s_c = lax.dot_general(qh, kc, nt, preferred_element_type=F32)
        mx = jnp.max(s_c, axis=-1, keepdims=True)
        for s in segs:
            mx = jnp.maximum(mx, jnp.max(s, axis=-1, keepdims=True))
        e_c = jnp.exp(s_c - mx)
        den = jnp.sum(e_c, axis=-1, keepdims=True)
        es = []
        for s in segs:
            e = jnp.exp(s - mx)
            den = den + jnp.sum(e, axis=-1, keepdims=True)
            es.append(e)
        inv = 1.0 / den
        o = jnp.dot((e_c * inv).astype(BF16), vc, preferred_element_type=F32)
        for e, v in zip(es, vs):
            o = o + jnp.dot((e * inv).astype(BF16), v, preferred_element_type=F32)
        outs.append(o)
    o_ref[0] = jnp.where(lane < B_HEAD_DIM, outs[0], outs[1])


def _na_attn(z, zc, bias_tabs):
    b, n, _ = z.shape
    c = zc.shape[1]
    rows = n // GRID_W
    n_blocks = rows // NA_QROWS
    tq = NA_QROWS * GRID_W
    qcol, kcol, vcol = 12, 16, 20
    seg_rows = NA_SEG // GRID_W

    def kv_spec(col0, i):
        return pl.BlockSpec((1, NA_SEG, 128),
                            lambda hp, j, bi: (bi, _na_window_start(j, rows) // seg_rows + i, col0 + hp))

    def variant(j):
        return jnp.where(j == 0, 0, jnp.where(j == n_blocks - 1, 2, 1))

    in_specs = [pl.BlockSpec((1, tq, 128), lambda hp, j, bi: (bi, j, qcol + hp))]
    in_specs += [kv_spec(kcol, i) for i in range(4)]
    in_specs += [kv_spec(vcol, i) for i in range(4)]
    in_specs += [pl.BlockSpec((1, c, 128), lambda hp, j, bi: (bi, 0, kcol + hp)),
                 pl.BlockSpec((1, c, 128), lambda hp, j, bi: (bi, 0, vcol + hp)),
                 pl.BlockSpec((2, 1, tq, NA_KROWS * GRID_W), lambda hp, j, bi: (hp, variant(j), 0, 0))]
    return pl.pallas_call(
        _na_body,
        grid=(B_HEADS // 2, n_blocks, b),
        in_specs=in_specs,
        out_specs=pl.BlockSpec((1, tq, 128), lambda hp, j, bi: (bi, j, hp)),
        out_shape=jax.ShapeDtypeStruct((b, n, B_HEADS * B_HEAD_DIM), F32),
        compiler_params=_cparams("parallel", "parallel", "arbitrary"),
        name="na_attn",
    )(z, z, z, z, z, z, z, z, z, zc, zc, bias_tabs)


def _out_proj_body(h_ref, gate_ref, xa_ref, xb_ref, wa_ref, wb_ref, o_ref):
    y = (jnp.dot(xa_ref[...].astype(BF16), wa_ref[...], preferred_element_type=F32)
         + jnp.dot(xb_ref[...].astype(BF16), wb_ref[...], preferred_element_type=F32))
    o_ref[...] = h_ref[...] + gate_ref[0] * y


def _out_proj(h, gate, xa, xb, wa, wb, *, rows_per_group, tm=512):
    t, d = h.shape
    ka, kb = xa.shape[1], xb.shape[1]
    return pl.pallas_call(
        _out_proj_body,
        grid=(t // tm,),
        in_specs=[pl.BlockSpec((tm, d), lambda i: (i, 0)),
                  pl.BlockSpec((1, 1, d), lambda i: ((i * tm) // rows_per_group, 0, 0)),
                  pl.BlockSpec((tm, ka), lambda i: (i, 0)),
                  pl.BlockSpec((tm, kb), lambda i: (i, 0)),
                  pl.BlockSpec((ka, d), lambda i: (0, 0)),
                  pl.BlockSpec((kb, d), lambda i: (0, 0))],
        out_specs=pl.BlockSpec((tm, d), lambda i: (i, 0)),
        out_shape=jax.ShapeDtypeStruct((t, d), F32),
        compiler_params=_cparams("parallel"),
        name="out_proj",
    )(h, gate, xa, xb, wa, wb)


def _gmlp_tail_body(z_ref, h_ref, gate_ref, lng_ref, lnb_ref, ws_ref, bs_ref, wo_ref, o_ref, gated_ref, *, tm, width):
    gw = width // C_GROUPS
    for ci in range(tm // CHUNK):
        rows = slice(ci * CHUNK, (ci + 1) * CHUNK)
        v = z_ref[rows, width:]
        mu = jnp.mean(v, axis=-1, keepdims=True)
        var = jnp.mean(jnp.square(v - mu), axis=-1, keepdims=True)
        vn = ((v - mu) * lax.rsqrt(var + EPS)) * lng_ref[...] + lnb_ref[...]
        vb = vn.astype(BF16)
        for g in range(C_GROUPS):
            cols = slice(g * gw, (g + 1) * gw)
            sv = jnp.dot(ws_ref[g], vb[:, cols], preferred_element_type=F32) + bs_ref[g]
            gated_ref[rows, cols] = (z_ref[rows, cols] * sv).astype(BF16)
    y = jnp.dot(gated_ref[...], wo_ref[...], preferred_element_type=F32)
    o_ref[...] = h_ref[...] + gate_ref[0] * y


def _gmlp_tail(z, h, gate, ln_g, ln_b, w_s, b_s, w_out, *, rows_per_group, tm=512):
    t, d = h.shape
    width = z.shape[1] // 2
    gw = width // C_GROUPS
    bsb = jnp.broadcast_to(b_s[:, :, None], (C_GROUPS, CHUNK, gw))
    return pl.pallas_call(
        functools.partial(_gmlp_tail_body, tm=tm, width=width),
        grid=(t // tm,),
        in_specs=[pl.BlockSpec((tm, 2 * width), lambda i: (i, 0)),
                  pl.BlockSpec((tm, d), lambda i: (i, 0)),
                  pl.BlockSpec((1, 1, d), lambda i: ((i * tm) // rows_per_group, 0, 0)),
                  pl.BlockSpec((1, width), lambda i: (0, 0)),
                  pl.BlockSpec((1, width), lambda i: (0, 0)),
                  pl.BlockSpec((C_GROUPS, CHUNK, CHUNK), lambda i: (0, 0, 0)),
                  pl.BlockSpec((C_GROUPS, CHUNK, gw), lambda i: (0, 0, 0)),
                  pl.BlockSpec((width, d), lambda i: (0, 0))],
        out_specs=pl.BlockSpec((tm, d), lambda i: (i, 0)),
        out_shape=jax.ShapeDtypeStruct((t, d), F32),
        scratch_shapes=[pltpu.VMEM((tm, width), BF16)],
        compiler_params=_cparams("parallel"),
        name="gmlp_tail",
    )(z, h, gate, ln_g.reshape(1, width), ln_b.reshape(1, width), w_s.astype(BF16), bsb, w_out)


def _topk_rows(s, payload, k):
    n_rows = s.shape[0]
    riota = lax.broadcasted_iota(jnp.int32, s.shape, 0)
    vals, pays = [], []
    for _ in range(k):
        m = jnp.max(s, axis=0, keepdims=True)
        first = jnp.min(jnp.where(s == m, riota, n_rows), axis=0, keepdims=True)
        sel = riota == first
        vals.append(m)
        pays.append(first if payload is None else jnp.max(jnp.where(sel, payload, -1), axis=0, keepdims=True))
        s = jnp.where(sel, -jnp.inf, s)
    return jnp.concatenate(vals, axis=0), jnp.concatenate(pays, axis=0)


def _pair_candidates(x1, x2, combine, fill):
    k = PEER_TOPK
    sub = lax.broadcasted_iota(jnp.int32, (8, x1.shape[1]), 0)
    blocks = [combine(x1[0:1], x2)]
    for a in range(1, 8):
        blocks.append(jnp.where(sub < k // (a + 1), combine(x1[a:a + 1], x2[0:8]), fill))
    blocks.append(combine(x1[8:k], x2[0:1]))
    return jnp.concatenate(blocks, axis=0)


def _peer_topk_body(q_ref, keys_ref, e_ref, g_ref, *, tt):
    nt = (((1,), (1,)), ((), ()))
    es, gs = [], []
    for h in range(PEER_HEADS):
        halves = []
        for p in range(2):
            hp = 2 * h + p
            qhp = q_ref[:, hp * 128:(hp + 1) * 128].astype(BF16)
            s = lax.dot_general(keys_ref[hp], qhp, nt, preferred_element_type=F32)
            halves.append(_topk_rows(s, None, PEER_TOPK))
        (s1, i1), (s2, i2) = halves
        cand = _pair_candidates(s1, s2, lambda x, y: x + y, -jnp.inf)
        expert = _pair_candidates(i1, i2, lambda x, y: x * N_KEYS + y, -1)
        sc, e = _topk_rows(cand, expert, PEER_TOPK)
        ex = jnp.exp(sc - sc[0:1])
        es.append(e)
        gs.append(ex / jnp.sum(ex, axis=0, keepdims=True))
    e_ref[...] = jnp.concatenate(es, axis=0).T
    g_ref[...] = jnp.concatenate(gs, axis=0).T


def _peer_topk(q, sub_keys, *, tt=128):
    t, qw = q.shape
    keys = sub_keys.reshape(2 * PEER_HEADS, N_KEYS, qw // (2 * PEER_HEADS)).astype(BF16)
    return pl.pallas_call(
        functools.partial(_peer_topk_body, tt=tt),
        grid=(t // tt,),
        in_specs=[pl.BlockSpec((tt, qw), lambda i: (i, 0)),
                  pl.BlockSpec(keys.shape, lambda i: (0, 0, 0))],
        out_specs=[pl.BlockSpec((tt, PEER_SEL), lambda i: (i, 0)),
                   pl.BlockSpec((tt, PEER_SEL), lambda i: (i, 0))],
        out_shape=[jax.ShapeDtypeStruct((t, PEER_SEL), jnp.int32),
                   jax.ShapeDtypeStruct((t, PEER_SEL), F32)],
        compiler_params=_cparams("parallel"),
        name="peer_topk",
    )(q, keys)


SC_CORES = 2
SC_SUBCORES = 16
SC_LANES = 16
SC_TILES = SC_CORES * SC_SUBCORES
ROW_SUB = 8
ROW_LANE = 128
VECS_PER_SUB = ROW_LANE // SC_LANES
D_BLOCKS = 4
VECS_PER_BLOCK = ROW_SUB * VECS_PER_SUB // D_BLOCKS


def _sc_gelu(s):
    z = 0.7978845608028654 * (s + 0.044715 * (s * s * s))
    tanh_z = 1.0 - 2.0 / (jnp.exp(2.0 * z) + 1.0)
    return 0.5 * s * (1.0 + tanh_z)


def _vec_slot(jb, k):
    v = jb * VECS_PER_BLOCK + k
    return v // VECS_PER_SUB, pl.ds((v % VECS_PER_SUB) * SC_LANES, SC_LANES)


SC_TOKB = 8
PART_SUMS = 4
SC_RING = 3


def _peer_sc_body(e_hbm, g_hbm, x_hbm, u_hbm, v_hbm, o_hbm, idx_v, g_v, x_v, out_v, acc_v, w_v, ubuf, vbuf,
                  sem_u, sem_v, sem_o, *, batches_per_tile):
    wid = lax.axis_index("s") * SC_CORES + lax.axis_index("c")
    base = wid * batches_per_tile
    lane = lax.iota(jnp.int32, SC_LANES)
    n_groups = PEER_SEL // SC_LANES
    gpb = SC_TOKB * n_groups

    def sel16(ref, gi):
        return ref.at[gi // n_groups, pl.ds((gi % n_groups) * SC_LANES, SC_LANES)]

    def gather(gi, slot):
        return (pltpu.make_async_copy(u_hbm.at[sel16(idx_v, gi)], ubuf.at[slot], sem_u.at[slot]),
                pltpu.make_async_copy(v_hbm.at[sel16(idx_v, gi)], vbuf.at[slot], sem_v.at[slot]))

    def start(gi, slot):
        for cp in gather(gi, slot):
            cp.start()

    def out_copy(batch, par):
        return pltpu.make_async_copy(out_v.at[par], o_hbm.at[batch], sem_o.at[par])

    def u_phase(slot, t, gi):
        for jb in range(D_BLOCKS):
            xs = []
            for k in range(VECS_PER_BLOCK):
                sub, ls = _vec_slot(jb, k)
                xs.append(x_v[sub, t, ls])

            @plsc.parallel_loop(0, SC_LANES, unroll=2)
            def _(r, jb=jb, xs=xs):
                parts = []
                per = VECS_PER_BLOCK // PART_SUMS
                for q in range(PART_SUMS):
                    p = xs[q * per] * ubuf[(slot, r) + _vec_slot(jb, q * per)]
                    for k in range(q * per + 1, (q + 1) * per):
                        p = p + xs[k] * ubuf[(slot, r) + _vec_slot(jb, k)]
                    parts.append(p)
                a = (parts[0] + parts[1]) + (parts[2] + parts[3])
                if jb > 0:
                    a = a + acc_v[r, :]
                acc_v[r, :] = a
        cols = [plsc.load_gather(acc_v, [lane, jnp.full((SC_LANES,), l, jnp.int32)]) for l in range(SC_LANES)]
        while len(cols) > 1:
            cols = [cols[i] + cols[i + 1] for i in range(0, len(cols), 2)]
        w_v[...] = sel16(g_v, gi)[...] * _sc_gelu(cols[0])

    def v_phase(slot, par, t):
        for jb in range(D_BLOCKS):
            slots = [_vec_slot(jb, k) for k in range(VECS_PER_BLOCK)]
            init = tuple(out_v[par, sub, t, ls] for sub, ls in slots)

            def add_row(r, os, jb=jb):
                wb = plsc.load_gather(w_v, [jnp.full((SC_LANES,), r, jnp.int32)])
                return tuple(os[k] + wb * vbuf[(slot, r) + _vec_slot(jb, k)] for k in range(VECS_PER_BLOCK))

            os = plsc.parallel_loop(0, SC_LANES, unroll=2, carry=init)(add_row)
            for (sub, ls), o in zip(slots, os):
                out_v[par, sub, t, ls] = o

    @pl.loop(0, batches_per_tile)
    def _(bi):
        batch = base + bi
        par = bi % 2
        pltpu.sync_copy(e_hbm.at[pl.ds(batch * SC_TOKB, SC_TOKB)], idx_v)
        pltpu.sync_copy(g_hbm.at[pl.ds(batch * SC_TOKB, SC_TOKB)], g_v)
        pltpu.sync_copy(x_hbm.at[batch], x_v)
        for gi0 in range(SC_RING - 1):
            start(gi0, gi0)

        @pl.when(bi >= 2)
        def _():
            out_copy(batch, par).wait()

        def step(gi, slot):
            gi = jnp.asarray(gi, jnp.int32)
            t = gi // n_groups

            @pl.when(gi + SC_RING - 1 < gpb)
            def _():
                start(gi + SC_RING - 1, (slot + SC_RING - 1) % SC_RING)

            @pl.when(gi % n_groups == 0)
            def _():
                for j in range(ROW_SUB):
                    for l in range(VECS_PER_SUB):
                        out_v[par, j, t, pl.ds(l * SC_LANES, SC_LANES)] = jnp.zeros((SC_LANES,), F32)

            cu, cv = gather(gi, slot)
            cu.wait()
            u_phase(slot, t, gi)
            cv.wait()
            v_phase(slot, par, t)

        @pl.loop(0, gpb // SC_RING)
        def _(it):
            for slot in range(SC_RING):
                step(it * SC_RING + slot, slot)

        for gi in range(gpb - gpb % SC_RING, gpb):
            step(gi, gi % SC_RING)

        out_copy(batch, par).start()

    for par in range(2):
        out_copy(base, par).wait()


def _expert_rows(table):
    return table.reshape(table.shape[0], ROW_SUB, ROW_LANE)


def _peer_experts(e, g, xm, u3, v3):
    t, d = xm.shape
    assert d == ROW_SUB * ROW_LANE and t % (SC_TILES * SC_TOKB * 2) == 0 and e.shape == (t, PEER_SEL)
    n_batches = t // SC_TOKB

    def tiled(a):
        return a.reshape(n_batches, SC_TOKB, ROW_SUB, ROW_LANE).transpose(0, 2, 1, 3)

    call = pl.kernel(
        functools.partial(_peer_sc_body, batches_per_tile=n_batches // SC_TILES),
        out_type=jax.ShapeDtypeStruct((n_batches, ROW_SUB, SC_TOKB, ROW_LANE), F32),
        mesh=plsc.VectorSubcoreMesh(core_axis_name="c", subcore_axis_name="s"),
        scratch_types=[pltpu.VMEM((SC_TOKB, PEER_SEL), jnp.int32),
                       pltpu.VMEM((SC_TOKB, PEER_SEL), F32),
                       pltpu.VMEM((ROW_SUB, SC_TOKB, ROW_LANE), F32),
                       pltpu.VMEM((2, ROW_SUB, SC_TOKB, ROW_LANE), F32),
                       pltpu.VMEM((SC_LANES, SC_LANES), F32),
                       pltpu.VMEM((SC_LANES,), F32),
                       pltpu.VMEM((SC_RING, SC_LANES, ROW_SUB, ROW_LANE), F32),
                       pltpu.VMEM((SC_RING, SC_LANES, ROW_SUB, ROW_LANE), F32),
                       pltpu.SemaphoreType.DMA((SC_RING,)),
                       pltpu.SemaphoreType.DMA((SC_RING,)),
                       pltpu.SemaphoreType.DMA((2,))],
        compiler_params=pltpu.CompilerParams(needs_layout_passes=False),
        name="peer_experts_sc",
    )
    out = call(e, g, tiled(xm), u3, v3)
    return out.transpose(0, 2, 1, 3).reshape(t, d)


def _peer_select(q, sub_keys):
    return tuple(_peer_topk(q, sub_keys))


def _final_body(h_ref, delta_ref, gate_ref, g_ref, o_ref):
    h = h_ref[...] + gate_ref[0] * delta_ref[...]
    o_ref[...] = _rms(h) * g_ref[...]


def _final(h, delta, gate, gain, *, rows_per_group, tm=512):
    t, d = h.shape
    row = pl.BlockSpec((tm, d), lambda i: (i, 0))
    return pl.pallas_call(
        _final_body,
        grid=(t // tm,),
        in_specs=[row, row, pl.BlockSpec((1, 1, d), lambda i: ((i * tm) // rows_per_group, 0, 0)),
                  pl.BlockSpec((1, d), lambda i: (0, 0))],
        out_specs=row,
        out_shape=jax.ShapeDtypeStruct((t, d), F32),
        compiler_params=_cparams("parallel"),
        name="final_norm",
    )(h, delta, gate, gain.reshape(1, d))


def kernel(x, c, ctx, c_ctx, w_mod, b_mod, attn_w_in, attn_w_out, lambda_q1, lambda_k1, lambda_q2, lambda_k2,
           subln_g, na_rpb, chunk_w_in, chunk_b_in, chunk_ln_g, chunk_ln_b, chunk_w_s, chunk_b_s, chunk_w_out,
           peer_w_query, peer_sub_keys, peer_u, peer_v, final_norm_g):
    b, n, d = x.shape
    n_ctx = ctx.shape[1]

    cond = jnp.concatenate([c, c_ctx[None], jnp.zeros((-(b + 1) % 8, d), F32)], axis=0)
    mods = [_adaln(cond, w_mod[i], b_mod[i]) for i in range(2)]
    m0, m1 = mods

    w_in = attn_w_in[0].astype(BF16)
    w_out = attn_w_out[0].astype(BF16)
    w_q0 = peer_w_query[0].astype(BF16)
    w_q1 = peer_w_query[1].astype(BF16)
    w_gin = chunk_w_in[0].astype(BF16)
    w_gout = chunk_w_out[0].astype(BF16)
    rope = _rope_tables(n, 512)
    bias_tabs = _na_bias_tables(na_rpb[0], n // GRID_W)
    lam_init = 0.8 - 0.6 * math.exp(-0.3 * 0)
    lam = (jnp.exp(jnp.sum(lambda_q1[0] * lambda_k1[0])) - jnp.exp(jnp.sum(lambda_q2[0] * lambda_k2[0]))
           + lam_init).astype(F32)
    ka = A_HEADS * 2 * A_QK_DIM

    def cx(m, k):
        return m[b:b + 1, k * d:(k + 1) * d].reshape(1, 1, d)

    n_chunks = PIPE_CHUNKS if b % PIPE_CHUNKS == 0 else 1
    bc = b // n_chunks
    t = bc * n

    def lat(ci, m, k):
        return m[ci * bc:(ci + 1) * bc, k * d:(k + 1) * d].reshape(bc, 1, d)

    def stage_a(ci, x2):
        zc = _mod_mm(ctx[ci * bc:(ci + 1) * bc].reshape(bc * n_ctx, d), cx(m0, 0), cx(m0, 1), w_in,
                     rows_per_group=n_ctx, tm=256, name="attn_in_ctx")[0]
        z = _mod_mm(x2, lat(ci, m0, 0), lat(ci, m0, 1), w_in, rows_per_group=n, tm=256, rope=rope,
                    rope_chunks=(0, 1), scale_chunks=(0, 3), name="attn_in")[0]
        z3 = z.reshape(bc, n, -1)
        zc3 = zc.reshape(bc, n_ctx, -1)
        oa = _diff_attn(z3, zc3, lam, subln_g[0], lam_init)
        ob = _na_attn(z3, zc3, bias_tabs)
        h1 = _out_proj(x2, lat(ci, m0, 2), oa.reshape(t, ka), ob.reshape(t, -1), w_out[:ka], w_out[ka:],
                       rows_per_group=n)
        q0, xm0 = _mod_mm(h1, lat(ci, m0, 3), lat(ci, m0, 4), w_q0, rows_per_group=n, tm=256, emit_xm=True,
                          name="peer_q0")
        return h1, xm0, _peer_select(q0, peer_sub_keys[0])

    def stage_b(ci, h1, p0):
        zg, h1b = _mod_mm(h1, lat(ci, m1, 0), lat(ci, m1, 1), w_gin, rows_per_group=n, tm=256, delta=p0,
                          gate=lat(ci, m0, 5), bias=chunk_b_in[0], act="gelu", name="gmlp_in")
        h2 = _gmlp_tail(zg, h1b, lat(ci, m1, 2), chunk_ln_g[0], chunk_ln_b[0], chunk_w_s[0], chunk_b_s[0], w_gout,
                        rows_per_group=n)
        q1, xm1 = _mod_mm(h2, lat(ci, m1, 3), lat(ci, m1, 4), w_q1, rows_per_group=n, tm=256, emit_xm=True,
                          name="peer_q1")
        return h2, xm1, _peer_select(q1, peer_sub_keys[1])

    tables = [(_expert_rows(peer_u[i]), _expert_rows(peer_v[i])) for i in range(2)]

    h1s, p0s = [], []
    prev_sel = None
    for ci in range(n_chunks):
        x2 = x[ci * bc:(ci + 1) * bc].reshape(t, d)
        if prev_sel is not None:
            x2, prev_sel, tables = lax.optimization_barrier((x2, prev_sel, tables))
            p0s.append(_peer_experts(*prev_sel, xm_prev, *tables[0]))
        h1, xm_prev, prev_sel = stage_a(ci, x2)
        h1s.append(h1)
    p0s.append(_peer_experts(*prev_sel, xm_prev, *tables[0]))

    h2s, p1s = [], []
    prev_sel = None
    for ci in range(n_chunks):
        p0 = p0s[ci]
        if prev_sel is not None:
            p0, prev_sel = lax.optimization_barrier((p0, prev_sel))
            p1s.append(_peer_experts(*prev_sel, xm_prev, *tables[1]))
        h2, xm_prev, prev_sel = stage_b(ci, h1s[ci], p0)
        h2s.append(h2)
    p1s.append(_peer_experts(*prev_sel, xm_prev, *tables[1]))

    outs = [_final(h2s[ci], p1s[ci], lat(ci, m1, 5), final_norm_g, rows_per_group=n).reshape(bc, n, d)
            for ci in range(n_chunks)]
    return outs[0] if n_chunks == 1 else jnp.concatenate(outs, axis=0)
```

```python
import functools
import math

import numpy as np
import jax
import jax.numpy as jnp
from jax import lax
from jax.experimental import pallas as pl
from jax.experimental.pallas import tpu as pltpu
from jax.experimental.pallas import tpu_sc as plsc

F32 = jnp.float32
BF16 = jnp.bfloat16

EPS = 1e-6
GRID_W = 64
NA_ROWS = 8
NA_COLS = 16
ROPE_BASE = 10000.0
A_HEADS = 4
A_QK_DIM = 64
B_HEADS = 8
B_HEAD_DIM = 64
PEER_HEADS = 8
N_KEYS = 128
PEER_TOPK = 16
PEER_SEL = PEER_HEADS * PEER_TOPK
CHUNK = 128
C_GROUPS = 8
MASK_VALUE = -1e30

VMEM_LIMIT = 56 * 1024 * 1024
PIPE_BATCHES = (1, 2, 2, 3)
PEER_TC_SHARE = 3 / 32


def _cparams(*sem):
    return pltpu.CompilerParams(dimension_semantics=sem, vmem_limit_bytes=VMEM_LIMIT)


def _rms(x):
    return x * lax.rsqrt(jnp.mean(x * x, axis=-1, keepdims=True) + EPS)


def _gelu(x):
    return jax.nn.gelu(x)


def _adaln_body(c_ref, w_ref, b_ref, o_ref):
    cnd = c_ref[...]
    a = (cnd * jax.nn.sigmoid(cnd)).astype(BF16)
    o_ref[...] = jnp.dot(a, w_ref[...].astype(BF16), preferred_element_type=F32) + b_ref[...]


def _adaln(cond, w, b):
    m, d = cond.shape
    n = w.shape[1]
    tn = 1024
    return pl.pallas_call(
        _adaln_body,
        grid=(n // tn,),
        in_specs=[pl.BlockSpec((m, d), lambda j: (0, 0)),
                  pl.BlockSpec((d, tn), lambda j: (0, j)),
                  pl.BlockSpec((1, tn), lambda j: (0, j))],
        out_specs=pl.BlockSpec((m, tn), lambda j: (0, j)),
        out_shape=jax.ShapeDtypeStruct((m, n), F32),
        compiler_params=_cparams("parallel"),
        name="adaln",
    )(cond, w, b.reshape(1, n))


def _mod_mm_body(*refs, has_delta, has_bias, rope_chunks, scale_chunks, act, emit_xm, tn, n_out):
    it = iter(refs)
    h_ref = next(it)
    delta_ref = gate_ref = None
    if has_delta:
        delta_ref, gate_ref = next(it), next(it)
    shift_ref, scale_ref, w_ref = next(it), next(it), next(it)
    b_ref = next(it) if has_bias else None
    cos_ref = sin_ref = None
    if rope_chunks:
        cos_ref, sin_ref = next(it), next(it)
    y_ref = next(it)
    hn_ref = next(it) if has_delta else None
    xm_ref = next(it) if emit_xm else None

    h = h_ref[...]
    if has_delta:
        h = h + gate_ref[0] * delta_ref[...]
        hn_ref[...] = h
    xm = _rms(h) * (1.0 + scale_ref[0]) + shift_ref[0]
    if emit_xm:
        xm_ref[...] = xm
    xb = xm.astype(BF16)
    for c in range(n_out // tn):
        cols = slice(c * tn, (c + 1) * tn)
        y = jnp.dot(xb, w_ref[:, cols], preferred_element_type=F32)
        if has_bias:
            y = y + b_ref[:, cols]
        if c in rope_chunks:
            lane = lax.broadcasted_iota(jnp.int32, y.shape, 1)
            partner = jnp.where((lane & 16) == 0, pltpu.roll(y, tn - 16, 1), pltpu.roll(y, 16, 1))
            y = y * cos_ref[...] + partner * sin_ref[...]
        if c in scale_chunks:
            y = y * 0.125
        if act == "gelu":
            y = _gelu(y)
        y_ref[:, cols] = y


def _mod_mm(h, shift, scale, w, *, rows_per_group, tm, delta=None, gate=None, bias=None, rope=None,
            rope_chunks=(), scale_chunks=(), act="none", emit_xm=False, name="mod_mm"):
    t, d = h.shape
    n_out = w.shape[1]
    tn = 512
    groups = shift.shape[0]
    assert t % tm == 0 and rows_per_group % tm == 0 and n_out % tn == 0

    def gidx(i):
        return ((i * tm) // rows_per_group if groups > 1 else 0, 0, 0)

    row_spec = pl.BlockSpec((tm, d), lambda i: (i, 0))
    mod_spec = pl.BlockSpec((1, 1, d), gidx)
    args, specs = [h], [row_spec]
    if delta is not None:
        args += [delta, gate]
        specs += [row_spec, mod_spec]
    args += [shift, scale, w]
    specs += [mod_spec, mod_spec, pl.BlockSpec((d, n_out), lambda i: (0, 0))]
    if bias is not None:
        args.append(bias.reshape(1, n_out))
        specs.append(pl.BlockSpec((1, n_out), lambda i: (0, 0)))
    if rope_chunks:
        cos, sin = rope
        pos_blocks = cos.shape[0] // tm
        args += [cos, sin]
        specs += [pl.BlockSpec((tm, tn), lambda i: (i % pos_blocks, 0))] * 2
    out_shape = [jax.ShapeDtypeStruct((t, n_out), F32)]
    out_specs = [pl.BlockSpec((tm, n_out), lambda i: (i, 0))]
    if delta is not None:
        out_shape.append(jax.ShapeDtypeStruct((t, d), F32))
        out_specs.append(row_spec)
    if emit_xm:
        out_shape.append(jax.ShapeDtypeStruct((t, d), F32))
        out_specs.append(row_spec)
    body = functools.partial(_mod_mm_body, has_delta=delta is not None, has_bias=bias is not None,
                             rope_chunks=tuple(rope_chunks), scale_chunks=tuple(scale_chunks), act=act,
                             emit_xm=emit_xm, tn=tn, n_out=n_out)
    return pl.pallas_call(body, grid=(t // tm,), in_specs=specs, out_specs=out_specs, out_shape=out_shape,
                          compiler_params=_cparams("parallel"), name=name)(*args)


def _rope_tables(n_tokens, width):
    axis_dim = A_QK_DIM // 2
    inv_freq = 1.0 / (ROPE_BASE ** (jnp.arange(0, axis_dim, 2, dtype=F32) / axis_dim))
    t = jnp.arange(n_tokens)
    row = (t // GRID_W).astype(F32)
    col = (t % GRID_W).astype(F32)
    ang_r = row[:, None] * inv_freq
    ang_c = col[:, None] * inv_freq
    ang = jnp.concatenate([ang_r, ang_r, ang_c, ang_c], axis=1)
    sign = jnp.tile(jnp.concatenate([-jnp.ones(16, F32), jnp.ones(16, F32)]), 2)
    reps = width // A_QK_DIM
    return jnp.tile(jnp.cos(ang), (1, reps)), jnp.tile(jnp.sin(ang) * sign, (1, reps))


def _diff_attn_body(lam_ref, q_ref, k_ref, v_ref, kc_ref, vc_ref, g_ref, o_ref, *, out_scale):
    lam = lam_ref[0, 0]
    q = q_ref[0]
    k = k_ref[0].astype(BF16)
    kc = kc_ref[0].astype(BF16)
    lane = lax.broadcasted_iota(jnp.int32, q.shape, 1)
    nt = (((1,), (1,)), ((), ()))
    parts = []
    for m in range(2):
        keep = (lane < A_QK_DIM) if m == 0 else (lane >= A_QK_DIM)
        qm = jnp.where(keep, q, 0.0).astype(BF16)
        s_l = lax.dot_general(qm, k, nt, preferred_element_type=F32)
        s_c = lax.dot_general(qm, kc, nt, preferred_element_type=F32)
        mx = jnp.maximum(jnp.max(s_l, axis=-1, keepdims=True), jnp.max(s_c, axis=-1, keepdims=True))
        e_l = jnp.exp(s_l - mx)
        e_c = jnp.exp(s_c - mx)
        inv = 1.0 / (jnp.sum(e_l, axis=-1, keepdims=True) + jnp.sum(e_c, axis=-1, keepdims=True))
        parts.append((e_l, e_c, inv))
    (e1l, e1c, inv1), (e2l, e2c, inv2) = parts
    w2 = lam * inv2
    a_l = (e1l * inv1 - e2l * w2).astype(BF16)
    a_c = (e1c * inv1 - e2c * w2).astype(BF16)
    o = (jnp.dot(a_l, v_ref[0].astype(BF16), preferred_element_type=F32)
         + jnp.dot(a_c, vc_ref[0].astype(BF16), preferred_element_type=F32))
    o_ref[0] = _rms(o) * g_ref[...] * out_scale


def _diff_attn(z, zc, lam, subln_g, lam_init, *, tq=256):
    b, n, _ = z.shape
    c = zc.shape[1]
    kcol, vcol = A_HEADS, 2 * A_HEADS
    return pl.pallas_call(
        functools.partial(_diff_attn_body, out_scale=1.0 - lam_init),
        grid=(b, A_HEADS, n // tq),
        in_specs=[pl.BlockSpec((1, 1), lambda bi, h, i: (0, 0), memory_space=pltpu.SMEM),
                  pl.BlockSpec((1, tq, 128), lambda bi, h, i: (bi, i, h)),
                  pl.BlockSpec((1, n, 128), lambda bi, h, i: (bi, 0, kcol + h)),
                  pl.BlockSpec((1, n, 128), lambda bi, h, i: (bi, 0, vcol + h)),
                  pl.BlockSpec((1, c, 128), lambda bi, h, i: (bi, 0, kcol + h)),
                  pl.BlockSpec((1, c, 128), lambda bi, h, i: (bi, 0, vcol + h)),
                  pl.BlockSpec((1, 128), lambda bi, h, i: (0, 0))],
        out_specs=pl.BlockSpec((1, tq, 128), lambda bi, h, i: (bi, i, h)),
        out_shape=jax.ShapeDtypeStruct((b, n, A_HEADS * 128), F32),
        compiler_params=_cparams("parallel", "parallel", "arbitrary"),
        name="diff_attn",
    )(lam.reshape(1, 1), z, z, z, zc, zc, subln_g.reshape(1, 128))


NA_QROWS = 8
NA_KROWS = 16
NA_SEG = 4 * GRID_W


def _na_window_start(j, rows):
    return jnp.clip(NA_QROWS * j - NA_ROWS // 2, 0, rows - NA_KROWS)


def _na_bias_tables(rpb, rows):
    n_blocks = rows // NA_QROWS
    h = rpb.shape[0]
    ic = np.clip(np.arange(GRID_W)[None, :] - np.arange(GRID_W)[:, None] + (NA_COLS - 1), 0, 2 * NA_COLS - 2)
    toep = jnp.pad(rpb[:, :, ic], ((0, 0), (NA_KROWS, NA_KROWS), (0, 0), (0, 0)))
    tabs, oks = [], []
    for j in (0, 1, n_blocks - 1):
        ks = int(np.clip(NA_QROWS * j - NA_ROWS // 2, 0, rows - NA_KROWS))
        slabs = []
        for rq in range(NA_QROWS):
            dr0 = ks - (NA_QROWS * j + rq) + (NA_ROWS - 1) + NA_KROWS
            slab = toep[:, dr0:dr0 + NA_KROWS]
            slabs.append(slab.transpose(0, 2, 1, 3).reshape(h, GRID_W, NA_KROWS * GRID_W))
        tabs.append(jnp.concatenate(slabs, axis=1))
        r = (NA_QROWS * j + np.arange(NA_QROWS))[:, None, None, None]
        cq = np.arange(GRID_W)[None, :, None, None]
        kr = (ks + np.arange(NA_KROWS))[None, None, :, None]
        kc = np.arange(GRID_W)[None, None, None, :]
        r0 = np.clip(r - NA_ROWS // 2, 0, rows - NA_ROWS)
        c0 = np.clip(cq - NA_COLS // 2, 0, GRID_W - NA_COLS)
        ok = (kr >= r0) & (kr < r0 + NA_ROWS) & (kc >= c0) & (kc < c0 + NA_COLS)
        oks.append(ok.reshape(NA_QROWS * GRID_W, NA_KROWS * GRID_W))
    return jnp.where(np.stack(oks)[None], jnp.stack(tabs, axis=1), MASK_VALUE)


def _na_body(q_ref, k0, k1, k2, k3, v0, v1, v2, v3, kc_ref, vc_ref, bias_ref, o_ref):
    q = q_ref[0]
    lane = lax.broadcasted_iota(jnp.int32, q.shape, 1)
    nt = (((1,), (1,)), ((), ()))
    ks = [r[0].astype(BF16) for r in (k0, k1, k2, k3)]
    vs = [r[0].astype(BF16) for r in (v0, v1, v2, v3)]
    kc = kc_ref[0].astype(BF16)
    vc = vc_ref[0].astype(BF16)
    outs = []
    for hh in range(2):
        keep = (lane < B_HEAD_DIM) if hh == 0 else (lane >= B_HEAD_DIM)
        qh = jnp.where(keep, q, 0.0).astype(BF16)
        segs = [lax.dot_general(qh, ks[i], nt, preferred_element_type=F32)
                + bias_ref[hh, 0, :, i * NA_SEG:(i + 1) * NA_SEG] for i in range(4)]
        s_c = lax.dot_general(qh, kc, nt, preferred_element_type=F32)
        mx = jnp.max(s_c, axis=-1, keepdims=True)
        for s in segs:
            mx = jnp.maximum(mx, jnp.max(s, axis=-1, keepdims=True))
        e_c = jnp.exp(s_c - mx)
        den = jnp.sum(e_c, axis=-1, keepdims=True)
        es = []
        for s in segs:
            e = jnp.exp(s - mx)
            den = den + jnp.sum(e, axis=-1, keepdims=True)
            es.append(e)
        inv = 1.0 / den
        o = jnp.dot((e_c * inv).astype(BF16), vc, preferred_element_type=F32)
        for e, v in zip(es, vs):
            o = o + jnp.dot((e * inv).astype(BF16), v, preferred_element_type=F32)
        outs.append(o)
    o_ref[0] = jnp.where(lane < B_HEAD_DIM, outs[0], outs[1])


def _na_attn(z, zc, bias_tabs):
    b, n, _ = z.shape
    c = zc.shape[1]
    rows = n // GRID_W
    n_blocks = rows // NA_QROWS
    tq = NA_QROWS * GRID_W
    qcol, kcol, vcol = 12, 16, 20
    seg_rows = NA_SEG // GRID_W

    def kv_spec(col0, i):
        return pl.BlockSpec((1, NA_SEG, 128),
                            lambda hp, j, bi: (bi, _na_window_start(j, rows) // seg_rows + i, col0 + hp))

    def variant(j):
        return jnp.where(j == 0, 0, jnp.where(j == n_blocks - 1, 2, 1))

    in_specs = [pl.BlockSpec((1, tq, 128), lambda hp, j, bi: (bi, j, qcol + hp))]
    in_specs += [kv_spec(kcol, i) for i in range(4)]
    in_specs += [kv_spec(vcol, i) for i in range(4)]
    in_specs += [pl.BlockSpec((1, c, 128), lambda hp, j, bi: (bi, 0, kcol + hp)),
                 pl.BlockSpec((1, c, 128), lambda hp, j, bi: (bi, 0, vcol + hp)),
                 pl.BlockSpec((2, 1, tq, NA_KROWS * GRID_W), lambda hp, j, bi: (hp, variant(j), 0, 0))]
    return pl.pallas_call(
        _na_body,
        grid=(B_HEADS // 2, n_blocks, b),
        in_specs=in_specs,
        out_specs=pl.BlockSpec((1, tq, 128), lambda hp, j, bi: (bi, j, hp)),
        out_shape=jax.ShapeDtypeStruct((b, n, B_HEADS * B_HEAD_DIM), F32),
        compiler_params=_cparams("parallel", "parallel", "arbitrary"),
        name="na_attn",
    )(z, z, z, z, z, z, z, z, z, zc, zc, bias_tabs)


def _out_proj_body(h_ref, gate_ref, xa_ref, xb_ref, wa_ref, wb_ref, o_ref):
    y = (jnp.dot(xa_ref[...].astype(BF16), wa_ref[...], preferred_element_type=F32)
         + jnp.dot(xb_ref[...].astype(BF16), wb_ref[...], preferred_element_type=F32))
    o_ref[...] = h_ref[...] + gate_ref[0] * y


def _out_proj(h, gate, xa, xb, wa, wb, *, rows_per_group, tm=512):
    t, d = h.shape
    ka, kb = xa.shape[1], xb.shape[1]
    return pl.pallas_call(
        _out_proj_body,
        grid=(t // tm,),
        in_specs=[pl.BlockSpec((tm, d), lambda i: (i, 0)),
                  pl.BlockSpec((1, 1, d), lambda i: ((i * tm) // rows_per_group, 0, 0)),
                  pl.BlockSpec((tm, ka), lambda i: (i, 0)),
                  pl.BlockSpec((tm, kb), lambda i: (i, 0)),
                  pl.BlockSpec((ka, d), lambda i: (0, 0)),
                  pl.BlockSpec((kb, d), lambda i: (0, 0))],
        out_specs=pl.BlockSpec((tm, d), lambda i: (i, 0)),
        out_shape=jax.ShapeDtypeStruct((t, d), F32),
        compiler_params=_cparams("parallel"),
        name="out_proj",
    )(h, gate, xa, xb, wa, wb)


def _gmlp_tail_body(z_ref, h_ref, gate_ref, lng_ref, lnb_ref, ws_ref, bs_ref, wo_ref, o_ref, gated_ref, *, tm, width):
    gw = width // C_GROUPS
    for ci in range(tm // CHUNK):
        rows = slice(ci * CHUNK, (ci + 1) * CHUNK)
        v = z_ref[rows, width:]
        mu = jnp.mean(v, axis=-1, keepdims=True)
        var = jnp.mean(jnp.square(v - mu), axis=-1, keepdims=True)
        vn = ((v - mu) * lax.rsqrt(var + EPS)) * lng_ref[...] + lnb_ref[...]
        vb = vn.astype(BF16)
        for g in range(C_GROUPS):
            cols = slice(g * gw, (g + 1) * gw)
            sv = jnp.dot(ws_ref[g], vb[:, cols], preferred_element_type=F32) + bs_ref[g]
            gated_ref[rows, cols] = (z_ref[rows, cols] * sv).astype(BF16)
    y = jnp.dot(gated_ref[...], wo_ref[...], preferred_element_type=F32)
    o_ref[...] = h_ref[...] + gate_ref[0] * y


def _gmlp_tail(z, h, gate, ln_g, ln_b, w_s, b_s, w_out, *, rows_per_group, tm=512):
    t, d = h.shape
    width = z.shape[1] // 2
    gw = width // C_GROUPS
    bsb = jnp.broadcast_to(b_s[:, :, None], (C_GROUPS, CHUNK, gw))
    return pl.pallas_call(
        functools.partial(_gmlp_tail_body, tm=tm, width=width),
        grid=(t // tm,),
        in_specs=[pl.BlockSpec((tm, 2 * width), lambda i: (i, 0)),
                  pl.BlockSpec((tm, d), lambda i: (i, 0)),
                  pl.BlockSpec((1, 1, d), lambda i: ((i * tm) // rows_per_group, 0, 0)),
                  pl.BlockSpec((1, width), lambda i: (0, 0)),
                  pl.BlockSpec((1, width), lambda i: (0, 0)),
                  pl.BlockSpec((C_GROUPS, CHUNK, CHUNK), lambda i: (0, 0, 0)),
                  pl.BlockSpec((C_GROUPS, CHUNK, gw), lambda i: (0, 0, 0)),
                  pl.BlockSpec((width, d), lambda i: (0, 0))],
        out_specs=pl.BlockSpec((tm, d), lambda i: (i, 0)),
        out_shape=jax.ShapeDtypeStruct((t, d), F32),
        scratch_shapes=[pltpu.VMEM((tm, width), BF16)],
        compiler_params=_cparams("parallel"),
        name="gmlp_tail",
    )(z, h, gate, ln_g.reshape(1, width), ln_b.reshape(1, width), w_s.astype(BF16), bsb, w_out)


def _topk_rows(s, payload, k):
    n_rows = s.shape[0]
    riota = lax.broadcasted_iota(jnp.int32, s.shape, 0)
    vals, pays = [], []
    for _ in range(k):
        m = jnp.max(s, axis=0, keepdims=True)
        first = jnp.min(jnp.where(s == m, riota, n_rows), axis=0, keepdims=True)
        sel = riota == first
        vals.append(m)
        pays.append(first if payload is None else jnp.max(jnp.where(sel, payload, -1), axis=0, keepdims=True))
        s = jnp.where(sel, -jnp.inf, s)
    return jnp.concatenate(vals, axis=0), jnp.concatenate(pays, axis=0)


def _pair_candidates(x1, x2, combine, fill):
    k = PEER_TOPK
    sub = lax.broadcasted_iota(jnp.int32, (8, x1.shape[1]), 0)
    blocks = [combine(x1[0:1], x2)]
    for a in range(1, 8):
        blocks.append(jnp.where(sub < k // (a + 1), combine(x1[a:a + 1], x2[0:8]), fill))
    blocks.append(combine(x1[8:k], x2[0:1]))
    return jnp.concatenate(blocks, axis=0)


def _peer_topk_body(q_ref, keys_ref, e_ref, g_ref, *, tt):
    nt = (((1,), (1,)), ((), ()))
    es, gs = [], []
    for h in range(PEER_HEADS):
        halves = []
        for p in range(2):
            hp = 2 * h + p
            qhp = q_ref[:, hp * 128:(hp + 1) * 128].astype(BF16)
            s = lax.dot_general(keys_ref[hp], qhp, nt, preferred_element_type=F32)
            halves.append(_topk_rows(s, None, PEER_TOPK))
        (s1, i1), (s2, i2) = halves
        cand = _pair_candidates(s1, s2, lambda x, y: x + y, -jnp.inf)
        expert = _pair_candidates(i1, i2, lambda x, y: x * N_KEYS + y, -1)
        sc, e = _topk_rows(cand, expert, PEER_TOPK)
        ex = jnp.exp(sc - sc[0:1])
        es.append(e)
        gs.append(ex / jnp.sum(ex, axis=0, keepdims=True))
    e_ref[...] = jnp.concatenate(es, axis=0).T
    g_ref[...] = jnp.concatenate(gs, axis=0).T


def _peer_topk(q, sub_keys, *, tt=128):
    t, qw = q.shape
    keys = sub_keys.reshape(2 * PEER_HEADS, N_KEYS, qw // (2 * PEER_HEADS)).astype(BF16)
    return pl.pallas_call(
        functools.partial(_peer_topk_body, tt=tt),
        grid=(t // tt,),
        in_specs=[pl.BlockSpec((tt, qw), lambda i: (i, 0)),
                  pl.BlockSpec(keys.shape, lambda i: (0, 0, 0))],
        out_specs=[pl.BlockSpec((tt, PEER_SEL), lambda i: (i, 0)),
                   pl.BlockSpec((tt, PEER_SEL), lambda i: (i, 0))],
        out_shape=[jax.ShapeDtypeStruct((t, PEER_SEL), jnp.int32),
                   jax.ShapeDtypeStruct((t, PEER_SEL), F32)],
        compiler_params=_cparams("parallel"),
        name="peer_topk",
    )(q, keys)


SC_CORES = 2
SC_SUBCORES = 16
SC_LANES = 16
SC_TILES = SC_CORES * SC_SUBCORES
ROW_SUB = 8
ROW_LANE = 128
VECS_PER_SUB = ROW_LANE // SC_LANES
D_BLOCKS = 4
VECS_PER_BLOCK = ROW_SUB * VECS_PER_SUB // D_BLOCKS


def _sc_gelu(s):
    z = 0.7978845608028654 * (s + 0.044715 * (s * s * s))
    tanh_z = 1.0 - 2.0 / (jnp.exp(2.0 * z) + 1.0)
    return 0.5 * s * (1.0 + tanh_z)


def _vec_slot(jb, k):
    v = jb * VECS_PER_BLOCK + k
    return v // VECS_PER_SUB, pl.ds((v % VECS_PER_SUB) * SC_LANES, SC_LANES)


SC_TOKB = 8
PART_SUMS = 4
SC_RING = 3
SC_TOKEN_STEP = SC_TILES * SC_TOKB


def _peer_sc_body(e_hbm, g_hbm, x_hbm, u_hbm, v_hbm, o_hbm, idx_v, g_v, x_v, out_v, acc_v, w_v, ubuf, vbuf,
                  sem_u, sem_v, sem_o, *, batches_per_tile):
    wid = lax.axis_index("s") * SC_CORES + lax.axis_index("c")
    base = wid * batches_per_tile
    lane = lax.iota(jnp.int32, SC_LANES)
    n_groups = PEER_SEL // SC_LANES
    gpb = SC_TOKB * n_groups

    def sel16(ref, gi):
        return ref.at[gi // n_groups, pl.ds((gi % n_groups) * SC_LANES, SC_LANES)]

    def gather(gi, slot):
        return (pltpu.make_async_copy(u_hbm.at[sel16(idx_v, gi)], ubuf.at[slot], sem_u.at[slot]),
                pltpu.make_async_copy(v_hbm.at[sel16(idx_v, gi)], vbuf.at[slot], sem_v.at[slot]))

    def start(gi, slot):
        for cp in gather(gi, slot):
            cp.start()

    def out_copy(batch, par):
        return pltpu.make_async_copy(out_v.at[par], o_hbm.at[batch], sem_o.at[par])

    def u_phase(slot, t, gi):
        for jb in range(D_BLOCKS):
            xs = []
            for k in range(VECS_PER_BLOCK):
                sub, ls = _vec_slot(jb, k)
                xs.append(x_v[sub, t, ls])

            @plsc.parallel_loop(0, SC_LANES, unroll=2)
            def _(r, jb=jb, xs=xs):
                parts = []
                per = VECS_PER_BLOCK // PART_SUMS
                for q in range(PART_SUMS):
                    p = xs[q * per] * ubuf[(slot, r) + _vec_slot(jb, q * per)]
                    for k in range(q * per + 1, (q + 1) * per):
                        p = p + xs[k] * ubuf[(slot, r) + _vec_slot(jb, k)]
                    parts.append(p)
                a = (parts[0] + parts[1]) + (parts[2] + parts[3])
                if jb > 0:
                    a = a + acc_v[r, :]
                acc_v[r, :] = a
        cols = [plsc.load_gather(acc_v, [lane, jnp.full((SC_LANES,), l, jnp.int32)]) for l in range(SC_LANES)]
        while len(cols) > 1:
            cols = [cols[i] + cols[i + 1] for i in range(0, len(cols), 2)]
        w_v[...] = sel16(g_v, gi)[...] * _sc_gelu(cols[0])

    def v_phase(slot, par, t):
        for jb in range(D_BLOCKS):
            slots = [_vec_slot(jb, k) for k in range(VECS_PER_BLOCK)]
            init = tuple(out_v[par, sub, t, ls] for sub, ls in slots)

            def add_row(r, os, jb=jb):
                wb = plsc.load_gather(w_v, [jnp.full((SC_LANES,), r, jnp.int32)])
                return tuple(os[k] + wb * vbuf[(slot, r) + _vec_slot(jb, k)] for k in range(VECS_PER_BLOCK))

            os = plsc.parallel_loop(0, SC_LANES, unroll=2, carry=init)(add_row)
            for (sub, ls), o in zip(slots, os):
                out_v[par, sub, t, ls] = o

    @pl.loop(0, batches_per_tile)
    def _(bi):
        batch = base + bi
        par = bi % 2
        pltpu.sync_copy(e_hbm.at[pl.ds(batch * SC_TOKB, SC_TOKB)], idx_v)
        pltpu.sync_copy(g_hbm.at[pl.ds(batch * SC_TOKB, SC_TOKB)], g_v)
        pltpu.sync_copy(x_hbm.at[batch], x_v)
        for gi0 in range(SC_RING - 1):
            start(gi0, gi0)

        @pl.when(bi >= 2)
        def _():
            out_copy(batch, par).wait()

        def step(gi, slot):
            gi = jnp.asarray(gi, jnp.int32)
            t = gi // n_groups

            @pl.when(gi + SC_RING - 1 < gpb)
            def _():
                start(gi + SC_RING - 1, (slot + SC_RING - 1) % SC_RING)

            @pl.when(gi % n_groups == 0)
            def _():
                for j in range(ROW_SUB):
                    for l in range(VECS_PER_SUB):
                        out_v[par, j, t, pl.ds(l * SC_LANES, SC_LANES)] = jnp.zeros((SC_LANES,), F32)

            cu, cv = gather(gi, slot)
            cu.wait()
            u_phase(slot, t, gi)
            cv.wait()
            v_phase(slot, par, t)

        @pl.loop(0, gpb // SC_RING)
        def _(it):
            for slot in range(SC_RING):
                step(it * SC_RING + slot, slot)

        for gi in range(gpb - gpb % SC_RING, gpb):
            step(gi, gi % SC_RING)

        out_copy(batch, par).start()

    for par in range(2):
        out_copy(base, par).wait()


def _expert_rows(table):
    return table.reshape(table.shape[0], ROW_SUB, ROW_LANE)


def _peer_experts(e, g, xm, u3, v3):
    t, d = xm.shape
    assert d == ROW_SUB * ROW_LANE and t % SC_TOKEN_STEP == 0 and t >= 2 * SC_TOKEN_STEP and e.shape == (t, PEER_SEL)
    n_batches = t // SC_TOKB

    def tiled(a):
        return a.reshape(n_batches, SC_TOKB, ROW_SUB, ROW_LANE).transpose(0, 2, 1, 3)

    call = pl.kernel(
        functools.partial(_peer_sc_body, batches_per_tile=n_batches // SC_TILES),
        out_type=jax.ShapeDtypeStruct((n_batches, ROW_SUB, SC_TOKB, ROW_LANE), F32),
        mesh=plsc.VectorSubcoreMesh(core_axis_name="c", subcore_axis_name="s"),
        scratch_types=[pltpu.VMEM((SC_TOKB, PEER_SEL), jnp.int32),
                       pltpu.VMEM((SC_TOKB, PEER_SEL), F32),
                       pltpu.VMEM((ROW_SUB, SC_TOKB, ROW_LANE), F32),
                       pltpu.VMEM((2, ROW_SUB, SC_TOKB, ROW_LANE), F32),
                       pltpu.VMEM((SC_LANES, SC_LANES), F32),
                       pltpu.VMEM((SC_LANES,), F32),
                       pltpu.VMEM((SC_RING, SC_LANES, ROW_SUB, ROW_LANE), F32),
                       pltpu.VMEM((SC_RING, SC_LANES, ROW_SUB, ROW_LANE), F32),
                       pltpu.SemaphoreType.DMA((SC_RING,)),
                       pltpu.SemaphoreType.DMA((SC_RING,)),
                       pltpu.SemaphoreType.DMA((2,))],
        compiler_params=pltpu.CompilerParams(needs_layout_passes=False),
        name="peer_experts_sc",
    )
    out = call(e, g, tiled(xm), u3, v3)
    return out.transpose(0, 2, 1, 3).reshape(t, d)


PEER_TB = 8


def _peer_tc_body(e_cur, e_nxt, g_ref, x_ref, u_hbm, v_hbm, o_ref, ubuf, vbuf, sem, *, n_steps):
    i = pl.program_id(0)
    n_rows = PEER_TB * PEER_SEL
    slot = i % 2

    def issue(e_ref, dst):
        def one(r, carry):
            idx = e_ref[r // PEER_SEL, r % PEER_SEL]
            pltpu.make_async_copy(u_hbm.at[idx], ubuf.at[dst, :, r], sem.at[0, dst]).start()
            pltpu.make_async_copy(v_hbm.at[idx], vbuf.at[dst, :, r], sem.at[1, dst]).start()
            return carry
        lax.fori_loop(0, n_rows, one, 0, unroll=8)

    @pl.when(i == 0)
    def _():
        issue(e_cur, slot)

    @pl.when(i + 1 < n_steps)
    def _():
        issue(e_nxt, 1 - slot)

    pltpu.make_async_copy(ubuf.at[slot], ubuf.at[slot], sem.at[0, slot]).wait()
    pltpu.make_async_copy(vbuf.at[slot], vbuf.at[slot], sem.at[1, slot]).wait()

    for t in range(PEER_TB):
        rows = slice(t * PEER_SEL, (t + 1) * PEER_SEL)
        acc = ubuf[slot, 0, rows, :] * x_ref[t:t + 1, 0:ROW_LANE]
        for j in range(1, ROW_SUB):
            acc = acc + ubuf[slot, j, rows, :] * x_ref[t:t + 1, j * ROW_LANE:(j + 1) * ROW_LANE]
        w = g_ref[t] * _gelu(jnp.sum(acc, axis=-1, keepdims=True))
        for j in range(ROW_SUB):
            o_ref[t:t + 1, j * ROW_LANE:(j + 1) * ROW_LANE] = jnp.sum(w * vbuf[slot, j, rows, :], axis=0,
                                                                       keepdims=True)


def _peer_experts_tc(e, g, xm, u3, v3):
    t, d = xm.shape
    assert d == ROW_SUB * ROW_LANE and t % PEER_TB == 0
    n_steps = t // PEER_TB
    last = n_steps - 1
    n_rows = PEER_TB * PEER_SEL
    return pl.pallas_call(
        functools.partial(_peer_tc_body, n_steps=n_steps),
        grid=(n_steps,),
        in_specs=[pl.BlockSpec((PEER_TB, PEER_SEL), lambda i: (i, 0), memory_space=pltpu.SMEM),
                  pl.BlockSpec((PEER_TB, PEER_SEL), lambda i: (jnp.minimum(i + 1, last), 0), memory_space=pltpu.SMEM),
                  pl.BlockSpec((PEER_TB, PEER_SEL, 1), lambda i: (i, 0, 0)),
                  pl.BlockSpec((PEER_TB, d), lambda i: (i, 0)),
                  pl.BlockSpec(memory_space=pl.ANY),
                  pl.BlockSpec(memory_space=pl.ANY)],
        out_specs=pl.BlockSpec((PEER_TB, d), lambda i: (i, 0)),
        out_shape=jax.ShapeDtypeStruct((t, d), F32),
        scratch_shapes=[pltpu.VMEM((2, ROW_SUB, n_rows, ROW_LANE), F32),
                        pltpu.VMEM((2, ROW_SUB, n_rows, ROW_LANE), F32),
                        pltpu.SemaphoreType.DMA((2, 2))],
        compiler_params=_cparams("arbitrary"),
        name="peer_experts_tc",
    )(e, e, g.reshape(t, PEER_SEL, 1), xm, u3, v3)


def _peer_select(q, sub_keys):
    return tuple(_peer_topk(q, sub_keys))


def _final_body(h_ref, delta_ref, gate_ref, g_ref, o_ref):
    h = h_ref[...] + gate_ref[0] * delta_ref[...]
    o_ref[...] = _rms(h) * g_ref[...]


def _final(h, delta, gate, gain, *, rows_per_group, tm=512):
    t, d = h.shape
    row = pl.BlockSpec((tm, d), lambda i: (i, 0))
    return pl.pallas_call(
        _final_body,
        grid=(t // tm,),
        in_specs=[row, row, pl.BlockSpec((1, 1, d), lambda i: ((i * tm) // rows_per_group, 0, 0)),
                  pl.BlockSpec((1, d), lambda i: (0, 0))],
        out_specs=row,
        out_shape=jax.ShapeDtypeStruct((t, d), F32),
        compiler_params=_cparams("parallel"),
        name="final_norm",
    )(h, delta, gate, gain.reshape(1, d))


def kernel(x, c, ctx, c_ctx, w_mod, b_mod, attn_w_in, attn_w_out, lambda_q1, lambda_k1, lambda_q2, lambda_k2,
           subln_g, na_rpb, chunk_w_in, chunk_b_in, chunk_ln_g, chunk_ln_b, chunk_w_s, chunk_b_s, chunk_w_out,
           peer_w_query, peer_sub_keys, peer_u, peer_v, final_norm_g):
    b, n, d = x.shape
    n_ctx = ctx.shape[1]

    cond = jnp.concatenate([c, c_ctx[None], jnp.zeros((-(b + 1) % 8, d), F32)], axis=0)
    mods = [_adaln(cond, w_mod[i], b_mod[i]) for i in range(2)]
    m0, m1 = mods

    w_in = attn_w_in[0].astype(BF16)
    w_out = attn_w_out[0].astype(BF16)
    w_q0 = peer_w_query[0].astype(BF16)
    w_q1 = peer_w_query[1].astype(BF16)
    w_gin = chunk_w_in[0].astype(BF16)
    w_gout = chunk_w_out[0].astype(BF16)
    rope = _rope_tables(n, 512)
    bias_tabs = _na_bias_tables(na_rpb[0], n // GRID_W)
    lam_init = 0.8 - 0.6 * math.exp(-0.3 * 0)
    lam = (jnp.exp(jnp.sum(lambda_q1[0] * lambda_k1[0])) - jnp.exp(jnp.sum(lambda_q2[0] * lambda_k2[0]))
           + lam_init).astype(F32)
    ka = A_HEADS * 2 * A_QK_DIM

    def cx(m, k):
        return m[b:b + 1, k * d:(k + 1) * d].reshape(1, 1, d)

    sizes = PIPE_BATCHES if sum(PIPE_BATCHES) == b else (b,)
    starts = [sum(sizes[:ci]) for ci in range(len(sizes))]
    n_chunks = len(sizes)

    def rows(a, ci):
        return a[starts[ci]:starts[ci] + sizes[ci]]

    def lat(ci, m, k):
        return rows(m, ci)[:, k * d:(k + 1) * d].reshape(sizes[ci], 1, d)

    def stage_a(ci, x2):
        bc = sizes[ci]
        zc = _mod_mm(rows(ctx, ci).reshape(bc * n_ctx, d), cx(m0, 0), cx(m0, 1), w_in, rows_per_group=n_ctx,
                     tm=256, name="attn_in_ctx")[0]
        z = _mod_mm(x2, lat(ci, m0, 0), lat(ci, m0, 1), w_in, rows_per_group=n, tm=256, rope=rope,
                    rope_chunks=(0, 1), scale_chunks=(0, 3), name="attn_in")[0]
        z3 = z.reshape(bc, n, -1)
        zc3 = zc.reshape(bc, n_ctx, -1)
        oa = _diff_attn(z3, zc3, lam, subln_g[0], lam_init)
        ob = _na_attn(z3, zc3, bias_tabs)
        h1 = _out_proj(x2, lat(ci, m0, 2), oa.reshape(bc * n, ka), ob.reshape(bc * n, -1), w_out[:ka], w_out[ka:],
                       rows_per_group=n)
        q0, xm0 = _mod_mm(h1, lat(ci, m0, 3), lat(ci, m0, 4), w_q0, rows_per_group=n, tm=256, emit_xm=True,
                          name="peer_q0")
        return h1, xm0, _peer_select(q0, peer_sub_keys[0])

    def stage_b(ci, h1, p0):
        zg, h1b = _mod_mm(h1, lat(ci, m1, 0), lat(ci, m1, 1), w_gin, rows_per_group=n, tm=256, delta=p0,
                          gate=lat(ci, m0, 5), bias=chunk_b_in[0], act="gelu", name="gmlp_in")
        h2 = _gmlp_tail(zg, h1b, lat(ci, m1, 2), chunk_ln_g[0], chunk_ln_b[0], chunk_w_s[0], chunk_b_s[0], w_gout,
                        rows_per_group=n)
        q1, xm1 = _mod_mm(h2, lat(ci, m1, 3), lat(ci, m1, 4), w_q1, rows_per_group=n, tm=256, emit_xm=True,
                          name="peer_q1")
        return h2, xm1, _peer_select(q1, peer_sub_keys[1])

    tables = [(_expert_rows(peer_u[i]), _expert_rows(peer_v[i])) for i in range(2)]

    def experts(sel, xm, tabs):
        e, g = sel
        t = xm.shape[0]
        k = int(t * PEER_TC_SHARE) // SC_TOKEN_STEP * SC_TOKEN_STEP
        if k == 0 or (t - k) % SC_TOKEN_STEP:
            return _peer_experts(e, g, xm, *tabs)
        p_sc = _peer_experts(e[k:], g[k:], xm[k:], *tabs)
        p_tc = _peer_experts_tc(e[:k], g[:k], xm[:k], *tabs)
        return jnp.concatenate([p_tc, p_sc], axis=0)

    h1s, p0s = [], []
    prev_sel = None
    for ci in range(n_chunks):
        x2 = rows(x, ci).reshape(sizes[ci] * n, d)
        if prev_sel is not None:
            x2, prev_sel, tables = lax.optimization_barrier((x2, prev_sel, tables))
            p0s.append(experts(prev_sel, xm_prev, tables[0]))
        h1, xm_prev, prev_sel = stage_a(ci, x2)
        h1s.append(h1)
    p0s.append(experts(prev_sel, xm_prev, tables[0]))

    h2s, p1s = [], []
    prev_sel = None
    for ci in range(n_chunks):
        p0 = p0s[ci]
        if prev_sel is not None:
            p0, prev_sel = lax.optimization_barrier((p0, prev_sel))
            p1s.append(experts(prev_sel, xm_prev, tables[1]))
        h2, xm_prev, prev_sel = stage_b(ci, h1s[ci], p0)
        h2s.append(h2)
    p1s.append(experts(prev_sel, xm_prev, tables[1]))

    outs = [_final(h2s[ci], p1s[ci], lat(ci, m1, 5), final_norm_g, rows_per_group=n).reshape(sizes[ci], n, d)
            for ci in range(n_chunks)]
    return outs[0] if n_chunks == 1 else jnp.concatenate(outs, axis=0)
```

```python
import functools
import math

import numpy as np
import jax
import jax.numpy as jnp
from jax import lax
from jax.experimental import pallas as pl
from jax.experimental.pallas import tpu as pltpu
from jax.experimental.pallas import tpu_sc as plsc

F32 = jnp.float32
BF16 = jnp.bfloat16

EPS = 1e-6
GRID_W = 64
NA_ROWS = 8
NA_COLS = 16
ROPE_BASE = 10000.0
A_HEADS = 4
A_QK_DIM = 64
B_HEADS = 8
B_HEAD_DIM = 64
PEER_HEADS = 8
N_KEYS = 128
PEER_TOPK = 16
PEER_SEL = PEER_HEADS * PEER_TOPK
CHUNK = 128
C_GROUPS = 8
MASK_VALUE = -1e30

VMEM_LIMIT = 56 * 1024 * 1024
PIPE_BATCHES = (1, 2, 2, 3)
PEER_TC_SHARE = 15 / 64


def _cparams(*sem):
    return pltpu.CompilerParams(dimension_semantics=sem, vmem_limit_bytes=VMEM_LIMIT)


def _rms(x):
    return x * lax.rsqrt(jnp.mean(x * x, axis=-1, keepdims=True) + EPS)


def _gelu(x):
    return jax.nn.gelu(x)


def _adaln_body(c_ref, w_ref, b_ref, o_ref):
    cnd = c_ref[...]
    a = (cnd * jax.nn.sigmoid(cnd)).astype(BF16)
    o_ref[...] = jnp.dot(a, w_ref[...].astype(BF16), preferred_element_type=F32) + b_ref[...]


def _adaln(cond, w, b):
    m, d = cond.shape
    n = w.shape[1]
    tn = 1024
    return pl.pallas_call(
        _adaln_body,
        grid=(n // tn,),
        in_specs=[pl.BlockSpec((m, d), lambda j: (0, 0)),
                  pl.BlockSpec((d, tn), lambda j: (0, j)),
                  pl.BlockSpec((1, tn), lambda j: (0, j))],
        out_specs=pl.BlockSpec((m, tn), lambda j: (0, j)),
        out_shape=jax.ShapeDtypeStruct((m, n), F32),
        compiler_params=_cparams("parallel"),
        name="adaln",
    )(cond, w, b.reshape(1, n))


def _mod_mm_body(*refs, has_delta, has_bias, rope_chunks, scale_chunks, act, emit_xm, tn, n_out):
    it = iter(refs)
    h_ref = next(it)
    delta_ref = gate_ref = None
    if has_delta:
        delta_ref, gate_ref = next(it), next(it)
    shift_ref, scale_ref, w_ref = next(it), next(it), next(it)
    b_ref = next(it) if has_bias else None
    cos_ref = sin_ref = None
    if rope_chunks:
        cos_ref, sin_ref = next(it), next(it)
    y_ref = next(it)
    hn_ref = next(it) if has_delta else None
    xm_ref = next(it) if emit_xm else None

    h = h_ref[...]
    if has_delta:
        h = h + gate_ref[0] * delta_ref[...]
        hn_ref[...] = h
    xm = _rms(h) * (1.0 + scale_ref[0]) + shift_ref[0]
    if emit_xm:
        xm_ref[...] = xm
    xb = xm.astype(BF16)
    for c in range(n_out // tn):
        cols = slice(c * tn, (c + 1) * tn)
        y = jnp.dot(xb, w_ref[:, cols], preferred_element_type=F32)
        if has_bias:
            y = y + b_ref[:, cols]
        if c in rope_chunks:
            lane = lax.broadcasted_iota(jnp.int32, y.shape, 1)
            partner = jnp.where((lane & 16) == 0, pltpu.roll(y, tn - 16, 1), pltpu.roll(y, 16, 1))
            y = y * cos_ref[...] + partner * sin_ref[...]
        if c in scale_chunks:
            y = y * 0.125
        if act == "gelu":
            y = _gelu(y)
        y_ref[:, cols] = y


def _mod_mm(h, shift, scale, w, *, rows_per_group, tm, delta=None, gate=None, bias=None, rope=None,
            rope_chunks=(), scale_chunks=(), act="none", emit_xm=False, name="mod_mm"):
    t, d = h.shape
    n_out = w.shape[1]
    tn = 512
    groups = shift.shape[0]
    assert t % tm == 0 and rows_per_group % tm == 0 and n_out % tn == 0

    def gidx(i):
        return ((i * tm) // rows_per_group if groups > 1 else 0, 0, 0)

    row_spec = pl.BlockSpec((tm, d), lambda i: (i, 0))
    mod_spec = pl.BlockSpec((1, 1, d), gidx)
    args, specs = [h], [row_spec]
    if delta is not None:
        args += [delta, gate]
        specs += [row_spec, mod_spec]
    args += [shift, scale, w]
    specs += [mod_spec, mod_spec, pl.BlockSpec((d, n_out), lambda i: (0, 0))]
    if bias is not None:
        args.append(bias.reshape(1, n_out))
        specs.append(pl.BlockSpec((1, n_out), lambda i: (0, 0)))
    if rope_chunks:
        cos, sin = rope
        pos_blocks = cos.shape[0] // tm
        args += [cos, sin]
        specs += [pl.BlockSpec((tm, tn), lambda i: (i % pos_blocks, 0))] * 2
    out_shape = [jax.ShapeDtypeStruct((t, n_out), F32)]
    out_specs = [pl.BlockSpec((tm, n_out), lambda i: (i, 0))]
    if delta is not None:
        out_shape.append(jax.ShapeDtypeStruct((t, d), F32))
        out_specs.append(row_spec)
    if emit_xm:
        out_shape.append(jax.ShapeDtypeStruct((t, d), F32))
        out_specs.append(row_spec)
    body = functools.partial(_mod_mm_body, has_delta=delta is not None, has_bias=bias is not None,
                             rope_chunks=tuple(rope_chunks), scale_chunks=tuple(scale_chunks), act=act,
                             emit_xm=emit_xm, tn=tn, n_out=n_out)
    return pl.pallas_call(body, grid=(t // tm,), in_specs=specs, out_specs=out_specs, out_shape=out_shape,
                          compiler_params=_cparams("parallel"), name=name)(*args)


def _rope_tables(n_tokens, width):
    axis_dim = A_QK_DIM // 2
    inv_freq = 1.0 / (ROPE_BASE ** (jnp.arange(0, axis_dim, 2, dtype=F32) / axis_dim))
    t = jnp.arange(n_tokens)
    row = (t // GRID_W).astype(F32)
    col = (t % GRID_W).astype(F32)
    ang_r = row[:, None] * inv_freq
    ang_c = col[:, None] * inv_freq
    ang = jnp.concatenate([ang_r, ang_r, ang_c, ang_c], axis=1)
    sign = jnp.tile(jnp.concatenate([-jnp.ones(16, F32), jnp.ones(16, F32)]), 2)
    reps = width // A_QK_DIM
    return jnp.tile(jnp.cos(ang), (1, reps)), jnp.tile(jnp.sin(ang) * sign, (1, reps))


def _diff_attn_body(lam_ref, q_ref, k_ref, v_ref, kc_ref, vc_ref, g_ref, o_ref, *, out_scale):
    lam = lam_ref[0, 0]
    q = q_ref[0]
    k = k_ref[0].astype(BF16)
    kc = kc_ref[0].astype(BF16)
    lane = lax.broadcasted_iota(jnp.int32, q.shape, 1)
    nt = (((1,), (1,)), ((), ()))
    parts = []
    for m in range(2):
        keep = (lane < A_QK_DIM) if m == 0 else (lane >= A_QK_DIM)
        qm = jnp.where(keep, q, 0.0).astype(BF16)
        s_l = lax.dot_general(qm, k, nt, preferred_element_type=F32)
        s_c = lax.dot_general(qm, kc, nt, preferred_element_type=F32)
        mx = jnp.maximum(jnp.max(s_l, axis=-1, keepdims=True), jnp.max(s_c, axis=-1, keepdims=True))
        e_l = jnp.exp(s_l - mx)
        e_c = jnp.exp(s_c - mx)
        inv = 1.0 / (jnp.sum(e_l, axis=-1, keepdims=True) + jnp.sum(e_c, axis=-1, keepdims=True))
        parts.append((e_l, e_c, inv))
    (e1l, e1c, inv1), (e2l, e2c, inv2) = parts
    w2 = lam * inv2
    a_l = (e1l * inv1 - e2l * w2).astype(BF16)
    a_c = (e1c * inv1 - e2c * w2).astype(BF16)
    o = (jnp.dot(a_l, v_ref[0].astype(BF16), preferred_element_type=F32)
         + jnp.dot(a_c, vc_ref[0].astype(BF16), preferred_element_type=F32))
    o_ref[0] = _rms(o) * g_ref[...] * out_scale


def _diff_attn(z, zc, lam, subln_g, lam_init, *, tq=256):
    b, n, _ = z.shape
    c = zc.shape[1]
    kcol, vcol = A_HEADS, 2 * A_HEADS
    return pl.pallas_call(
        functools.partial(_diff_attn_body, out_scale=1.0 - lam_init),
        grid=(b, A_HEADS, n // tq),
        in_specs=[pl.BlockSpec((1, 1), lambda bi, h, i: (0, 0), memory_space=pltpu.SMEM),
                  pl.BlockSpec((1, tq, 128), lambda bi, h, i: (bi, i, h)),
                  pl.BlockSpec((1, n, 128), lambda bi, h, i: (bi, 0, kcol + h)),
                  pl.BlockSpec((1, n, 128), lambda bi, h, i: (bi, 0, vcol + h)),
                  pl.BlockSpec((1, c, 128), lambda bi, h, i: (bi, 0, kcol + h)),
                  pl.BlockSpec((1, c, 128), lambda bi, h, i: (bi, 0, vcol + h)),
                  pl.BlockSpec((1, 128), lambda bi, h, i: (0, 0))],
        out_specs=pl.BlockSpec((1, tq, 128), lambda bi, h, i: (bi, i, h)),
        out_shape=jax.ShapeDtypeStruct((b, n, A_HEADS * 128), F32),
        compiler_params=_cparams("parallel", "parallel", "arbitrary"),
        name="diff_attn",
    )(lam.reshape(1, 1), z, z, z, zc, zc, subln_g.reshape(1, 128))


NA_QROWS = 8
NA_KROWS = 16
NA_SEG = 4 * GRID_W


def _na_window_start(j, rows):
    return jnp.clip(NA_QROWS * j - NA_ROWS // 2, 0, rows - NA_KROWS)


def _na_bias_tables(rpb, rows):
    n_blocks = rows // NA_QROWS
    h = rpb.shape[0]
    ic = np.clip(np.arange(GRID_W)[None, :] - np.arange(GRID_W)[:, None] + (NA_COLS - 1), 0, 2 * NA_COLS - 2)
    toep = jnp.pad(rpb[:, :, ic], ((0, 0), (NA_KROWS, NA_KROWS), (0, 0), (0, 0)))
    tabs, oks = [], []
    for j in (0, 1, n_blocks - 1):
        ks = int(np.clip(NA_QROWS * j - NA_ROWS // 2, 0, rows - NA_KROWS))
        slabs = []
        for rq in range(NA_QROWS):
            dr0 = ks - (NA_QROWS * j + rq) + (NA_ROWS - 1) + NA_KROWS
            slab = toep[:, dr0:dr0 + NA_KROWS]
            slabs.append(slab.transpose(0, 2, 1, 3).reshape(h, GRID_W, NA_KROWS * GRID_W))
        tabs.append(jnp.concatenate(slabs, axis=1))
        r = (NA_QROWS * j + np.arange(NA_QROWS))[:, None, None, None]
        cq = np.arange(GRID_W)[None, :, None, None]
        kr = (ks + np.arange(NA_KROWS))[None, None, :, None]
        kc = np.arange(GRID_W)[None, None, None, :]
        r0 = np.clip(r - NA_ROWS // 2, 0, rows - NA_ROWS)
        c0 = np.clip(cq - NA_COLS // 2, 0, GRID_W - NA_COLS)
        ok = (kr >= r0) & (kr < r0 + NA_ROWS) & (kc >= c0) & (kc < c0 + NA_COLS)
        oks.append(ok.reshape(NA_QROWS * GRID_W, NA_KROWS * GRID_W))
    return jnp.where(np.stack(oks)[None], jnp.stack(tabs, axis=1), MASK_VALUE)


def _na_body(q_ref, k0, k1, k2, k3, v0, v1, v2, v3, kc_ref, vc_ref, bias_ref, o_ref):
    q = q_ref[0]
    lane = lax.broadcasted_iota(jnp.int32, q.shape, 1)
    nt = (((1,), (1,)), ((), ()))
    ks = [r[0].astype(BF16) for r in (k0, k1, k2, k3)]
    vs = [r[0].astype(BF16) for r in (v0, v1, v2, v3)]
    kc = kc_ref[0].astype(BF16)
    vc = vc_ref[0].astype(BF16)
    outs = []
    for hh in range(2):
        keep = (lane < B_HEAD_DIM) if hh == 0 else (lane >= B_HEAD_DIM)
        qh = jnp.where(keep, q, 0.0).astype(BF16)
        segs = [lax.dot_general(qh, ks[i], nt, preferred_element_type=F32)
                + bias_ref[hh, 0, :, i * NA_SEG:(i + 1) * NA_SEG] for i in range(4)]
        s_c = lax.dot_general(qh, kc, nt, preferred_element_type=F32)
        mx = jnp.max(s_c, axis=-1, keepdims=True)
        for s in segs:
            mx = jnp.maximum(mx, jnp.max(s, axis=-1, keepdims=True))
        e_c = jnp.exp(s_c - mx)
        den = jnp.sum(e_c, axis=-1, keepdims=True)
        es = []
        for s in segs:
            e = jnp.exp(s - mx)
            den = den + jnp.sum(e, axis=-1, keepdims=True)
            es.append(e)
        inv = 1.0 / den
        o = jnp.dot((e_c * inv).astype(BF16), vc, preferred_element_type=F32)
        for e, v in zip(es, vs):
            o = o + jnp.dot((e * inv).astype(BF16), v, preferred_element_type=F32)
        outs.append(o)
    o_ref[0] = jnp.where(lane < B_HEAD_DIM, outs[0], outs[1])


def _na_attn(z, zc, bias_tabs):
    b, n, _ = z.shape
    c = zc.shape[1]
    rows = n // GRID_W
    n_blocks = rows // NA_QROWS
    tq = NA_QROWS * GRID_W
    qcol, kcol, vcol = 12, 16, 20
    seg_rows = NA_SEG // GRID_W

    def kv_spec(col0, i):
        return pl.BlockSpec((1, NA_SEG, 128),
                            lambda hp, j, bi: (bi, _na_window_start(j, rows) // seg_rows + i, col0 + hp))

    def variant(j):
        return jnp.where(j == 0, 0, jnp.where(j == n_blocks - 1, 2, 1))

    in_specs = [pl.BlockSpec((1, tq, 128), lambda hp, j, bi: (bi, j, qcol + hp))]
    in_specs += [kv_spec(kcol, i) for i in range(4)]
    in_specs += [kv_spec(vcol, i) for i in range(4)]
    in_specs += [pl.BlockSpec((1, c, 128), lambda hp, j, bi: (bi, 0, kcol + hp)),
                 pl.BlockSpec((1, c, 128), lambda hp, j, bi: (bi, 0, vcol + hp)),
                 pl.BlockSpec((2, 1, tq, NA_KROWS * GRID_W), lambda hp, j, bi: (hp, variant(j), 0, 0))]
    return pl.pallas_call(
        _na_body,
        grid=(B_HEADS // 2, n_blocks, b),
        in_specs=in_specs,
        out_specs=pl.BlockSpec((1, tq, 128), lambda hp, j, bi: (bi, j, hp)),
        out_shape=jax.ShapeDtypeStruct((b, n, B_HEADS * B_HEAD_DIM), F32),
        compiler_params=_cparams("parallel", "parallel", "arbitrary"),
        name="na_attn",
    )(z, z, z, z, z, z, z, z, z, zc, zc, bias_tabs)


def _out_proj_body(h_ref, gate_ref, xa_ref, xb_ref, wa_ref, wb_ref, o_ref):
    y = (jnp.dot(xa_ref[...].astype(BF16), wa_ref[...], preferred_element_type=F32)
         + jnp.dot(xb_ref[...].astype(BF16), wb_ref[...], preferred_element_type=F32))
    o_ref[...] = h_ref[...] + gate_ref[0] * y


def _out_proj(h, gate, xa, xb, wa, wb, *, rows_per_group, tm=512):
    t, d = h.shape
    ka, kb = xa.shape[1], xb.shape[1]
    return pl.pallas_call(
        _out_proj_body,
        grid=(t // tm,),
        in_specs=[pl.BlockSpec((tm, d), lambda i: (i, 0)),
                  pl.BlockSpec((1, 1, d), lambda i: ((i * tm) // rows_per_group, 0, 0)),
                  pl.BlockSpec((tm, ka), lambda i: (i, 0)),
                  pl.BlockSpec((tm, kb), lambda i: (i, 0)),
                  pl.BlockSpec((ka, d), lambda i: (0, 0)),
                  pl.BlockSpec((kb, d), lambda i: (0, 0))],
        out_specs=pl.BlockSpec((tm, d), lambda i: (i, 0)),
        out_shape=jax.ShapeDtypeStruct((t, d), F32),
        compiler_params=_cparams("parallel"),
        name="out_proj",
    )(h, gate, xa, xb, wa, wb)


def _gmlp_tail_body(z_ref, h_ref, gate_ref, lng_ref, lnb_ref, ws_ref, bs_ref, wo_ref, o_ref, gated_ref, *, tm, width):
    gw = width // C_GROUPS
    for ci in range(tm // CHUNK):
        rows = slice(ci * CHUNK, (ci + 1) * CHUNK)
        v = z_ref[rows, width:]
        mu = jnp.mean(v, axis=-1, keepdims=True)
        var = jnp.mean(jnp.square(v - mu), axis=-1, keepdims=True)
        vn = ((v - mu) * lax.rsqrt(var + EPS)) * lng_ref[...] + lnb_ref[...]
        vb = vn.astype(BF16)
        for g in range(C_GROUPS):
            cols = slice(g * gw, (g + 1) * gw)
            sv = jnp.dot(ws_ref[g], vb[:, cols], preferred_element_type=F32) + bs_ref[g]
            gated_ref[rows, cols] = (z_ref[rows, cols] * sv).astype(BF16)
    y = jnp.dot(gated_ref[...], wo_ref[...], preferred_element_type=F32)
    o_ref[...] = h_ref[...] + gate_ref[0] * y


def _gmlp_tail(z, h, gate, ln_g, ln_b, w_s, b_s, w_out, *, rows_per_group, tm=512):
    t, d = h.shape
    width = z.shape[1] // 2
    gw = width // C_GROUPS
    bsb = jnp.broadcast_to(b_s[:, :, None], (C_GROUPS, CHUNK, gw))
    return pl.pallas_call(
        functools.partial(_gmlp_tail_body, tm=tm, width=width),
        grid=(t // tm,),
        in_specs=[pl.BlockSpec((tm, 2 * width), lambda i: (i, 0)),
                  pl.BlockSpec((tm, d), lambda i: (i, 0)),
                  pl.BlockSpec((1, 1, d), lambda i: ((i * tm) // rows_per_group, 0, 0)),
                  pl.BlockSpec((1, width), lambda i: (0, 0)),
                  pl.BlockSpec((1, width), lambda i: (0, 0)),
                  pl.BlockSpec((C_GROUPS, CHUNK, CHUNK), lambda i: (0, 0, 0)),
                  pl.BlockSpec((C_GROUPS, CHUNK, gw), lambda i: (0, 0, 0)),
                  pl.BlockSpec((width, d), lambda i: (0, 0))],
        out_specs=pl.BlockSpec((tm, d), lambda i: (i, 0)),
        out_shape=jax.ShapeDtypeStruct((t, d), F32),
        scratch_shapes=[pltpu.VMEM((tm, width), BF16)],
        compiler_params=_cparams("parallel"),
        name="gmlp_tail",
    )(z, h, gate, ln_g.reshape(1, width), ln_b.reshape(1, width), w_s.astype(BF16), bsb, w_out)


def _topk_rows(s, payload, k):
    n_rows = s.shape[0]
    riota = lax.broadcasted_iota(jnp.int32, s.shape, 0)
    vals, pays = [], []
    for _ in range(k):
        m = jnp.max(s, axis=0, keepdims=True)
        first = jnp.min(jnp.where(s == m, riota, n_rows), axis=0, keepdims=True)
        sel = riota == first
        vals.append(m)
        pays.append(first if payload is None else jnp.max(jnp.where(sel, payload, -1), axis=0, keepdims=True))
        s = jnp.where(sel, -jnp.inf, s)
    return jnp.concatenate(vals, axis=0), jnp.concatenate(pays, axis=0)


def _pair_candidates(x1, x2, combine, fill):
    k = PEER_TOPK
    sub = lax.broadcasted_iota(jnp.int32, (8, x1.shape[1]), 0)
    blocks = [combine(x1[0:1], x2)]
    for a in range(1, 8):
        blocks.append(jnp.where(sub < k // (a + 1), combine(x1[a:a + 1], x2[0:8]), fill))
    blocks.append(combine(x1[8:k], x2[0:1]))
    return jnp.concatenate(blocks, axis=0)


def _peer_topk_body(q_ref, keys_ref, e_ref, g_ref, *, tt):
    nt = (((1,), (1,)), ((), ()))
    es, gs = [], []
    for h in range(PEER_HEADS):
        halves = []
        for p in range(2):
            hp = 2 * h + p
            qhp = q_ref[:, hp * 128:(hp + 1) * 128].astype(BF16)
            s = lax.dot_general(keys_ref[hp], qhp, nt, preferred_element_type=F32)
            halves.append(_topk_rows(s, None, PEER_TOPK))
        (s1, i1), (s2, i2) = halves
        cand = _pair_candidates(s1, s2, lambda x, y: x + y, -jnp.inf)
        expert = _pair_candidates(i1, i2, lambda x, y: 2 * (x * N_KEYS + y), -1)
        sc, e = _topk_rows(cand, expert, PEER_TOPK)
        ex = jnp.exp(sc - sc[0:1])
        es.append(e)
        gs.append(ex / jnp.sum(ex, axis=0, keepdims=True))
    e_ref[...] = jnp.concatenate(es, axis=0).T
    g_ref[...] = jnp.concatenate(gs, axis=0).T


def _peer_topk(q, sub_keys, *, tt=128):
    t, qw = q.shape
    keys = sub_keys.reshape(2 * PEER_HEADS, N_KEYS, qw // (2 * PEER_HEADS)).astype(BF16)
    return pl.pallas_call(
        functools.partial(_peer_topk_body, tt=tt),
        grid=(t // tt,),
        in_specs=[pl.BlockSpec((tt, qw), lambda i: (i, 0)),
                  pl.BlockSpec(keys.shape, lambda i: (0, 0, 0))],
        out_specs=[pl.BlockSpec((tt, PEER_SEL), lambda i: (i, 0)),
                   pl.BlockSpec((tt, PEER_SEL), lambda i: (i, 0))],
        out_shape=[jax.ShapeDtypeStruct((t, PEER_SEL), jnp.int32),
                   jax.ShapeDtypeStruct((t, PEER_SEL), F32)],
        compiler_params=_cparams("parallel"),
        name="peer_topk",
    )(q, keys)


SC_CORES = 2
SC_SUBCORES = 16
SC_LANES = 16
SC_TILES = SC_CORES * SC_SUBCORES
ROW_SUB = 8
ROW_LANE = 128
VECS_PER_SUB = ROW_LANE // SC_LANES
D_BLOCKS = 4
VECS_PER_BLOCK = ROW_SUB * VECS_PER_SUB // D_BLOCKS


def _sc_gelu(s):
    z = 0.7978845608028654 * (s + 0.044715 * (s * s * s))
    tanh_z = 1.0 - 2.0 / (jnp.exp(2.0 * z) + 1.0)
    return 0.5 * s * (1.0 + tanh_z)


def _vec_slot(jb, k):
    v = jb * VECS_PER_BLOCK + k
    return v // VECS_PER_SUB, pl.ds((v % VECS_PER_SUB) * SC_LANES, SC_LANES)


SC_TOKB = 8
PART_SUMS = 4
SC_RING = 3
SC_TOKEN_STEP = SC_TILES * SC_TOKB


def _peer_sc_body(e_hbm, g_hbm, x_hbm, uv_hbm, o_hbm, idx_v, idx1_v, g_v, x_v, out_v, acc_v, w_v, ubuf, vbuf,
                  sem_u, sem_v, sem_o, *, batches_per_tile):
    wid = lax.axis_index("s") * SC_CORES + lax.axis_index("c")
    base = wid * batches_per_tile
    lane = lax.iota(jnp.int32, SC_LANES)
    n_groups = PEER_SEL // SC_LANES
    gpb = SC_TOKB * n_groups

    def sel16(ref, gi):
        return ref.at[gi // n_groups, pl.ds((gi % n_groups) * SC_LANES, SC_LANES)]

    def gather(gi, slot):
        return (pltpu.make_async_copy(uv_hbm.at[sel16(idx_v, gi)], ubuf.at[slot], sem_u.at[slot]),
                pltpu.make_async_copy(uv_hbm.at[sel16(idx1_v, gi)], vbuf.at[slot], sem_v.at[slot]))

    def start(gi, slot):
        for cp in gather(gi, slot):
            cp.start()

    def out_copy(batch, par):
        return pltpu.make_async_copy(out_v.at[par], o_hbm.at[batch], sem_o.at[par])

    def u_phase(slot, t, gi):
        for jb in range(D_BLOCKS):
            xs = []
            for k in range(VECS_PER_BLOCK):
                sub, ls = _vec_slot(jb, k)
                xs.append(x_v[sub, t, ls])

            @plsc.parallel_loop(0, SC_LANES, unroll=2)
            def _(r, jb=jb, xs=xs):
                parts = []
                per = VECS_PER_BLOCK // PART_SUMS
                for q in range(PART_SUMS):
                    p = xs[q * per] * ubuf[(slot, r) + _vec_slot(jb, q * per)]
                    for k in range(q * per + 1, (q + 1) * per):
                        p = p + xs[k] * ubuf[(slot, r) + _vec_slot(jb, k)]
                    parts.append(p)
                a = (parts[0] + parts[1]) + (parts[2] + parts[3])
                if jb > 0:
                    a = a + acc_v[r, :]
                acc_v[r, :] = a
        cols = [plsc.load_gather(acc_v, [lane, jnp.full((SC_LANES,), l, jnp.int32)]) for l in range(SC_LANES)]
        while len(cols) > 1:
            cols = [cols[i] + cols[i + 1] for i in range(0, len(cols), 2)]
        w_v[...] = sel16(g_v, gi)[...] * _sc_gelu(cols[0])

    def v_phase(slot, par, t):
        for jb in range(D_BLOCKS):
            slots = [_vec_slot(jb, k) for k in range(VECS_PER_BLOCK)]
            init = tuple(out_v[par, sub, t, ls] for sub, ls in slots)

            def add_row(r, os, jb=jb):
                wb = plsc.load_gather(w_v, [jnp.full((SC_LANES,), r, jnp.int32)])
                return tuple(os[k] + wb * vbuf[(slot, r) + _vec_slot(jb, k)] for k in range(VECS_PER_BLOCK))

            os = plsc.parallel_loop(0, SC_LANES, unroll=2, carry=init)(add_row)
            for (sub, ls), o in zip(slots, os):
                out_v[par, sub, t, ls] = o

    @pl.loop(0, batches_per_tile)
    def _(bi):
        batch = base + bi
        par = bi % 2
        pltpu.sync_copy(e_hbm.at[pl.ds(batch * SC_TOKB, SC_TOKB)], idx_v)
        pltpu.sync_copy(g_hbm.at[pl.ds(batch * SC_TOKB, SC_TOKB)], g_v)
        pltpu.sync_copy(x_hbm.at[batch], x_v)
        for tt in range(SC_TOKB):
            for c in range(n_groups):
                cols = pl.ds(c * SC_LANES, SC_LANES)
                idx1_v[tt, cols] = idx_v[tt, cols] + 1
        for gi0 in range(SC_RING - 1):
            start(gi0, gi0)

        @pl.when(bi >= 2)
        def _():
            out_copy(batch, par).wait()

        def step(gi, slot):
            gi = jnp.asarray(gi, jnp.int32)
            t = gi // n_groups

            @pl.when(gi + SC_RING - 1 < gpb)
            def _():
                start(gi + SC_RING - 1, (slot + SC_RING - 1) % SC_RING)

            @pl.when(gi % n_groups == 0)
            def _():
                for j in range(ROW_SUB):
                    for l in range(VECS_PER_SUB):
                        out_v[par, j, t, pl.ds(l * SC_LANES, SC_LANES)] = jnp.zeros((SC_LANES,), F32)

            cu, cv = gather(gi, slot)
            cu.wait()
            u_phase(slot, t, gi)
            cv.wait()
            v_phase(slot, par, t)

        @pl.loop(0, gpb // SC_RING)
        def _(it):
            for slot in range(SC_RING):
                step(it * SC_RING + slot, slot)

        for gi in range(gpb - gpb % SC_RING, gpb):
            step(gi, gi % SC_RING)

        out_copy(batch, par).start()

    for par in range(2):
        out_copy(base, par).wait()


def _expert_rows(u, v):
    n_e = u.shape[0]
    rows = jnp.stack([u.reshape(n_e, ROW_SUB, ROW_LANE), v.reshape(n_e, ROW_SUB, ROW_LANE)], axis=1)
    return rows.reshape(2 * n_e, ROW_SUB, ROW_LANE)


def _peer_experts(e, g, xm, uv):
    t, d = xm.shape
    assert d == ROW_SUB * ROW_LANE and t % SC_TOKEN_STEP == 0 and t >= 2 * SC_TOKEN_STEP and e.shape == (t, PEER_SEL)
    n_batches = t // SC_TOKB

    def tiled(a):
        return a.reshape(n_batches, SC_TOKB, ROW_SUB, ROW_LANE).transpose(0, 2, 1, 3)

    call = pl.kernel(
        functools.partial(_peer_sc_body, batches_per_tile=n_batches // SC_TILES),
        out_type=jax.ShapeDtypeStruct((n_batches, ROW_SUB, SC_TOKB, ROW_LANE), F32),
        mesh=plsc.VectorSubcoreMesh(core_axis_name="c", subcore_axis_name="s"),
        scratch_types=[pltpu.VMEM((SC_TOKB, PEER_SEL), jnp.int32),
                       pltpu.VMEM((SC_TOKB, PEER_SEL), jnp.int32),
                       pltpu.VMEM((SC_TOKB, PEER_SEL), F32),
                       pltpu.VMEM((ROW_SUB, SC_TOKB, ROW_LANE), F32),
                       pltpu.VMEM((2, ROW_SUB, SC_TOKB, ROW_LANE), F32),
                       pltpu.VMEM((SC_LANES, SC_LANES), F32),
                       pltpu.VMEM((SC_LANES,), F32),
                       pltpu.VMEM((SC_RING, SC_LANES, ROW_SUB, ROW_LANE), F32),
                       pltpu.VMEM((SC_RING, SC_LANES, ROW_SUB, ROW_LANE), F32),
                       pltpu.SemaphoreType.DMA((SC_RING,)),
                       pltpu.SemaphoreType.DMA((SC_RING,)),
                       pltpu.SemaphoreType.DMA((2,))],
        compiler_params=pltpu.CompilerParams(needs_layout_passes=False),
        name="peer_experts_sc",
    )
    out = call(e, g, tiled(xm), uv)
    return out.transpose(0, 2, 1, 3).reshape(t, d)


PEER_TB = 8


def _peer_tc_body(e_cur, e_nxt, g_ref, x_ref, uv_hbm, o_ref, buf, sem, *, n_steps):
    i = pl.program_id(0)
    n_rows = PEER_TB * PEER_SEL
    slot = i % 2

    def issue(e_ref, dst):
        def one(r, carry):
            row = e_ref[r // PEER_SEL, r % PEER_SEL]
            pltpu.make_async_copy(uv_hbm.at[pl.ds(row, 2)], buf.at[dst, :, :, r], sem.at[dst]).start()
            return carry
        lax.fori_loop(0, n_rows, one, 0, unroll=8)

    @pl.when(i == 0)
    def _():
        issue(e_cur, slot)

    @pl.when(i + 1 < n_steps)
    def _():
        issue(e_nxt, 1 - slot)

    pltpu.make_async_copy(buf.at[slot], buf.at[slot], sem.at[slot]).wait()

    for t in range(PEER_TB):
        rows = slice(t * PEER_SEL, (t + 1) * PEER_SEL)
        acc = buf[slot, 0, 0, rows, :] * x_ref[t:t + 1, 0:ROW_LANE]
        for j in range(1, ROW_SUB):
            acc = acc + buf[slot, 0, j, rows, :] * x_ref[t:t + 1, j * ROW_LANE:(j + 1) * ROW_LANE]
        w = g_ref[t] * _gelu(jnp.sum(acc, axis=-1, keepdims=True))
        for j in range(ROW_SUB):
            o_ref[t:t + 1, j * ROW_LANE:(j + 1) * ROW_LANE] = jnp.sum(w * buf[slot, 1, j, rows, :], axis=0,
                                                                       keepdims=True)


def _peer_experts_tc(e, g, xm, uv):
    t, d = xm.shape
    assert d == ROW_SUB * ROW_LANE and t % PEER_TB == 0
    n_steps = t // PEER_TB
    last = n_steps - 1
    n_rows = PEER_TB * PEER_SEL
    return pl.pallas_call(
        functools.partial(_peer_tc_body, n_steps=n_steps),
        grid=(n_steps,),
        in_specs=[pl.BlockSpec((PEER_TB, PEER_SEL), lambda i: (i, 0), memory_space=pltpu.SMEM),
                  pl.BlockSpec((PEER_TB, PEER_SEL), lambda i: (jnp.minimum(i + 1, last), 0), memory_space=pltpu.SMEM),
                  pl.BlockSpec((PEER_TB, PEER_SEL, 1), lambda i: (i, 0, 0)),
                  pl.BlockSpec((PEER_TB, d), lambda i: (i, 0)),
                  pl.BlockSpec(memory_space=pl.ANY)],
        out_specs=pl.BlockSpec((PEER_TB, d), lambda i: (i, 0)),
        out_shape=jax.ShapeDtypeStruct((t, d), F32),
        scratch_shapes=[pltpu.VMEM((2, 2, ROW_SUB, n_rows, ROW_LANE), F32),
                        pltpu.SemaphoreType.DMA((2,))],
        compiler_params=_cparams("arbitrary"),
        name="peer_experts_tc",
    )(e, e, g.reshape(t, PEER_SEL, 1), xm, uv)


def _peer_select(q, sub_keys):
    return tuple(_peer_topk(q, sub_keys))


def _final_body(h_ref, delta_ref, gate_ref, g_ref, o_ref):
    h = h_ref[...] + gate_ref[0] * delta_ref[...]
    o_ref[...] = _rms(h) * g_ref[...]


def _final(h, delta, gate, gain, *, rows_per_group, tm=512):
    t, d = h.shape
    row = pl.BlockSpec((tm, d), lambda i: (i, 0))
    return pl.pallas_call(
        _final_body,
        grid=(t // tm,),
        in_specs=[row, row, pl.BlockSpec((1, 1, d), lambda i: ((i * tm) // rows_per_group, 0, 0)),
                  pl.BlockSpec((1, d), lambda i: (0, 0))],
        out_specs=row,
        out_shape=jax.ShapeDtypeStruct((t, d), F32),
        compiler_params=_cparams("parallel"),
        name="final_norm",
    )(h, delta, gate, gain.reshape(1, d))


def kernel(x, c, ctx, c_ctx, w_mod, b_mod, attn_w_in, attn_w_out, lambda_q1, lambda_k1, lambda_q2, lambda_k2,
           subln_g, na_rpb, chunk_w_in, chunk_b_in, chunk_ln_g, chunk_ln_b, chunk_w_s, chunk_b_s, chunk_w_out,
           peer_w_query, peer_sub_keys, peer_u, peer_v, final_norm_g):
    b, n, d = x.shape
    n_ctx = ctx.shape[1]

    cond = jnp.concatenate([c, c_ctx[None], jnp.zeros((-(b + 1) % 8, d), F32)], axis=0)
    mods = [_adaln(cond, w_mod[i], b_mod[i]) for i in range(2)]
    m0, m1 = mods

    w_in = attn_w_in[0].astype(BF16)
    w_out = attn_w_out[0].astype(BF16)
    w_q0 = peer_w_query[0].astype(BF16)
    w_q1 = peer_w_query[1].astype(BF16)
    w_gin = chunk_w_in[0].astype(BF16)
    w_gout = chunk_w_out[0].astype(BF16)
    rope = _rope_tables(n, 512)
    bias_tabs = _na_bias_tables(na_rpb[0], n // GRID_W)
    lam_init = 0.8 - 0.6 * math.exp(-0.3 * 0)
    lam = (jnp.exp(jnp.sum(lambda_q1[0] * lambda_k1[0])) - jnp.exp(jnp.sum(lambda_q2[0] * lambda_k2[0]))
           + lam_init).astype(F32)
    ka = A_HEADS * 2 * A_QK_DIM

    def cx(m, k):
        return m[b:b + 1, k * d:(k + 1) * d].reshape(1, 1, d)

    sizes = PIPE_BATCHES if sum(PIPE_BATCHES) == b else (b,)
    starts = [sum(sizes[:ci]) for ci in range(len(sizes))]
    n_chunks = len(sizes)

    def rows(a, ci):
        return a[starts[ci]:starts[ci] + sizes[ci]]

    def lat(ci, m, k):
        return rows(m, ci)[:, k * d:(k + 1) * d].reshape(sizes[ci], 1, d)

    def stage_a(ci, x2):
        bc = sizes[ci]
        zc = _mod_mm(rows(ctx, ci).reshape(bc * n_ctx, d), cx(m0, 0), cx(m0, 1), w_in, rows_per_group=n_ctx,
                     tm=256, name="attn_in_ctx")[0]
        z = _mod_mm(x2, lat(ci, m0, 0), lat(ci, m0, 1), w_in, rows_per_group=n, tm=256, rope=rope,
                    rope_chunks=(0, 1), scale_chunks=(0, 3), name="attn_in")[0]
        z3 = z.reshape(bc, n, -1)
        zc3 = zc.reshape(bc, n_ctx, -1)
        oa = _diff_attn(z3, zc3, lam, subln_g[0], lam_init)
        ob = _na_attn(z3, zc3, bias_tabs)
        h1 = _out_proj(x2, lat(ci, m0, 2), oa.reshape(bc * n, ka), ob.reshape(bc * n, -1), w_out[:ka], w_out[ka:],
                       rows_per_group=n)
        q0, xm0 = _mod_mm(h1, lat(ci, m0, 3), lat(ci, m0, 4), w_q0, rows_per_group=n, tm=256, emit_xm=True,
                          name="peer_q0")
        return h1, xm0, _peer_select(q0, peer_sub_keys[0])

    def stage_b(ci, h1, p0):
        zg, h1b = _mod_mm(h1, lat(ci, m1, 0), lat(ci, m1, 1), w_gin, rows_per_group=n, tm=256, delta=p0,
                          gate=lat(ci, m0, 5), bias=chunk_b_in[0], act="gelu", name="gmlp_in")
        h2 = _gmlp_tail(zg, h1b, lat(ci, m1, 2), chunk_ln_g[0], chunk_ln_b[0], chunk_w_s[0], chunk_b_s[0], w_gout,
                        rows_per_group=n)
        q1, xm1 = _mod_mm(h2, lat(ci, m1, 3), lat(ci, m1, 4), w_q1, rows_per_group=n, tm=256, emit_xm=True,
                          name="peer_q1")
        return h2, xm1, _peer_select(q1, peer_sub_keys[1])

    tables = [_expert_rows(peer_u[i], peer_v[i]) for i in range(2)]

    def experts(sel, xm, tabs):
        e, g = sel
        t = xm.shape[0]
        k = int(t * PEER_TC_SHARE) // SC_TOKEN_STEP * SC_TOKEN_STEP
        if k == 0 or (t - k) % SC_TOKEN_STEP:
            return _peer_experts(e, g, xm, tabs)
        p_sc = _peer_experts(e[k:], g[k:], xm[k:], tabs)
        p_tc = _peer_experts_tc(e[:k], g[:k], xm[:k], tabs)
        return jnp.concatenate([p_tc, p_sc], axis=0)

    h1s, p0s = [], []
    prev_sel = None
    for ci in range(n_chunks):
        x2 = rows(x, ci).reshape(sizes[ci] * n, d)
        if prev_sel is not None:
            x2, prev_sel, tables = lax.optimization_barrier((x2, prev_sel, tables))
            p0s.append(experts(prev_sel, xm_prev, tables[0]))
        h1, xm_prev, prev_sel = stage_a(ci, x2)
        h1s.append(h1)
    p0s.append(experts(prev_sel, xm_prev, tables[0]))

    h2s, p1s = [], []
    prev_sel = None
    for ci in range(n_chunks):
        p0 = p0s[ci]
        if prev_sel is not None:
            p0, prev_sel = lax.optimization_barrier((p0, prev_sel))
            p1s.append(experts(prev_sel, xm_prev, tables[1]))
        h2, xm_prev, prev_sel = stage_b(ci, h1s[ci], p0)
        h2s.append(h2)
    p1s.append(experts(prev_sel, xm_prev, tables[1]))

    outs = [_final(h2s[ci], p1s[ci], lat(ci, m1, 5), final_norm_g, rows_per_group=n).reshape(sizes[ci], n, d)
            for ci in range(n_chunks)]
    return outs[0] if n_chunks == 1 else jnp.concatenate(outs, axis=0)
```

```python
import functools
import math

import numpy as np
import jax
import jax.numpy as jnp
from jax import lax
from jax.experimental import pallas as pl
from jax.experimental.pallas import tpu as pltpu
from jax.experimental.pallas import tpu_sc as plsc

F32 = jnp.float32
BF16 = jnp.bfloat16

EPS = 1e-6
GRID_W = 64
NA_ROWS = 8
NA_COLS = 16
ROPE_BASE = 10000.0
A_HEADS = 4
A_QK_DIM = 64
B_HEADS = 8
B_HEAD_DIM = 64
PEER_HEADS = 8
N_KEYS = 128
PEER_TOPK = 16
PEER_SEL = PEER_HEADS * PEER_TOPK
CHUNK = 128
C_GROUPS = 8
MASK_VALUE = -1e30

VMEM_LIMIT = 56 * 1024 * 1024
PIPE_BATCHES = (1, 2, 2, 3)
PEER_TC_SHARE = 3 / 8


def _cparams(*sem):
    return pltpu.CompilerParams(dimension_semantics=sem, vmem_limit_bytes=VMEM_LIMIT)


def _rms(x):
    return x * lax.rsqrt(jnp.mean(x * x, axis=-1, keepdims=True) + EPS)


def _gelu(x):
    return jax.nn.gelu(x)


def _adaln_body(c_ref, w_ref, b_ref, o_ref):
    cnd = c_ref[...]
    a = (cnd * jax.nn.sigmoid(cnd)).astype(BF16)
    o_ref[...] = jnp.dot(a, w_ref[...].astype(BF16), preferred_element_type=F32) + b_ref[...]


def _adaln(cond, w, b):
    m, d = cond.shape
    n = w.shape[1]
    tn = 1024
    return pl.pallas_call(
        _adaln_body,
        grid=(n // tn,),
        in_specs=[pl.BlockSpec((m, d), lambda j: (0, 0)),
                  pl.BlockSpec((d, tn), lambda j: (0, j)),
                  pl.BlockSpec((1, tn), lambda j: (0, j))],
        out_specs=pl.BlockSpec((m, tn), lambda j: (0, j)),
        out_shape=jax.ShapeDtypeStruct((m, n), F32),
        compiler_params=_cparams("parallel"),
        name="adaln",
    )(cond, w, b.reshape(1, n))


def _mod_mm_body(*refs, has_delta, has_bias, rope_chunks, scale_chunks, act, emit_xm, tn, n_out):
    it = iter(refs)
    h_ref = next(it)
    delta_ref = gate_ref = None
    if has_delta:
        delta_ref, gate_ref = next(it), next(it)
    shift_ref, scale_ref, w_ref = next(it), next(it), next(it)
    b_ref = next(it) if has_bias else None
    cos_ref = sin_ref = None
    if rope_chunks:
        cos_ref, sin_ref = next(it), next(it)
    y_ref = next(it)
    hn_ref = next(it) if has_delta else None
    xm_ref = next(it) if emit_xm else None

    h = h_ref[...]
    if has_delta:
        h = h + gate_ref[0] * delta_ref[...]
        hn_ref[...] = h
    xm = _rms(h) * (1.0 + scale_ref[0]) + shift_ref[0]
    if emit_xm:
        xm_ref[...] = xm
    xb = xm.astype(BF16)
    for c in range(n_out // tn):
        cols = slice(c * tn, (c + 1) * tn)
        y = jnp.dot(xb, w_ref[:, cols], preferred_element_type=F32)
        if has_bias:
            y = y + b_ref[:, cols]
        if c in rope_chunks:
            lane = lax.broadcasted_iota(jnp.int32, y.shape, 1)
            partner = jnp.where((lane & 16) == 0, pltpu.roll(y, tn - 16, 1), pltpu.roll(y, 16, 1))
            y = y * cos_ref[...] + partner * sin_ref[...]
        if c in scale_chunks:
            y = y * 0.125
        if act == "gelu":
            y = _gelu(y)
        y_ref[:, cols] = y


def _mod_mm(h, shift, scale, w, *, rows_per_group, tm, delta=None, gate=None, bias=None, rope=None,
            rope_chunks=(), scale_chunks=(), act="none", emit_xm=False, name="mod_mm"):
    t, d = h.shape
    n_out = w.shape[1]
    tn = 512
    groups = shift.shape[0]
    assert t % tm == 0 and rows_per_group % tm == 0 and n_out % tn == 0

    def gidx(i):
        return ((i * tm) // rows_per_group if groups > 1 else 0, 0, 0)

    row_spec = pl.BlockSpec((tm, d), lambda i: (i, 0))
    mod_spec = pl.BlockSpec((1, 1, d), gidx)
    args, specs = [h], [row_spec]
    if delta is not None:
        args += [delta, gate]
        specs += [row_spec, mod_spec]
    args += [shift, scale, w]
    specs += [mod_spec, mod_spec, pl.BlockSpec((d, n_out), lambda i: (0, 0))]
    if bias is not None:
        args.append(bias.reshape(1, n_out))
        specs.append(pl.BlockSpec((1, n_out), lambda i: (0, 0)))
    if rope_chunks:
        cos, sin = rope
        pos_blocks = cos.shape[0] // tm
        args += [cos, sin]
        specs += [pl.BlockSpec((tm, tn), lambda i: (i % pos_blocks, 0))] * 2
    out_shape = [jax.ShapeDtypeStruct((t, n_out), F32)]
    out_specs = [pl.BlockSpec((tm, n_out), lambda i: (i, 0))]
    if delta is not None:
        out_shape.append(jax.ShapeDtypeStruct((t, d), F32))
        out_specs.append(row_spec)
    if emit_xm:
        out_shape.append(jax.ShapeDtypeStruct((t, d), F32))
        out_specs.append(row_spec)
    body = functools.partial(_mod_mm_body, has_delta=delta is not None, has_bias=bias is not None,
                             rope_chunks=tuple(rope_chunks), scale_chunks=tuple(scale_chunks), act=act,
                             emit_xm=emit_xm, tn=tn, n_out=n_out)
    return pl.pallas_call(body, grid=(t // tm,), in_specs=specs, out_specs=out_specs, out_shape=out_shape,
                          compiler_params=_cparams("parallel"), name=name)(*args)


def _rope_tables(n_tokens, width):
    axis_dim = A_QK_DIM // 2
    inv_freq = 1.0 / (ROPE_BASE ** (jnp.arange(0, axis_dim, 2, dtype=F32) / axis_dim))
    t = jnp.arange(n_tokens)
    row = (t // GRID_W).astype(F32)
    col = (t % GRID_W).astype(F32)
    ang_r = row[:, None] * inv_freq
    ang_c = col[:, None] * inv_freq
    ang = jnp.concatenate([ang_r, ang_r, ang_c, ang_c], axis=1)
    sign = jnp.tile(jnp.concatenate([-jnp.ones(16, F32), jnp.ones(16, F32)]), 2)
    reps = width // A_QK_DIM
    return jnp.tile(jnp.cos(ang), (1, reps)), jnp.tile(jnp.sin(ang) * sign, (1, reps))


def _diff_attn_body(lam_ref, q_ref, k_ref, v_ref, kc_ref, vc_ref, g_ref, o_ref, *, out_scale):
    lam = lam_ref[0, 0]
    q = q_ref[0]
    k = k_ref[0].astype(BF16)
    kc = kc_ref[0].astype(BF16)
    lane = lax.broadcasted_iota(jnp.int32, q.shape, 1)
    nt = (((1,), (1,)), ((), ()))
    parts = []
    for m in range(2):
        keep = (lane < A_QK_DIM) if m == 0 else (lane >= A_QK_DIM)
        qm = jnp.where(keep, q, 0.0).astype(BF16)
        s_l = lax.dot_general(qm, k, nt, preferred_element_type=F32)
        s_c = lax.dot_general(qm, kc, nt, preferred_element_type=F32)
        mx = jnp.maximum(jnp.max(s_l, axis=-1, keepdims=True), jnp.max(s_c, axis=-1, keepdims=True))
        e_l = jnp.exp(s_l - mx)
        e_c = jnp.exp(s_c - mx)
        inv = 1.0 / (jnp.sum(e_l, axis=-1, keepdims=True) + jnp.sum(e_c, axis=-1, keepdims=True))
        parts.append((e_l, e_c, inv))
    (e1l, e1c, inv1), (e2l, e2c, inv2) = parts
    w2 = lam * inv2
    a_l = (e1l * inv1 - e2l * w2).astype(BF16)
    a_c = (e1c * inv1 - e2c * w2).astype(BF16)
    o = (jnp.dot(a_l, v_ref[0].astype(BF16), preferred_element_type=F32)
         + jnp.dot(a_c, vc_ref[0].astype(BF16), preferred_element_type=F32))
    o_ref[0] = _rms(o) * g_ref[...] * out_scale


def _diff_attn(z, zc, lam, subln_g, lam_init, *, tq=256):
    b, n, _ = z.shape
    c = zc.shape[1]
    kcol, vcol = A_HEADS, 2 * A_HEADS
    return pl.pallas_call(
        functools.partial(_diff_attn_body, out_scale=1.0 - lam_init),
        grid=(b, A_HEADS, n // tq),
        in_specs=[pl.BlockSpec((1, 1), lambda bi, h, i: (0, 0), memory_space=pltpu.SMEM),
                  pl.BlockSpec((1, tq, 128), lambda bi, h, i: (bi, i, h)),
                  pl.BlockSpec((1, n, 128), lambda bi, h, i: (bi, 0, kcol + h)),
                  pl.BlockSpec((1, n, 128), lambda bi, h, i: (bi, 0, vcol + h)),
                  pl.BlockSpec((1, c, 128), lambda bi, h, i: (bi, 0, kcol + h)),
                  pl.BlockSpec((1, c, 128), lambda bi, h, i: (bi, 0, vcol + h)),
                  pl.BlockSpec((1, 128), lambda bi, h, i: (0, 0))],
        out_specs=pl.BlockSpec((1, tq, 128), lambda bi, h, i: (bi, i, h)),
        out_shape=jax.ShapeDtypeStruct((b, n, A_HEADS * 128), F32),
        compiler_params=_cparams("parallel", "parallel", "arbitrary"),
        name="diff_attn",
    )(lam.reshape(1, 1), z, z, z, zc, zc, subln_g.reshape(1, 128))


NA_QROWS = 8
NA_KROWS = 16
NA_SEG = 4 * GRID_W


def _na_window_start(j, rows):
    return jnp.clip(NA_QROWS * j - NA_ROWS // 2, 0, rows - NA_KROWS)


def _na_bias_tables(rpb, rows):
    n_blocks = rows // NA_QROWS
    h = rpb.shape[0]
    ic = np.clip(np.arange(GRID_W)[None, :] - np.arange(GRID_W)[:, None] + (NA_COLS - 1), 0, 2 * NA_COLS - 2)
    toep = jnp.pad(rpb[:, :, ic], ((0, 0), (NA_KROWS, NA_KROWS), (0, 0), (0, 0)))
    tabs, oks = [], []
    for j in (0, 1, n_blocks - 1):
        ks = int(np.clip(NA_QROWS * j - NA_ROWS // 2, 0, rows - NA_KROWS))
        slabs = []
        for rq in range(NA_QROWS):
            dr0 = ks - (NA_QROWS * j + rq) + (NA_ROWS - 1) + NA_KROWS
            slab = toep[:, dr0:dr0 + NA_KROWS]
            slabs.append(slab.transpose(0, 2, 1, 3).reshape(h, GRID_W, NA_KROWS * GRID_W))
        tabs.append(jnp.concatenate(slabs, axis=1))
        r = (NA_QROWS * j + np.arange(NA_QROWS))[:, None, None, None]
        cq = np.arange(GRID_W)[None, :, None, None]
        kr = (ks + np.arange(NA_KROWS))[None, None, :, None]
        kc = np.arange(GRID_W)[None, None, None, :]
        r0 = np.clip(r - NA_ROWS // 2, 0, rows - NA_ROWS)
        c0 = np.clip(cq - NA_COLS // 2, 0, GRID_W - NA_COLS)
        ok = (kr >= r0) & (kr < r0 + NA_ROWS) & (kc >= c0) & (kc < c0 + NA_COLS)
        oks.append(ok.reshape(NA_QROWS * GRID_W, NA_KROWS * GRID_W))
    return jnp.where(np.stack(oks)[None], jnp.stack(tabs, axis=1), MASK_VALUE)


def _na_body(q_ref, k0, k1, k2, k3, v0, v1, v2, v3, kc_ref, vc_ref, bias_ref, o_ref):
    q = q_ref[0]
    lane = lax.broadcasted_iota(jnp.int32, q.shape, 1)
    nt = (((1,), (1,)), ((), ()))
    ks = [r[0].astype(BF16) for r in (k0, k1, k2, k3)]
    vs = [r[0].astype(BF16) for r in (v0, v1, v2, v3)]
    kc = kc_ref[0].astype(BF16)
    vc = vc_ref[0].astype(BF16)
    outs = []
    for hh in range(2):
        keep = (lane < B_HEAD_DIM) if hh == 0 else (lane >= B_HEAD_DIM)
        qh = jnp.where(keep, q, 0.0).astype(BF16)
        segs = [lax.dot_general(qh, ks[i], nt, preferred_element_type=F32)
                + bias_ref[hh, 0, :, i * NA_SEG:(i + 1) * NA_SEG] for i in range(4)]
        s_c = lax.dot_general(qh, kc, nt, preferred_element_type=F32)
        mx = jnp.max(s_c, axis=-1, keepdims=True)
        for s in segs:
            mx = jnp.maximum(mx, jnp.max(s, axis=-1, keepdims=True))
        e_c = jnp.exp(s_c - mx)
        den = jnp.sum(e_c, axis=-1, keepdims=True)
        es = []
        for s in segs:
            e = jnp.exp(s - mx)
            den = den + jnp.sum(e, axis=-1, keepdims=True)
            es.append(e)
        inv = 1.0 / den
        o = jnp.dot((e_c * inv).astype(BF16), vc, preferred_element_type=F32)
        for e, v in zip(es, vs):
            o = o + jnp.dot((e * inv).astype(BF16), v, preferred_element_type=F32)
        outs.append(o)
    o_ref[0] = jnp.where(lane < B_HEAD_DIM, outs[0], outs[1])


def _na_attn(z, zc, bias_tabs):
    b, n, _ = z.shape
    c = zc.shape[1]
    rows = n // GRID_W
    n_blocks = rows // NA_QROWS
    tq = NA_QROWS * GRID_W
    qcol, kcol, vcol = 12, 16, 20
    seg_rows = NA_SEG // GRID_W

    def kv_spec(col0, i):
        return pl.BlockSpec((1, NA_SEG, 128),
                            lambda hp, j, bi: (bi, _na_window_start(j, rows) // seg_rows + i, col0 + hp))

    def variant(j):
        return jnp.where(j == 0, 0, jnp.where(j == n_blocks - 1, 2, 1))

    in_specs = [pl.BlockSpec((1, tq, 128), lambda hp, j, bi: (bi, j, qcol + hp))]
    in_specs += [kv_spec(kcol, i) for i in range(4)]
    in_specs += [kv_spec(vcol, i) for i in range(4)]
    in_specs += [pl.BlockSpec((1, c, 128), lambda hp, j, bi: (bi, 0, kcol + hp)),
                 pl.BlockSpec((1, c, 128), lambda hp, j, bi: (bi, 0, vcol + hp)),
                 pl.BlockSpec((2, 1, tq, NA_KROWS * GRID_W), lambda hp, j, bi: (hp, variant(j), 0, 0))]
    return pl.pallas_call(
        _na_body,
        grid=(B_HEADS // 2, n_blocks, b),
        in_specs=in_specs,
        out_specs=pl.BlockSpec((1, tq, 128), lambda hp, j, bi: (bi, j, hp)),
        out_shape=jax.ShapeDtypeStruct((b, n, B_HEADS * B_HEAD_DIM), F32),
        compiler_params=_cparams("parallel", "parallel", "arbitrary"),
        name="na_attn",
    )(z, z, z, z, z, z, z, z, z, zc, zc, bias_tabs)


def _out_proj_body(h_ref, gate_ref, xa_ref, xb_ref, wa_ref, wb_ref, o_ref):
    y = (jnp.dot(xa_ref[...].astype(BF16), wa_ref[...], preferred_element_type=F32)
         + jnp.dot(xb_ref[...].astype(BF16), wb_ref[...], preferred_element_type=F32))
    o_ref[...] = h_ref[...] + gate_ref[0] * y


def _out_proj(h, gate, xa, xb, wa, wb, *, rows_per_group, tm=512):
    t, d = h.shape
    ka, kb = xa.shape[1], xb.shape[1]
    return pl.pallas_call(
        _out_proj_body,
        grid=(t // tm,),
        in_specs=[pl.BlockSpec((tm, d), lambda i: (i, 0)),
                  pl.BlockSpec((1, 1, d), lambda i: ((i * tm) // rows_per_group, 0, 0)),
                  pl.BlockSpec((tm, ka), lambda i: (i, 0)),
                  pl.BlockSpec((tm, kb), lambda i: (i, 0)),
                  pl.BlockSpec((ka, d), lambda i: (0, 0)),
                  pl.BlockSpec((kb, d), lambda i: (0, 0))],
        out_specs=pl.BlockSpec((tm, d), lambda i: (i, 0)),
        out_shape=jax.ShapeDtypeStruct((t, d), F32),
        compiler_params=_cparams("parallel"),
        name="out_proj",
    )(h, gate, xa, xb, wa, wb)


def _gmlp_tail_body(z_ref, h_ref, gate_ref, lng_ref, lnb_ref, ws_ref, bs_ref, wo_ref, o_ref, gated_ref, *, tm, width):
    gw = width // C_GROUPS
    for ci in range(tm // CHUNK):
        rows = slice(ci * CHUNK, (ci + 1) * CHUNK)
        v = z_ref[rows, width:]
        mu = jnp.mean(v, axis=-1, keepdims=True)
        var = jnp.mean(jnp.square(v - mu), axis=-1, keepdims=True)
        vn = ((v - mu) * lax.rsqrt(var + EPS)) * lng_ref[...] + lnb_ref[...]
        vb = vn.astype(BF16)
        for g in range(C_GROUPS):
            cols = slice(g * gw, (g + 1) * gw)
            sv = jnp.dot(ws_ref[g], vb[:, cols], preferred_element_type=F32) + bs_ref[g]
            gated_ref[rows, cols] = (z_ref[rows, cols] * sv).astype(BF16)
    y = jnp.dot(gated_ref[...], wo_ref[...], preferred_element_type=F32)
    o_ref[...] = h_ref[...] + gate_ref[0] * y


def _gmlp_tail(z, h, gate, ln_g, ln_b, w_s, b_s, w_out, *, rows_per_group, tm=512):
    t, d = h.shape
    width = z.shape[1] // 2
    gw = width // C_GROUPS
    bsb = jnp.broadcast_to(b_s[:, :, None], (C_GROUPS, CHUNK, gw))
    return pl.pallas_call(
        functools.partial(_gmlp_tail_body, tm=tm, width=width),
        grid=(t // tm,),
        in_specs=[pl.BlockSpec((tm, 2 * width), lambda i: (i, 0)),
                  pl.BlockSpec((tm, d), lambda i: (i, 0)),
                  pl.BlockSpec((1, 1, d), lambda i: ((i * tm) // rows_per_group, 0, 0)),
                  pl.BlockSpec((1, width), lambda i: (0, 0)),
                  pl.BlockSpec((1, width), lambda i: (0, 0)),
                  pl.BlockSpec((C_GROUPS, CHUNK, CHUNK), lambda i: (0, 0, 0)),
                  pl.BlockSpec((C_GROUPS, CHUNK, gw), lambda i: (0, 0, 0)),
                  pl.BlockSpec((width, d), lambda i: (0, 0))],
        out_specs=pl.BlockSpec((tm, d), lambda i: (i, 0)),
        out_shape=jax.ShapeDtypeStruct((t, d), F32),
        scratch_shapes=[pltpu.VMEM((tm, width), BF16)],
        compiler_params=_cparams("parallel"),
        name="gmlp_tail",
    )(z, h, gate, ln_g.reshape(1, width), ln_b.reshape(1, width), w_s.astype(BF16), bsb, w_out)


def _topk_rows(s, payload, k):
    n_rows = s.shape[0]
    riota = lax.broadcasted_iota(jnp.int32, s.shape, 0)
    vals, pays = [], []
    for _ in range(k):
        m = jnp.max(s, axis=0, keepdims=True)
        first = jnp.min(jnp.where(s == m, riota, n_rows), axis=0, keepdims=True)
        sel = riota == first
        vals.append(m)
        pays.append(first if payload is None else jnp.max(jnp.where(sel, payload, -1), axis=0, keepdims=True))
        s = jnp.where(sel, -jnp.inf, s)
    return jnp.concatenate(vals, axis=0), jnp.concatenate(pays, axis=0)


def _pair_candidates(x1, x2, combine, fill):
    k = PEER_TOPK
    sub = lax.broadcasted_iota(jnp.int32, (8, x1.shape[1]), 0)
    blocks = [combine(x1[0:1], x2)]
    for a in range(1, 8):
        blocks.append(jnp.where(sub < k // (a + 1), combine(x1[a:a + 1], x2[0:8]), fill))
    blocks.append(combine(x1[8:k], x2[0:1]))
    return jnp.concatenate(blocks, axis=0)


def _peer_topk_body(q_ref, keys_ref, e_ref, g_ref, *, tt):
    nt = (((1,), (1,)), ((), ()))
    es, gs = [], []
    for h in range(PEER_HEADS):
        halves = []
        for p in range(2):
            hp = 2 * h + p
            qhp = q_ref[:, hp * 128:(hp + 1) * 128].astype(BF16)
            s = lax.dot_general(keys_ref[hp], qhp, nt, preferred_element_type=F32)
            halves.append(_topk_rows(s, None, PEER_TOPK))
        (s1, i1), (s2, i2) = halves
        cand = _pair_candidates(s1, s2, lambda x, y: x + y, -jnp.inf)
        expert = _pair_candidates(i1, i2, lambda x, y: 2 * (x * N_KEYS + y), -1)
        sc, e = _topk_rows(cand, expert, PEER_TOPK)
        ex = jnp.exp(sc - sc[0:1])
        es.append(e)
        gs.append(ex / jnp.sum(ex, axis=0, keepdims=True))
    e_ref[...] = jnp.concatenate(es, axis=0).T
    g_ref[...] = jnp.concatenate(gs, axis=0).T


def _peer_topk(q, sub_keys, *, tt=128):
    t, qw = q.shape
    keys = sub_keys.reshape(2 * PEER_HEADS, N_KEYS, qw // (2 * PEER_HEADS)).astype(BF16)
    return pl.pallas_call(
        functools.partial(_peer_topk_body, tt=tt),
        grid=(t // tt,),
        in_specs=[pl.BlockSpec((tt, qw), lambda i: (i, 0)),
                  pl.BlockSpec(keys.shape, lambda i: (0, 0, 0))],
        out_specs=[pl.BlockSpec((tt, PEER_SEL), lambda i: (i, 0)),
                   pl.BlockSpec((tt, PEER_SEL), lambda i: (i, 0))],
        out_shape=[jax.ShapeDtypeStruct((t, PEER_SEL), jnp.int32),
                   jax.ShapeDtypeStruct((t, PEER_SEL), F32)],
        compiler_params=_cparams("parallel"),
        name="peer_topk",
    )(q, keys)


SC_CORES = 2
SC_SUBCORES = 16
SC_LANES = 16
SC_TILES = SC_CORES * SC_SUBCORES
ROW_SUB = 8
ROW_LANE = 128
VECS_PER_SUB = ROW_LANE // SC_LANES
D_BLOCKS = 4
VECS_PER_BLOCK = ROW_SUB * VECS_PER_SUB // D_BLOCKS


def _sc_gelu(s):
    z = 0.7978845608028654 * (s + 0.044715 * (s * s * s))
    tanh_z = 1.0 - 2.0 / (jnp.exp(2.0 * z) + 1.0)
    return 0.5 * s * (1.0 + tanh_z)


def _vec_slot(jb, k):
    v = jb * VECS_PER_BLOCK + k
    return v // VECS_PER_SUB, pl.ds((v % VECS_PER_SUB) * SC_LANES, SC_LANES)


SC_TOKB = 8
PART_SUMS = 4
SC_RING = 3
SC_TOKEN_STEP = SC_TILES * SC_TOKB


def _peer_sc_body(e_hbm, g_hbm, x_hbm, uv_hbm, o_hbm, idx_v, idx1_v, g_v, x_v, out_v, acc_v, w_v, ubuf, vbuf,
                  sem_u, sem_v, sem_o, *, batches_per_tile):
    wid = lax.axis_index("s") * SC_CORES + lax.axis_index("c")
    base = wid * batches_per_tile
    lane = lax.iota(jnp.int32, SC_LANES)
    n_groups = PEER_SEL // SC_LANES
    gpb = SC_TOKB * n_groups

    def sel16(ref, gi):
        return ref.at[gi // n_groups, pl.ds((gi % n_groups) * SC_LANES, SC_LANES)]

    def gather(gi, slot):
        return (pltpu.make_async_copy(uv_hbm.at[sel16(idx_v, gi)], ubuf.at[slot], sem_u.at[slot]),
                pltpu.make_async_copy(uv_hbm.at[sel16(idx1_v, gi)], vbuf.at[slot], sem_v.at[slot]))

    def start(gi, slot):
        for cp in gather(gi, slot):
            cp.start()

    def out_copy(batch, par):
        return pltpu.make_async_copy(out_v.at[par], o_hbm.at[batch], sem_o.at[par])

    def u_phase(slot, t, gi):
        for jb in range(D_BLOCKS):
            xs = []
            for k in range(VECS_PER_BLOCK):
                sub, ls = _vec_slot(jb, k)
                xs.append(x_v[sub, t, ls])

            @plsc.parallel_loop(0, SC_LANES, unroll=2)
            def _(r, jb=jb, xs=xs):
                parts = []
                per = VECS_PER_BLOCK // PART_SUMS
                for q in range(PART_SUMS):
                    p = xs[q * per] * ubuf[(slot, r) + _vec_slot(jb, q * per)]
                    for k in range(q * per + 1, (q + 1) * per):
                        p = p + xs[k] * ubuf[(slot, r) + _vec_slot(jb, k)]
                    parts.append(p)
                a = (parts[0] + parts[1]) + (parts[2] + parts[3])
                if jb > 0:
                    a = a + acc_v[r, :]
                acc_v[r, :] = a
        cols = [plsc.load_gather(acc_v, [lane, jnp.full((SC_LANES,), l, jnp.int32)]) for l in range(SC_LANES)]
        while len(cols) > 1:
            cols = [cols[i] + cols[i + 1] for i in range(0, len(cols), 2)]
        w_v[...] = sel16(g_v, gi)[...] * _sc_gelu(cols[0])

    def v_phase(slot, par, t):
        for jb in range(D_BLOCKS):
            slots = [_vec_slot(jb, k) for k in range(VECS_PER_BLOCK)]
            init = tuple(out_v[par, sub, t, ls] for sub, ls in slots)

            def add_row(r, os, jb=jb):
                wb = plsc.load_gather(w_v, [jnp.full((SC_LANES,), r, jnp.int32)])
                return tuple(os[k] + wb * vbuf[(slot, r) + _vec_slot(jb, k)] for k in range(VECS_PER_BLOCK))

            os = plsc.parallel_loop(0, SC_LANES, unroll=2, carry=init)(add_row)
            for (sub, ls), o in zip(slots, os):
                out_v[par, sub, t, ls] = o

    @pl.loop(0, batches_per_tile)
    def _(bi):
        batch = base + bi
        par = bi % 2
        pltpu.sync_copy(e_hbm.at[pl.ds(batch * SC_TOKB, SC_TOKB)], idx_v)
        pltpu.sync_copy(g_hbm.at[pl.ds(batch * SC_TOKB, SC_TOKB)], g_v)
        pltpu.sync_copy(x_hbm.at[batch], x_v)
        for tt in range(SC_TOKB):
            for c in range(n_groups):
                cols = pl.ds(c * SC_LANES, SC_LANES)
                idx1_v[tt, cols] = idx_v[tt, cols] + 1
        for gi0 in range(SC_RING - 1):
            start(gi0, gi0)

        @pl.when(bi >= 2)
        def _():
            out_copy(batch, par).wait()

        def step(gi, slot):
            gi = jnp.asarray(gi, jnp.int32)
            t = gi // n_groups

            @pl.when(gi + SC_RING - 1 < gpb)
            def _():
                start(gi + SC_RING - 1, (slot + SC_RING - 1) % SC_RING)

            @pl.when(gi % n_groups == 0)
            def _():
                for j in range(ROW_SUB):
                    for l in range(VECS_PER_SUB):
                        out_v[par, j, t, pl.ds(l * SC_LANES, SC_LANES)] = jnp.zeros((SC_LANES,), F32)

            cu, cv = gather(gi, slot)
            cu.wait()
            u_phase(slot, t, gi)
            cv.wait()
            v_phase(slot, par, t)

        @pl.loop(0, gpb // SC_RING)
        def _(it):
            for slot in range(SC_RING):
                step(it * SC_RING + slot, slot)

        for gi in range(gpb - gpb % SC_RING, gpb):
            step(gi, gi % SC_RING)

        out_copy(batch, par).start()

    for par in range(2):
        out_copy(base, par).wait()


def _expert_rows(u, v):
    n_e = u.shape[0]
    rows = jnp.stack([u.reshape(n_e, ROW_SUB, ROW_LANE), v.reshape(n_e, ROW_SUB, ROW_LANE)], axis=1)
    return rows.reshape(2 * n_e, ROW_SUB, ROW_LANE)


def _peer_experts(e, g, xm, uv):
    t, d = xm.shape
    assert d == ROW_SUB * ROW_LANE and t % SC_TOKEN_STEP == 0 and t >= 2 * SC_TOKEN_STEP and e.shape == (t, PEER_SEL)
    n_batches = t // SC_TOKB

    def tiled(a):
        return a.reshape(n_batches, SC_TOKB, ROW_SUB, ROW_LANE).transpose(0, 2, 1, 3)

    call = pl.kernel(
        functools.partial(_peer_sc_body, batches_per_tile=n_batches // SC_TILES),
        out_type=jax.ShapeDtypeStruct((n_batches, ROW_SUB, SC_TOKB, ROW_LANE), F32),
        mesh=plsc.VectorSubcoreMesh(core_axis_name="c", subcore_axis_name="s"),
        scratch_types=[pltpu.VMEM((SC_TOKB, PEER_SEL), jnp.int32),
                       pltpu.VMEM((SC_TOKB, PEER_SEL), jnp.int32),
                       pltpu.VMEM((SC_TOKB, PEER_SEL), F32),
                       pltpu.VMEM((ROW_SUB, SC_TOKB, ROW_LANE), F32),
                       pltpu.VMEM((2, ROW_SUB, SC_TOKB, ROW_LANE), F32),
                       pltpu.VMEM((SC_LANES, SC_LANES), F32),
                       pltpu.VMEM((SC_LANES,), F32),
                       pltpu.VMEM((SC_RING, SC_LANES, ROW_SUB, ROW_LANE), F32),
                       pltpu.VMEM((SC_RING, SC_LANES, ROW_SUB, ROW_LANE), F32),
                       pltpu.SemaphoreType.DMA((SC_RING,)),
                       pltpu.SemaphoreType.DMA((SC_RING,)),
                       pltpu.SemaphoreType.DMA((2,))],
        compiler_params=pltpu.CompilerParams(needs_layout_passes=False),
        name="peer_experts_sc",
    )
    out = call(e, g, tiled(xm), uv)
    return out.transpose(0, 2, 1, 3).reshape(t, d)


PEER_TB = 8


def _peer_tc_body(e_cur, e_nxt, g_ref, x_ref, uv_hbm, o_ref, buf, sem, *, n_steps):
    i = pl.program_id(0)
    slot = i % 2

    def issue(e_ref, dst):
        for t in range(PEER_TB):
            def one(k, carry, t=t):
                row = e_ref[t, k]
                pltpu.make_async_copy(uv_hbm.at[pl.ds(row, 2)], buf.at[dst, :, :, t * PEER_SEL + k],
                                      sem.at[dst]).start()
                return carry
            lax.fori_loop(0, PEER_SEL, one, 0, unroll=8)

    @pl.when(i == 0)
    def _():
        issue(e_cur, slot)

    @pl.when(i + 1 < n_steps)
    def _():
        issue(e_nxt, 1 - slot)

    pltpu.make_async_copy(buf.at[slot], buf.at[slot], sem.at[slot]).wait()

    for t in range(PEER_TB):
        rows = slice(t * PEER_SEL, (t + 1) * PEER_SEL)
        acc = buf[slot, 0, 0, rows, :] * x_ref[t:t + 1, 0:ROW_LANE]
        for j in range(1, ROW_SUB):
            acc = acc + buf[slot, 0, j, rows, :] * x_ref[t:t + 1, j * ROW_LANE:(j + 1) * ROW_LANE]
        w = g_ref[t] * _gelu(jnp.sum(acc, axis=-1, keepdims=True))
        for j in range(ROW_SUB):
            o_ref[t:t + 1, j * ROW_LANE:(j + 1) * ROW_LANE] = jnp.sum(w * buf[slot, 1, j, rows, :], axis=0,
                                                                       keepdims=True)


def _peer_experts_tc(e, g, xm, uv):
    t, d = xm.shape
    assert d == ROW_SUB * ROW_LANE and t % PEER_TB == 0
    n_steps = t // PEER_TB
    last = n_steps - 1
    n_rows = PEER_TB * PEER_SEL
    return pl.pallas_call(
        functools.partial(_peer_tc_body, n_steps=n_steps),
        grid=(n_steps,),
        in_specs=[pl.BlockSpec((PEER_TB, PEER_SEL), lambda i: (i, 0), memory_space=pltpu.SMEM),
                  pl.BlockSpec((PEER_TB, PEER_SEL), lambda i: (jnp.minimum(i + 1, last), 0), memory_space=pltpu.SMEM),
                  pl.BlockSpec((PEER_TB, PEER_SEL, 1), lambda i: (i, 0, 0)),
                  pl.BlockSpec((PEER_TB, d), lambda i: (i, 0)),
                  pl.BlockSpec(memory_space=pl.ANY)],
        out_specs=pl.BlockSpec((PEER_TB, d), lambda i: (i, 0)),
        out_shape=jax.ShapeDtypeStruct((t, d), F32),
        scratch_shapes=[pltpu.VMEM((2, 2, ROW_SUB, n_rows, ROW_LANE), F32),
                        pltpu.SemaphoreType.DMA((2,))],
        compiler_params=_cparams("arbitrary"),
        name="peer_experts_tc",
    )(e, e, g.reshape(t, PEER_SEL, 1), xm, uv)


def _peer_select(q, sub_keys):
    return tuple(_peer_topk(q, sub_keys))


def _final_body(h_ref, delta_ref, gate_ref, g_ref, o_ref):
    h = h_ref[...] + gate_ref[0] * delta_ref[...]
    o_ref[...] = _rms(h) * g_ref[...]


def _final(h, delta, gate, gain, *, rows_per_group, tm=512):
    t, d = h.shape
    row = pl.BlockSpec((tm, d), lambda i: (i, 0))
    return pl.pallas_call(
        _final_body,
        grid=(t // tm,),
        in_specs=[row, row, pl.BlockSpec((1, 1, d), lambda i: ((i * tm) // rows_per_group, 0, 0)),
                  pl.BlockSpec((1, d), lambda i: (0, 0))],
        out_specs=row,
        out_shape=jax.ShapeDtypeStruct((t, d), F32),
        compiler_params=_cparams("parallel"),
        name="final_norm",
    )(h, delta, gate, gain.reshape(1, d))


def kernel(x, c, ctx, c_ctx, w_mod, b_mod, attn_w_in, attn_w_out, lambda_q1, lambda_k1, lambda_q2, lambda_k2,
           subln_g, na_rpb, chunk_w_in, chunk_b_in, chunk_ln_g, chunk_ln_b, chunk_w_s, chunk_b_s, chunk_w_out,
           peer_w_query, peer_sub_keys, peer_u, peer_v, final_norm_g):
    b, n, d = x.shape
    n_ctx = ctx.shape[1]

    cond = jnp.concatenate([c, c_ctx[None], jnp.zeros((-(b + 1) % 8, d), F32)], axis=0)
    mods = [_adaln(cond, w_mod[i], b_mod[i]) for i in range(2)]
    m0, m1 = mods

    w_in = attn_w_in[0].astype(BF16)
    w_out = attn_w_out[0].astype(BF16)
    w_q0 = peer_w_query[0].astype(BF16)
    w_q1 = peer_w_query[1].astype(BF16)
    w_gin = chunk_w_in[0].astype(BF16)
    w_gout = chunk_w_out[0].astype(BF16)
    rope = _rope_tables(n, 512)
    bias_tabs = _na_bias_tables(na_rpb[0], n // GRID_W)
    lam_init = 0.8 - 0.6 * math.exp(-0.3 * 0)
    lam = (jnp.exp(jnp.sum(lambda_q1[0] * lambda_k1[0])) - jnp.exp(jnp.sum(lambda_q2[0] * lambda_k2[0]))
           + lam_init).astype(F32)
    ka = A_HEADS * 2 * A_QK_DIM

    def cx(m, k):
        return m[b:b + 1, k * d:(k + 1) * d].reshape(1, 1, d)

    sizes = PIPE_BATCHES if sum(PIPE_BATCHES) == b else (b,)
    starts = [sum(sizes[:ci]) for ci in range(len(sizes))]
    n_chunks = len(sizes)

    def rows(a, ci):
        return a[starts[ci]:starts[ci] + sizes[ci]]

    def lat(ci, m, k):
        return rows(m, ci)[:, k * d:(k + 1) * d].reshape(sizes[ci], 1, d)

    def stage_a(ci, x2):
        bc = sizes[ci]
        zc = _mod_mm(rows(ctx, ci).reshape(bc * n_ctx, d), cx(m0, 0), cx(m0, 1), w_in, rows_per_group=n_ctx,
                     tm=256, name="attn_in_ctx")[0]
        z = _mod_mm(x2, lat(ci, m0, 0), lat(ci, m0, 1), w_in, rows_per_group=n, tm=256, rope=rope,
                    rope_chunks=(0, 1), scale_chunks=(0, 3), name="attn_in")[0]
        z3 = z.reshape(bc, n, -1)
        zc3 = zc.reshape(bc, n_ctx, -1)
        oa = _diff_attn(z3, zc3, lam, subln_g[0], lam_init)
        ob = _na_attn(z3, zc3, bias_tabs)
        h1 = _out_proj(x2, lat(ci, m0, 2), oa.reshape(bc * n, ka), ob.reshape(bc * n, -1), w_out[:ka], w_out[ka:],
                       rows_per_group=n)
        q0, xm0 = _mod_mm(h1, lat(ci, m0, 3), lat(ci, m0, 4), w_q0, rows_per_group=n, tm=256, emit_xm=True,
                          name="peer_q0")
        return h1, xm0, _peer_select(q0, peer_sub_keys[0])

    def stage_b(ci, h1, p0):
        zg, h1b = _mod_mm(h1, lat(ci, m1, 0), lat(ci, m1, 1), w_gin, rows_per_group=n, tm=256, delta=p0,
                          gate=lat(ci, m0, 5), bias=chunk_b_in[0], act="gelu", name="gmlp_in")
        h2 = _gmlp_tail(zg, h1b, lat(ci, m1, 2), chunk_ln_g[0], chunk_ln_b[0], chunk_w_s[0], chunk_b_s[0], w_gout,
                        rows_per_group=n)
        q1, xm1 = _mod_mm(h2, lat(ci, m1, 3), lat(ci, m1, 4), w_q1, rows_per_group=n, tm=256, emit_xm=True,
                          name="peer_q1")
        return h2, xm1, _peer_select(q1, peer_sub_keys[1])

    tables = [_expert_rows(peer_u[i], peer_v[i]) for i in range(2)]

    def experts(sel, xm, tabs):
        e, g = sel
        t = xm.shape[0]
        k = int(t * PEER_TC_SHARE) // SC_TOKEN_STEP * SC_TOKEN_STEP
        if k == 0 or (t - k) % SC_TOKEN_STEP:
            return _peer_experts(e, g, xm, tabs)
        p_sc = _peer_experts(e[k:], g[k:], xm[k:], tabs)
        p_tc = _peer_experts_tc(e[:k], g[:k], xm[:k], tabs)
        return jnp.concatenate([p_tc, p_sc], axis=0)

    h1s, p0s = [], []
    prev_sel = None
    for ci in range(n_chunks):
        x2 = rows(x, ci).reshape(sizes[ci] * n, d)
        if prev_sel is not None:
            x2, prev_sel, tables = lax.optimization_barrier((x2, prev_sel, tables))
            p0s.append(experts(prev_sel, xm_prev, tables[0]))
        h1, xm_prev, prev_sel = stage_a(ci, x2)
        h1s.append(h1)
    p0s.append(experts(prev_sel, xm_prev, tables[0]))

    h2s, p1s = [], []
    prev_sel = None
    for ci in range(n_chunks):
        p0 = p0s[ci]
        if prev_sel is not None:
            p0, prev_sel = lax.optimization_barrier((p0, prev_sel))
            p1s.append(experts(prev_sel, xm_prev, tables[1]))
        h2, xm_prev, prev_sel = stage_b(ci, h1s[ci], p0)
        h2s.append(h2)
    p1s.append(experts(prev_sel, xm_prev, tables[1]))

    outs = [_final(h2s[ci], p1s[ci], lat(ci, m1, 5), final_norm_g, rows_per_group=n).reshape(sizes[ci], n, d)
            for ci in range(n_chunks)]
    return outs[0] if n_chunks == 1 else jnp.concatenate(outs, axis=0)
```

```python
import functools
import math

import numpy as np
import jax
import jax.numpy as jnp
from jax import lax
from jax.experimental import pallas as pl
from jax.experimental.pallas import tpu as pltpu
from jax.experimental.pallas import tpu_sc as plsc

F32 = jnp.float32
BF16 = jnp.bfloat16

EPS = 1e-6
GRID_W = 64
NA_ROWS = 8
NA_COLS = 16
ROPE_BASE = 10000.0
A_HEADS = 4
A_QK_DIM = 64
B_HEADS = 8
B_HEAD_DIM = 64
PEER_HEADS = 8
N_KEYS = 128
PEER_TOPK = 16
PEER_SEL = PEER_HEADS * PEER_TOPK
CHUNK = 128
C_GROUPS = 8
MASK_VALUE = -1e30

VMEM_LIMIT = 56 * 1024 * 1024
PIPE_BATCHES = (1, 2, 2, 3)
PEER_TC_SHARE = 11 / 32


def _cparams(*sem):
    return pltpu.CompilerParams(dimension_semantics=sem, vmem_limit_bytes=VMEM_LIMIT)


def _rms(x):
    return x * lax.rsqrt(jnp.mean(x * x, axis=-1, keepdims=True) + EPS)


def _gelu(x):
    return jax.nn.gelu(x)


def _adaln_body(c_ref, w_ref, b_ref, o_ref):
    cnd = c_ref[...]
    a = (cnd * jax.nn.sigmoid(cnd)).astype(BF16)
    o_ref[...] = jnp.dot(a, w_ref[...].astype(BF16), preferred_element_type=F32) + b_ref[...]


def _adaln(cond, w, b):
    m, d = cond.shape
    n = w.shape[1]
    tn = 1024
    return pl.pallas_call(
        _adaln_body,
        grid=(n // tn,),
        in_specs=[pl.BlockSpec((m, d), lambda j: (0, 0)),
                  pl.BlockSpec((d, tn), lambda j: (0, j)),
                  pl.BlockSpec((1, tn), lambda j: (0, j))],
        out_specs=pl.BlockSpec((m, tn), lambda j: (0, j)),
        out_shape=jax.ShapeDtypeStruct((m, n), F32),
        compiler_params=_cparams("parallel"),
        name="adaln",
    )(cond, w, b.reshape(1, n))


def _mod_mm_body(*refs, has_delta, has_bias, rope_chunks, scale_chunks, act, emit_xm, tn, n_out):
    it = iter(refs)
    h_ref = next(it)
    delta_ref = gate_ref = None
    if has_delta:
        delta_ref, gate_ref = next(it), next(it)
    shift_ref, scale_ref, w_ref = next(it), next(it), next(it)
    b_ref = next(it) if has_bias else None
    cos_ref = sin_ref = None
    if rope_chunks:
        cos_ref, sin_ref = next(it), next(it)
    y_ref = next(it)
    hn_ref = next(it) if has_delta else None
    xm_ref = next(it) if emit_xm else None

    h = h_ref[...]
    if has_delta:
        h = h + gate_ref[0] * delta_ref[...]
        hn_ref[...] = h
    xm = _rms(h) * (1.0 + scale_ref[0]) + shift_ref[0]
    if emit_xm:
        xm_ref[...] = xm
    xb = xm.astype(BF16)
    for c in range(n_out // tn):
        cols = slice(c * tn, (c + 1) * tn)
        y = jnp.dot(xb, w_ref[:, cols], preferred_element_type=F32)
        if has_bias:
            y = y + b_ref[:, cols]
        if c in rope_chunks:
            lane = lax.broadcasted_iota(jnp.int32, y.shape, 1)
            partner = jnp.where((lane & 16) == 0, pltpu.roll(y, tn - 16, 1), pltpu.roll(y, 16, 1))
            y = y * cos_ref[...] + partner * sin_ref[...]
        if c in scale_chunks:
            y = y * 0.125
        if act == "gelu":
            y = _gelu(y)
        y_ref[:, cols] = y


def _mod_mm(h, shift, scale, w, *, rows_per_group, tm, delta=None, gate=None, bias=None, rope=None,
            rope_chunks=(), scale_chunks=(), act="none", emit_xm=False, name="mod_mm"):
    t, d = h.shape
    n_out = w.shape[1]
    tn = 512
    groups = shift.shape[0]
    assert t % tm == 0 and rows_per_group % tm == 0 and n_out % tn == 0

    def gidx(i):
        return ((i * tm) // rows_per_group if groups > 1 else 0, 0, 0)

    row_spec = pl.BlockSpec((tm, d), lambda i: (i, 0))
    mod_spec = pl.BlockSpec((1, 1, d), gidx)
    args, specs = [h], [row_spec]
    if delta is not None:
        args += [delta, gate]
        specs += [row_spec, mod_spec]
    args += [shift, scale, w]
    specs += [mod_spec, mod_spec, pl.BlockSpec((d, n_out), lambda i: (0, 0))]
    if bias is not None:
        args.append(bias.reshape(1, n_out))
        specs.append(pl.BlockSpec((1, n_out), lambda i: (0, 0)))
    if rope_chunks:
        cos, sin = rope
        pos_blocks = cos.shape[0] // tm
        args += [cos, sin]
        specs += [pl.BlockSpec((tm, tn), lambda i: (i % pos_blocks, 0))] * 2
    out_shape = [jax.ShapeDtypeStruct((t, n_out), F32)]
    out_specs = [pl.BlockSpec((tm, n_out), lambda i: (i, 0))]
    if delta is not None:
        out_shape.append(jax.ShapeDtypeStruct((t, d), F32))
        out_specs.append(row_spec)
    if emit_xm:
        out_shape.append(jax.ShapeDtypeStruct((t, d), F32))
        out_specs.append(row_spec)
    body = functools.partial(_mod_mm_body, has_delta=delta is not None, has_bias=bias is not None,
                             rope_chunks=tuple(rope_chunks), scale_chunks=tuple(scale_chunks), act=act,
                             emit_xm=emit_xm, tn=tn, n_out=n_out)
    return pl.pallas_call(body, grid=(t // tm,), in_specs=specs, out_specs=out_specs, out_shape=out_shape,
                          compiler_params=_cparams("parallel"), name=name)(*args)


def _rope_tables(n_tokens, width):
    axis_dim = A_QK_DIM // 2
    inv_freq = 1.0 / (ROPE_BASE ** (jnp.arange(0, axis_dim, 2, dtype=F32) / axis_dim))
    t = jnp.arange(n_tokens)
    row = (t // GRID_W).astype(F32)
    col = (t % GRID_W).astype(F32)
    ang_r = row[:, None] * inv_freq
    ang_c = col[:, None] * inv_freq
    ang = jnp.concatenate([ang_r, ang_r, ang_c, ang_c], axis=1)
    sign = jnp.tile(jnp.concatenate([-jnp.ones(16, F32), jnp.ones(16, F32)]), 2)
    reps = width // A_QK_DIM
    return jnp.tile(jnp.cos(ang), (1, reps)), jnp.tile(jnp.sin(ang) * sign, (1, reps))


def _diff_attn_body(lam_ref, q_ref, k_ref, v_ref, kc_ref, vc_ref, g_ref, o_ref, *, out_scale):
    lam = lam_ref[0, 0]
    q = q_ref[0]
    k = k_ref[0].astype(BF16)
    kc = kc_ref[0].astype(BF16)
    lane = lax.broadcasted_iota(jnp.int32, q.shape, 1)
    nt = (((1,), (1,)), ((), ()))
    parts = []
    for m in range(2):
        keep = (lane < A_QK_DIM) if m == 0 else (lane >= A_QK_DIM)
        qm = jnp.where(keep, q, 0.0).astype(BF16)
        s_l = lax.dot_general(qm, k, nt, preferred_element_type=F32)
        s_c = lax.dot_general(qm, kc, nt, preferred_element_type=F32)
        mx = jnp.maximum(jnp.max(s_l, axis=-1, keepdims=True), jnp.max(s_c, axis=-1, keepdims=True))
        e_l = jnp.exp(s_l - mx)
        e_c = jnp.exp(s_c - mx)
        inv = 1.0 / (jnp.sum(e_l, axis=-1, keepdims=True) + jnp.sum(e_c, axis=-1, keepdims=True))
        parts.append((e_l, e_c, inv))
    (e1l, e1c, inv1), (e2l, e2c, inv2) = parts
    w2 = lam * inv2
    a_l = (e1l * inv1 - e2l * w2).astype(BF16)
    a_c = (e1c * inv1 - e2c * w2).astype(BF16)
    o = (jnp.dot(a_l, v_ref[0].astype(BF16), preferred_element_type=F32)
         + jnp.dot(a_c, vc_ref[0].astype(BF16), preferred_element_type=F32))
    o_ref[0] = _rms(o) * g_ref[...] * out_scale


def _diff_attn(z, zc, lam, subln_g, lam_init, *, tq=256):
    b, n, _ = z.shape
    c = zc.shape[1]
    kcol, vcol = A_HEADS, 2 * A_HEADS
    return pl.pallas_call(
        functools.partial(_diff_attn_body, out_scale=1.0 - lam_init),
        grid=(b, A_HEADS, n // tq),
        in_specs=[pl.BlockSpec((1, 1), lambda bi, h, i: (0, 0), memory_space=pltpu.SMEM),
                  pl.BlockSpec((1, tq, 128), lambda bi, h, i: (bi, i, h)),
                  pl.BlockSpec((1, n, 128), lambda bi, h, i: (bi, 0, kcol + h)),
                  pl.BlockSpec((1, n, 128), lambda bi, h, i: (bi, 0, vcol + h)),
                  pl.BlockSpec((1, c, 128), lambda bi, h, i: (bi, 0, kcol + h)),
                  pl.BlockSpec((1, c, 128), lambda bi, h, i: (bi, 0, vcol + h)),
                  pl.BlockSpec((1, 128), lambda bi, h, i: (0, 0))],
        out_specs=pl.BlockSpec((1, tq, 128), lambda bi, h, i: (bi, i, h)),
        out_shape=jax.ShapeDtypeStruct((b, n, A_HEADS * 128), F32),
        compiler_params=_cparams("parallel", "parallel", "arbitrary"),
        name="diff_attn",
    )(lam.reshape(1, 1), z, z, z, zc, zc, subln_g.reshape(1, 128))


NA_QROWS = 8
NA_KROWS = 16
NA_SEG = 4 * GRID_W


def _na_window_start(j, rows):
    return jnp.clip(NA_QROWS * j - NA_ROWS // 2, 0, rows - NA_KROWS)


def _na_bias_tables(rpb, rows):
    n_blocks = rows // NA_QROWS
    h = rpb.shape[0]
    ic = np.clip(np.arange(GRID_W)[None, :] - np.arange(GRID_W)[:, None] + (NA_COLS - 1), 0, 2 * NA_COLS - 2)
    toep = jnp.pad(rpb[:, :, ic], ((0, 0), (NA_KROWS, NA_KROWS), (0, 0), (0, 0)))
    tabs, oks = [], []
    for j in (0, 1, n_blocks - 1):
        ks = int(np.clip(NA_QROWS * j - NA_ROWS // 2, 0, rows - NA_KROWS))
        slabs = []
        for rq in range(NA_QROWS):
            dr0 = ks - (NA_QROWS * j + rq) + (NA_ROWS - 1) + NA_KROWS
            slab = toep[:, dr0:dr0 + NA_KROWS]
            slabs.append(slab.transpose(0, 2, 1, 3).reshape(h, GRID_W, NA_KROWS * GRID_W))
        tabs.append(jnp.concatenate(slabs, axis=1))
        r = (NA_QROWS * j + np.arange(NA_QROWS))[:, None, None, None]
        cq = np.arange(GRID_W)[None, :, None, None]
        kr = (ks + np.arange(NA_KROWS))[None, None, :, None]
        kc = np.arange(GRID_W)[None, None, None, :]
        r0 = np.clip(r - NA_ROWS // 2, 0, rows - NA_ROWS)
        c0 = np.clip(cq - NA_COLS // 2, 0, GRID_W - NA_COLS)
        ok = (kr >= r0) & (kr < r0 + NA_ROWS) & (kc >= c0) & (kc < c0 + NA_COLS)
        oks.append(ok.reshape(NA_QROWS * GRID_W, NA_KROWS * GRID_W))
    return jnp.where(np.stack(oks)[None], jnp.stack(tabs, axis=1), MASK_VALUE)


def _na_body(q_ref, k0, k1, k2, k3, v0, v1, v2, v3, kc_ref, vc_ref, bias_ref, o_ref):
    q = q_ref[0]
    lane = lax.broadcasted_iota(jnp.int32, q.shape, 1)
    nt = (((1,), (1,)), ((), ()))
    ks = [r[0].astype(BF16) for r in (k0, k1, k2, k3)]
    vs = [r[0].astype(BF16) for r in (v0, v1, v2, v3)]
    kc = kc_ref[0].astype(BF16)
    vc = vc_ref[0].astype(BF16)
    outs = []
    for hh in range(2):
        keep = (lane < B_HEAD_DIM) if hh == 0 else (lane >= B_HEAD_DIM)
        qh = jnp.where(keep, q, 0.0).astype(BF16)
        segs = [lax.dot_general(qh, ks[i], nt, preferred_element_type=F32)
                + bias_ref[hh, 0, :, i * NA_SEG:(i + 1) * NA_SEG] for i in range(4)]
        s_c = lax.dot_general(qh, kc, nt, preferred_element_type=F32)
        mx = jnp.max(s_c, axis=-1, keepdims=True)
        for s in segs:
            mx = jnp.maximum(mx, jnp.max(s, axis=-1, keepdims=True))
        e_c = jnp.exp(s_c - mx)
        den = jnp.sum(e_c, axis=-1, keepdims=True)
        es = []
        for s in segs:
            e = jnp.exp(s - mx)
            den = den + jnp.sum(e, axis=-1, keepdims=True)
            es.append(e)
        inv = 1.0 / den
        o = jnp.dot((e_c * inv).astype(BF16), vc, preferred_element_type=F32)
        for e, v in zip(es, vs):
            o = o + jnp.dot((e * inv).astype(BF16), v, preferred_element_type=F32)
        outs.append(o)
    o_ref[0] = jnp.where(lane < B_HEAD_DIM, outs[0], outs[1])


def _na_attn(z, zc, bias_tabs):
    b, n, _ = z.shape
    c = zc.shape[1]
    rows = n // GRID_W
    n_blocks = rows // NA_QROWS
    tq = NA_QROWS * GRID_W
    qcol, kcol, vcol = 12, 16, 20
    seg_rows = NA_SEG // GRID_W

    def kv_spec(col0, i):
        return pl.BlockSpec((1, NA_SEG, 128),
                            lambda hp, j, bi: (bi, _na_window_start(j, rows) // seg_rows + i, col0 + hp))

    def variant(j):
        return jnp.where(j == 0, 0, jnp.where(j == n_blocks - 1, 2, 1))

    in_specs = [pl.BlockSpec((1, tq, 128), lambda hp, j, bi: (bi, j, qcol + hp))]
    in_specs += [kv_spec(kcol, i) for i in range(4)]
    in_specs += [kv_spec(vcol, i) for i in range(4)]
    in_specs += [pl.BlockSpec((1, c, 128), lambda hp, j, bi: (bi, 0, kcol + hp)),
                 pl.BlockSpec((1, c, 128), lambda hp, j, bi: (bi, 0, vcol + hp)),
                 pl.BlockSpec((2, 1, tq, NA_KROWS * GRID_W), lambda hp, j, bi: (hp, variant(j), 0, 0))]
    return pl.pallas_call(
        _na_body,
        grid=(B_HEADS // 2, n_blocks, b),
        in_specs=in_specs,
        out_specs=pl.BlockSpec((1, tq, 128), lambda hp, j, bi: (bi, j, hp)),
        out_shape=jax.ShapeDtypeStruct((b, n, B_HEADS * B_HEAD_DIM), F32),
        compiler_params=_cparams("parallel", "parallel", "arbitrary"),
        name="na_attn",
    )(z, z, z, z, z, z, z, z, z, zc, zc, bias_tabs)


def _out_proj_body(h_ref, gate_ref, xa_ref, xb_ref, wa_ref, wb_ref, o_ref):
    y = (jnp.dot(xa_ref[...].astype(BF16), wa_ref[...], preferred_element_type=F32)
         + jnp.dot(xb_ref[...].astype(BF16), wb_ref[...], preferred_element_type=F32))
    o_ref[...] = h_ref[...] + gate_ref[0] * y


def _out_proj(h, gate, xa, xb, wa, wb, *, rows_per_group, tm=512):
    t, d = h.shape
    ka, kb = xa.shape[1], xb.shape[1]
    return pl.pallas_call(
        _out_proj_body,
        grid=(t // tm,),
        in_specs=[pl.BlockSpec((tm, d), lambda i: (i, 0)),
                  pl.BlockSpec((1, 1, d), lambda i: ((i * tm) // rows_per_group, 0, 0)),
                  pl.BlockSpec((tm, ka), lambda i: (i, 0)),
                  pl.BlockSpec((tm, kb), lambda i: (i, 0)),
                  pl.BlockSpec((ka, d), lambda i: (0, 0)),
                  pl.BlockSpec((kb, d), lambda i: (0, 0))],
        out_specs=pl.BlockSpec((tm, d), lambda i: (i, 0)),
        out_shape=jax.ShapeDtypeStruct((t, d), F32),
        compiler_params=_cparams("parallel"),
        name="out_proj",
    )(h, gate, xa, xb, wa, wb)


def _gmlp_tail_body(z_ref, h_ref, gate_ref, lng_ref, lnb_ref, ws_ref, bs_ref, wo_ref, o_ref, gated_ref, *, tm, width):
    gw = width // C_GROUPS
    for ci in range(tm // CHUNK):
        rows = slice(ci * CHUNK, (ci + 1) * CHUNK)
        v = z_ref[rows, width:]
        mu = jnp.mean(v, axis=-1, keepdims=True)
        var = jnp.mean(jnp.square(v - mu), axis=-1, keepdims=True)
        vn = ((v - mu) * lax.rsqrt(var + EPS)) * lng_ref[...] + lnb_ref[...]
        vb = vn.astype(BF16)
        for g in range(C_GROUPS):
            cols = slice(g * gw, (g + 1) * gw)
            sv = jnp.dot(ws_ref[g], vb[:, cols], preferred_element_type=F32) + bs_ref[g]
            gated_ref[rows, cols] = (z_ref[rows, cols] * sv).astype(BF16)
    y = jnp.dot(gated_ref[...], wo_ref[...], preferred_element_type=F32)
    o_ref[...] = h_ref[...] + gate_ref[0] * y


def _gmlp_tail(z, h, gate, ln_g, ln_b, w_s, b_s, w_out, *, rows_per_group, tm=512):
    t, d = h.shape
    width = z.shape[1] // 2
    gw = width // C_GROUPS
    bsb = jnp.broadcast_to(b_s[:, :, None], (C_GROUPS, CHUNK, gw))
    return pl.pallas_call(
        functools.partial(_gmlp_tail_body, tm=tm, width=width),
        grid=(t // tm,),
        in_specs=[pl.BlockSpec((tm, 2 * width), lambda i: (i, 0)),
                  pl.BlockSpec((tm, d), lambda i: (i, 0)),
                  pl.BlockSpec((1, 1, d), lambda i: ((i * tm) // rows_per_group, 0, 0)),
                  pl.BlockSpec((1, width), lambda i: (0, 0)),
                  pl.BlockSpec((1, width), lambda i: (0, 0)),
                  pl.BlockSpec((C_GROUPS, CHUNK, CHUNK), lambda i: (0, 0, 0)),
                  pl.BlockSpec((C_GROUPS, CHUNK, gw), lambda i: (0, 0, 0)),
                  pl.BlockSpec((width, d), lambda i: (0, 0))],
        out_specs=pl.BlockSpec((tm, d), lambda i: (i, 0)),
        out_shape=jax.ShapeDtypeStruct((t, d), F32),
        scratch_shapes=[pltpu.VMEM((tm, width), BF16)],
        compiler_params=_cparams("parallel"),
        name="gmlp_tail",
    )(z, h, gate, ln_g.reshape(1, width), ln_b.reshape(1, width), w_s.astype(BF16), bsb, w_out)


def _topk_rows(scores, payloads, k):
    n_rows = scores[0].shape[0]
    riota = lax.broadcasted_iota(jnp.int32, scores[0].shape, 0)
    scores = list(scores)
    vals = [[] for _ in scores]
    pays = [[] for _ in scores]
    for _ in range(k):
        for c, (s, payload) in enumerate(zip(scores, payloads)):
            m = jnp.max(s, axis=0, keepdims=True)
            first = jnp.min(jnp.where(s == m, riota, n_rows), axis=0, keepdims=True)
            sel = riota == first
            vals[c].append(m)
            pays[c].append(first if payload is None
                           else jnp.max(jnp.where(sel, payload, -1), axis=0, keepdims=True))
            scores[c] = jnp.where(sel, -jnp.inf, s)
    return [(jnp.concatenate(v, axis=0), jnp.concatenate(p, axis=0)) for v, p in zip(vals, pays)]


def _pair_candidates(x1, x2, combine, fill):
    k = PEER_TOPK
    sub = lax.broadcasted_iota(jnp.int32, (8, x1.shape[1]), 0)
    blocks = [combine(x1[0:1], x2)]
    for a in range(1, 8):
        blocks.append(jnp.where(sub < k // (a + 1), combine(x1[a:a + 1], x2[0:8]), fill))
    blocks.append(combine(x1[8:k], x2[0:1]))
    return jnp.concatenate(blocks, axis=0)


def _peer_topk_body(q_ref, keys_ref, e_ref, g_ref, *, tt):
    nt = (((1,), (1,)), ((), ()))
    es, gs = [], []
    for h0 in range(0, PEER_HEADS, 2):
        cands, experts = [], []
        for h in (h0, h0 + 1):
            scores = []
            for p in range(2):
                hp = 2 * h + p
                qhp = q_ref[:, hp * 128:(hp + 1) * 128].astype(BF16)
                scores.append(lax.dot_general(keys_ref[hp], qhp, nt, preferred_element_type=F32))
            (s1, i1), (s2, i2) = _topk_rows(scores, [None, None], PEER_TOPK)
            cands.append(_pair_candidates(s1, s2, lambda x, y: x + y, -jnp.inf))
            experts.append(_pair_candidates(i1, i2, lambda x, y: 2 * (x * N_KEYS + y), -1))
        for sc, e in _topk_rows(cands, experts, PEER_TOPK):
            ex = jnp.exp(sc - sc[0:1])
            es.append(e)
            gs.append(ex / jnp.sum(ex, axis=0, keepdims=True))
    e_ref[...] = jnp.concatenate(es, axis=0).T
    g_ref[...] = jnp.concatenate(gs, axis=0).T


def _peer_topk(q, sub_keys, *, tt=128):
    t, qw = q.shape
    keys = sub_keys.reshape(2 * PEER_HEADS, N_KEYS, qw // (2 * PEER_HEADS)).astype(BF16)
    return pl.pallas_call(
        functools.partial(_peer_topk_body, tt=tt),
        grid=(t // tt,),
        in_specs=[pl.BlockSpec((tt, qw), lambda i: (i, 0)),
                  pl.BlockSpec(keys.shape, lambda i: (0, 0, 0))],
        out_specs=[pl.BlockSpec((tt, PEER_SEL), lambda i: (i, 0)),
                   pl.BlockSpec((tt, PEER_SEL), lambda i: (i, 0))],
        out_shape=[jax.ShapeDtypeStruct((t, PEER_SEL), jnp.int32),
                   jax.ShapeDtypeStruct((t, PEER_SEL), F32)],
        compiler_params=_cparams("parallel"),
        name="peer_topk",
    )(q, keys)


SC_CORES = 2
SC_SUBCORES = 16
SC_LANES = 16
SC_TILES = SC_CORES * SC_SUBCORES
ROW_SUB = 8
ROW_LANE = 128
VECS_PER_SUB = ROW_LANE // SC_LANES
D_BLOCKS = 4
VECS_PER_BLOCK = ROW_SUB * VECS_PER_SUB // D_BLOCKS


def _sc_gelu(s):
    z = 0.7978845608028654 * (s + 0.044715 * (s * s * s))
    tanh_z = 1.0 - 2.0 / (jnp.exp(2.0 * z) + 1.0)
    return 0.5 * s * (1.0 + tanh_z)


def _vec_slot(jb, k):
    v = jb * VECS_PER_BLOCK + k
    return v // VECS_PER_SUB, pl.ds((v % VECS_PER_SUB) * SC_LANES, SC_LANES)


SC_TOKB = 8
PART_SUMS = 4
SC_RING = 3
SC_TOKEN_STEP = SC_TILES * SC_TOKB


def _peer_sc_body(e_hbm, g_hbm, x_hbm, uv_hbm, o_hbm, idx_v, idx1_v, g_v, x_v, out_v, acc_v, w_v, ubuf, vbuf,
                  sem_u, sem_v, sem_o, *, batches_per_tile):
    wid = lax.axis_index("s") * SC_CORES + lax.axis_index("c")
    base = wid * batches_per_tile
    lane = lax.iota(jnp.int32, SC_LANES)
    n_groups = PEER_SEL // SC_LANES
    gpb = SC_TOKB * n_groups

    def sel16(ref, gi):
        return ref.at[gi // n_groups, pl.ds((gi % n_groups) * SC_LANES, SC_LANES)]

    def gather(gi, slot):
        return (pltpu.make_async_copy(uv_hbm.at[sel16(idx_v, gi)], ubuf.at[slot], sem_u.at[slot]),
                pltpu.make_async_copy(uv_hbm.at[sel16(idx1_v, gi)], vbuf.at[slot], sem_v.at[slot]))

    def start(gi, slot):
        for cp in gather(gi, slot):
            cp.start()

    def out_copy(batch, par):
        return pltpu.make_async_copy(out_v.at[par], o_hbm.at[batch], sem_o.at[par])

    def u_phase(slot, t, gi):
        for jb in range(D_BLOCKS):
            xs = []
            for k in range(VECS_PER_BLOCK):
                sub, ls = _vec_slot(jb, k)
                xs.append(x_v[sub, t, ls])

            @plsc.parallel_loop(0, SC_LANES, unroll=2)
            def _(r, jb=jb, xs=xs):
                parts = []
                per = VECS_PER_BLOCK // PART_SUMS
                for q in range(PART_SUMS):
                    p = xs[q * per] * ubuf[(slot, r) + _vec_slot(jb, q * per)]
                    for k in range(q * per + 1, (q + 1) * per):
                        p = p + xs[k] * ubuf[(slot, r) + _vec_slot(jb, k)]
                    parts.append(p)
                a = (parts[0] + parts[1]) + (parts[2] + parts[3])
                if jb > 0:
                    a = a + acc_v[r, :]
                acc_v[r, :] = a
        cols = [plsc.load_gather(acc_v, [lane, jnp.full((SC_LANES,), l, jnp.int32)]) for l in range(SC_LANES)]
        while len(cols) > 1:
            cols = [cols[i] + cols[i + 1] for i in range(0, len(cols), 2)]
        w_v[...] = sel16(g_v, gi)[...] * _sc_gelu(cols[0])

    def v_phase(slot, par, t):
        for jb in range(D_BLOCKS):
            slots = [_vec_slot(jb, k) for k in range(VECS_PER_BLOCK)]
            init = tuple(out_v[par, sub, t, ls] for sub, ls in slots)

            def add_row(r, os, jb=jb):
                wb = plsc.load_gather(w_v, [jnp.full((SC_LANES,), r, jnp.int32)])
                return tuple(os[k] + wb * vbuf[(slot, r) + _vec_slot(jb, k)] for k in range(VECS_PER_BLOCK))

            os = plsc.parallel_loop(0, SC_LANES, unroll=2, carry=init)(add_row)
            for (sub, ls), o in zip(slots, os):
                out_v[par, sub, t, ls] = o

    @pl.loop(0, batches_per_tile)
    def _(bi):
        batch = base + bi
        par = bi % 2
        pltpu.sync_copy(e_hbm.at[pl.ds(batch * SC_TOKB, SC_TOKB)], idx_v)
        pltpu.sync_copy(g_hbm.at[pl.ds(batch * SC_TOKB, SC_TOKB)], g_v)
        pltpu.sync_copy(x_hbm.at[batch], x_v)
        for tt in range(SC_TOKB):
            for c in range(n_groups):
                cols = pl.ds(c * SC_LANES, SC_LANES)
                idx1_v[tt, cols] = idx_v[tt, cols] + 1
        for gi0 in range(SC_RING - 1):
            start(gi0, gi0)

        @pl.when(bi >= 2)
        def _():
            out_copy(batch, par).wait()

        def step(gi, slot):
            gi = jnp.asarray(gi, jnp.int32)
            t = gi // n_groups

            @pl.when(gi + SC_RING - 1 < gpb)
            def _():
                start(gi + SC_RING - 1, (slot + SC_RING - 1) % SC_RING)

            @pl.when(gi % n_groups == 0)
            def _():
                for j in range(ROW_SUB):
                    for l in range(VECS_PER_SUB):
                        out_v[par, j, t, pl.ds(l * SC_LANES, SC_LANES)] = jnp.zeros((SC_LANES,), F32)

            cu, cv = gather(gi, slot)
            cu.wait()
            u_phase(slot, t, gi)
            cv.wait()
            v_phase(slot, par, t)

        @pl.loop(0, gpb // SC_RING)
        def _(it):
            for slot in range(SC_RING):
                step(it * SC_RING + slot, slot)

        for gi in range(gpb - gpb % SC_RING, gpb):
            step(gi, gi % SC_RING)

        out_copy(batch, par).start()

    for par in range(2):
        out_copy(base, par).wait()


def _expert_rows(u, v):
    n_e = u.shape[0]
    rows = jnp.stack([u.reshape(n_e, ROW_SUB, ROW_LANE), v.reshape(n_e, ROW_SUB, ROW_LANE)], axis=1)
    return rows.reshape(2 * n_e, ROW_SUB, ROW_LANE)


def _peer_experts(e, g, xm, uv):
    t, d = xm.shape
    assert d == ROW_SUB * ROW_LANE and t % SC_TOKEN_STEP == 0 and t >= 2 * SC_TOKEN_STEP and e.shape == (t, PEER_SEL)
    n_batches = t // SC_TOKB

    def tiled(a):
        return a.reshape(n_batches, SC_TOKB, ROW_SUB, ROW_LANE).transpose(0, 2, 1, 3)

    call = pl.kernel(
        functools.partial(_peer_sc_body, batches_per_tile=n_batches // SC_TILES),
        out_type=jax.ShapeDtypeStruct((n_batches, ROW_SUB, SC_TOKB, ROW_LANE), F32),
        mesh=plsc.VectorSubcoreMesh(core_axis_name="c", subcore_axis_name="s"),
        scratch_types=[pltpu.VMEM((SC_TOKB, PEER_SEL), jnp.int32),
                       pltpu.VMEM((SC_TOKB, PEER_SEL), jnp.int32),
                       pltpu.VMEM((SC_TOKB, PEER_SEL), F32),
                       pltpu.VMEM((ROW_SUB, SC_TOKB, ROW_LANE), F32),
                       pltpu.VMEM((2, ROW_SUB, SC_TOKB, ROW_LANE), F32),
                       pltpu.VMEM((SC_LANES, SC_LANES), F32),
                       pltpu.VMEM((SC_LANES,), F32),
                       pltpu.VMEM((SC_RING, SC_LANES, ROW_SUB, ROW_LANE), F32),
                       pltpu.VMEM((SC_RING, SC_LANES, ROW_SUB, ROW_LANE), F32),
                       pltpu.SemaphoreType.DMA((SC_RING,)),
                       pltpu.SemaphoreType.DMA((SC_RING,)),
                       pltpu.SemaphoreType.DMA((2,))],
        compiler_params=pltpu.CompilerParams(needs_layout_passes=False),
        name="peer_experts_sc",
    )
    out = call(e, g, tiled(xm), uv)
    return out.transpose(0, 2, 1, 3).reshape(t, d)


PEER_TB = 8
HALF_WORDS = ROW_SUB * ROW_LANE // 2
WORD_SUB = HALF_WORDS // ROW_LANE


def _expert_rows_bf16(u, v):
    def pack(a):
        bits = lax.bitcast_convert_type(a.astype(BF16), jnp.uint16).astype(jnp.uint32)
        return lax.bitcast_convert_type((bits[:, HALF_WORDS:] << 16) | bits[:, :HALF_WORDS], jnp.int32)

    n_e = u.shape[0]
    return jnp.stack([pack(u), pack(v)], axis=1).reshape(n_e, 2 * WORD_SUB, ROW_LANE)


def _bf16_pair(words):
    low = lax.bitcast_convert_type(words << 16, F32)
    high = lax.bitcast_convert_type(words & jnp.int32(-65536), F32)
    return low, high


def _peer_tc_body(e_cur, e_nxt, g_ref, x_ref, uv_hbm, o_ref, buf, sem, *, n_steps):
    i = pl.program_id(0)
    slot = i % 2

    def issue(e_ref, dst):
        for t in range(PEER_TB):
            def one(k, carry, t=t):
                row = e_ref[t, k] >> 1
                pltpu.make_async_copy(uv_hbm.at[row], buf.at[dst, :, t * PEER_SEL + k], sem.at[dst]).start()
                return carry
            lax.fori_loop(0, PEER_SEL, one, 0, unroll=8)

    @pl.when(i == 0)
    def _():
        issue(e_cur, slot)

    @pl.when(i + 1 < n_steps)
    def _():
        issue(e_nxt, 1 - slot)

    pltpu.make_async_copy(buf.at[slot], buf.at[slot], sem.at[slot]).wait()

    for t in range(PEER_TB):
        rows = slice(t * PEER_SEL, (t + 1) * PEER_SEL)
        acc = None
        for j in range(WORD_SUB):
            lo_cols = slice(j * ROW_LANE, (j + 1) * ROW_LANE)
            hi_cols = slice(HALF_WORDS + j * ROW_LANE, HALF_WORDS + (j + 1) * ROW_LANE)
            low, high = _bf16_pair(buf[slot, j, rows, :])
            term = low * x_ref[t:t + 1, lo_cols] + high * x_ref[t:t + 1, hi_cols]
            acc = term if acc is None else acc + term
        w = g_ref[t] * _gelu(jnp.sum(acc, axis=-1, keepdims=True))
        for j in range(WORD_SUB):
            lo_cols = slice(j * ROW_LANE, (j + 1) * ROW_LANE)
            hi_cols = slice(HALF_WORDS + j * ROW_LANE, HALF_WORDS + (j + 1) * ROW_LANE)
            low, high = _bf16_pair(buf[slot, WORD_SUB + j, rows, :])
            o_ref[t:t + 1, lo_cols] = jnp.sum(w * low, axis=0, keepdims=True)
            o_ref[t:t + 1, hi_cols] = jnp.sum(w * high, axis=0, keepdims=True)


def _peer_experts_tc(e, g, xm, uvb):
    t, d = xm.shape
    assert d == 2 * HALF_WORDS and t % PEER_TB == 0
    n_steps = t // PEER_TB
    last = n_steps - 1
    n_rows = PEER_TB * PEER_SEL
    return pl.pallas_call(
        functools.partial(_peer_tc_body, n_steps=n_steps),
        grid=(n_steps,),
        in_specs=[pl.BlockSpec((PEER_TB, PEER_SEL), lambda i: (i, 0), memory_space=pltpu.SMEM),
                  pl.BlockSpec((PEER_TB, PEER_SEL), lambda i: (jnp.minimum(i + 1, last), 0), memory_space=pltpu.SMEM),
                  pl.BlockSpec((PEER_TB, PEER_SEL, 1), lambda i: (i, 0, 0)),
                  pl.BlockSpec((PEER_TB, d), lambda i: (i, 0)),
                  pl.BlockSpec(memory_space=pl.ANY)],
        out_specs=pl.BlockSpec((PEER_TB, d), lambda i: (i, 0)),
        out_shape=jax.ShapeDtypeStruct((t, d), F32),
        scratch_shapes=[pltpu.VMEM((2, 2 * WORD_SUB, n_rows, ROW_LANE), jnp.int32),
                        pltpu.SemaphoreType.DMA((2,))],
        compiler_params=_cparams("arbitrary"),
        name="peer_experts_tc",
    )(e, e, g.reshape(t, PEER_SEL, 1), xm, uvb)


def _peer_select(q, sub_keys):
    return tuple(_peer_topk(q, sub_keys))


def _final_body(h_ref, delta_ref, gate_ref, g_ref, o_ref):
    h = h_ref[...] + gate_ref[0] * delta_ref[...]
    o_ref[...] = _rms(h) * g_ref[...]


def _final(h, delta, gate, gain, *, rows_per_group, tm=512):
    t, d = h.shape
    row = pl.BlockSpec((tm, d), lambda i: (i, 0))
    return pl.pallas_call(
        _final_body,
        grid=(t // tm,),
        in_specs=[row, row, pl.BlockSpec((1, 1, d), lambda i: ((i * tm) // rows_per_group, 0, 0)),
                  pl.BlockSpec((1, d), lambda i: (0, 0))],
        out_specs=row,
        out_shape=jax.ShapeDtypeStruct((t, d), F32),
        compiler_params=_cparams("parallel"),
        name="final_norm",
    )(h, delta, gate, gain.reshape(1, d))


def kernel(x, c, ctx, c_ctx, w_mod, b_mod, attn_w_in, attn_w_out, lambda_q1, lambda_k1, lambda_q2, lambda_k2,
           subln_g, na_rpb, chunk_w_in, chunk_b_in, chunk_ln_g, chunk_ln_b, chunk_w_s, chunk_b_s, chunk_w_out,
           peer_w_query, peer_sub_keys, peer_u, peer_v, final_norm_g):
    b, n, d = x.shape
    n_ctx = ctx.shape[1]

    cond = jnp.concatenate([c, c_ctx[None], jnp.zeros((-(b + 1) % 8, d), F32)], axis=0)
    mods = [_adaln(cond, w_mod[i], b_mod[i]) for i in range(2)]
    m0, m1 = mods

    w_in = attn_w_in[0].astype(BF16)
    w_out = attn_w_out[0].astype(BF16)
    w_q0 = peer_w_query[0].astype(BF16)
    w_q1 = peer_w_query[1].astype(BF16)
    w_gin = chunk_w_in[0].astype(BF16)
    w_gout = chunk_w_out[0].astype(BF16)
    rope = _rope_tables(n, 512)
    bias_tabs = _na_bias_tables(na_rpb[0], n // GRID_W)
    lam_init = 0.8 - 0.6 * math.exp(-0.3 * 0)
    lam = (jnp.exp(jnp.sum(lambda_q1[0] * lambda_k1[0])) - jnp.exp(jnp.sum(lambda_q2[0] * lambda_k2[0]))
           + lam_init).astype(F32)
    ka = A_HEADS * 2 * A_QK_DIM

    def cx(m, k):
        return m[b:b + 1, k * d:(k + 1) * d].reshape(1, 1, d)

    sizes = PIPE_BATCHES if sum(PIPE_BATCHES) == b else (b,)
    starts = [sum(sizes[:ci]) for ci in range(len(sizes))]
    n_chunks = len(sizes)

    def rows(a, ci):
        return a[starts[ci]:starts[ci] + sizes[ci]]

    def lat(ci, m, k):
        return rows(m, ci)[:, k * d:(k + 1) * d].reshape(sizes[ci], 1, d)

    def stage_a(ci, x2):
        bc = sizes[ci]
        zc = _mod_mm(rows(ctx, ci).reshape(bc * n_ctx, d), cx(m0, 0), cx(m0, 1), w_in, rows_per_group=n_ctx,
                     tm=256, name="attn_in_ctx")[0]
        z = _mod_mm(x2, lat(ci, m0, 0), lat(ci, m0, 1), w_in, rows_per_group=n, tm=256, rope=rope,
                    rope_chunks=(0, 1), scale_chunks=(0, 3), name="attn_in")[0]
        z3 = z.reshape(bc, n, -1)
        zc3 = zc.reshape(bc, n_ctx, -1)
        oa = _diff_attn(z3, zc3, lam, subln_g[0], lam_init)
        ob = _na_attn(z3, zc3, bias_tabs)
        h1 = _out_proj(x2, lat(ci, m0, 2), oa.reshape(bc * n, ka), ob.reshape(bc * n, -1), w_out[:ka], w_out[ka:],
                       rows_per_group=n)
        q0, xm0 = _mod_mm(h1, lat(ci, m0, 3), lat(ci, m0, 4), w_q0, rows_per_group=n, tm=256, emit_xm=True,
                          name="peer_q0")
        return h1, xm0, _peer_select(q0, peer_sub_keys[0])

    def stage_b(ci, h1, p0):
        zg, h1b = _mod_mm(h1, lat(ci, m1, 0), lat(ci, m1, 1), w_gin, rows_per_group=n, tm=256, delta=p0,
                          gate=lat(ci, m0, 5), bias=chunk_b_in[0], act="gelu", name="gmlp_in")
        h2 = _gmlp_tail(zg, h1b, lat(ci, m1, 2), chunk_ln_g[0], chunk_ln_b[0], chunk_w_s[0], chunk_b_s[0], w_gout,
                        rows_per_group=n)
        q1, xm1 = _mod_mm(h2, lat(ci, m1, 3), lat(ci, m1, 4), w_q1, rows_per_group=n, tm=256, emit_xm=True,
                          name="peer_q1")
        return h2, xm1, _peer_select(q1, peer_sub_keys[1])

    tables = [(_expert_rows(peer_u[i], peer_v[i]), _expert_rows_bf16(peer_u[i], peer_v[i])) for i in range(2)]

    def experts(sel, xm, tabs):
        e, g = sel
        t = xm.shape[0]
        k = int(t * PEER_TC_SHARE) // SC_TOKEN_STEP * SC_TOKEN_STEP
        if k == 0 or (t - k) % SC_TOKEN_STEP:
            return _peer_experts(e, g, xm, tabs[0])
        p_sc = _peer_experts(e[k:], g[k:], xm[k:], tabs[0])
        p_tc = _peer_experts_tc(e[:k], g[:k], xm[:k], tabs[1])
        return jnp.concatenate([p_tc, p_sc], axis=0)

    h1s, p0s = [], []
    prev_sel = None
    for ci in range(n_chunks):
        x2 = rows(x, ci).reshape(sizes[ci] * n, d)
        if prev_sel is not None:
            x2, prev_sel, tables = lax.optimization_barrier((x2, prev_sel, tables))
            p0s.append(experts(prev_sel, xm_prev, tables[0]))
        h1, xm_prev, prev_sel = stage_a(ci, x2)
        h1s.append(h1)
    p0s.append(experts(prev_sel, xm_prev, tables[0]))

    h2s, p1s = [], []
    prev_sel = None
    for ci in range(n_chunks):
        p0 = p0s[ci]
        if prev_sel is not None:
            p0, prev_sel = lax.optimization_barrier((p0, prev_sel))
            p1s.append(experts(prev_sel, xm_prev, tables[1]))
        h2, xm_prev, prev_sel = stage_b(ci, h1s[ci], p0)
        h2s.append(h2)
    p1s.append(experts(prev_sel, xm_prev, tables[1]))

    outs = [_final(h2s[ci], p1s[ci], lat(ci, m1, 5), final_norm_g, rows_per_group=n).reshape(sizes[ci], n, d)
            for ci in range(n_chunks)]
    return outs[0] if n_chunks == 1 else jnp.concatenate(outs, axis=0)
```

```python
import functools
import math

import numpy as np
import jax
import jax.numpy as jnp
from jax import lax
from jax.experimental import pallas as pl
from jax.experimental.pallas import tpu as pltpu
from jax.experimental.pallas import tpu_sc as plsc

F32 = jnp.float32
BF16 = jnp.bfloat16

EPS = 1e-6
GRID_W = 64
NA_ROWS = 8
NA_COLS = 16
ROPE_BASE = 10000.0
A_HEADS = 4
A_QK_DIM = 64
B_HEADS = 8
B_HEAD_DIM = 64
PEER_HEADS = 8
N_KEYS = 128
PEER_TOPK = 16
PEER_SEL = PEER_HEADS * PEER_TOPK
CHUNK = 128
C_GROUPS = 8
MASK_VALUE = -1e30

VMEM_LIMIT = 56 * 1024 * 1024
PIPE_BATCHES = (1, 2, 2, 3)
PEER_TC_SHARE = 11 / 32


def _cparams(*sem):
    return pltpu.CompilerParams(dimension_semantics=sem, vmem_limit_bytes=VMEM_LIMIT)


def _rms(x):
    return x * lax.rsqrt(jnp.mean(x * x, axis=-1, keepdims=True) + EPS)


def _gelu(x):
    return jax.nn.gelu(x)


def _adaln_body(c_ref, w_ref, b_ref, o_ref):
    cnd = c_ref[...]
    a = (cnd * jax.nn.sigmoid(cnd)).astype(BF16)
    o_ref[...] = jnp.dot(a, w_ref[...].astype(BF16), preferred_element_type=F32) + b_ref[...]


def _adaln(cond, w, b):
    m, d = cond.shape
    n = w.shape[1]
    tn = 1024
    return pl.pallas_call(
        _adaln_body,
        grid=(n // tn,),
        in_specs=[pl.BlockSpec((m, d), lambda j: (0, 0)),
                  pl.BlockSpec((d, tn), lambda j: (0, j)),
                  pl.BlockSpec((1, tn), lambda j: (0, j))],
        out_specs=pl.BlockSpec((m, tn), lambda j: (0, j)),
        out_shape=jax.ShapeDtypeStruct((m, n), F32),
        compiler_params=_cparams("parallel"),
        name="adaln",
    )(cond, w, b.reshape(1, n))


def _piece_specs(pieces, tm, d):
    specs, starts, b0 = [], [], 0
    for p in pieces:
        nb = p.shape[0] // tm
        assert nb * tm == p.shape[0]
        specs.append(pl.BlockSpec((tm, d), lambda i, b0=b0, nb=nb: (jnp.clip(i - b0, 0, nb - 1), 0)))
        starts.append(b0)
        b0 += nb
    return specs, tuple(starts)


def _pick_piece(refs, starts):
    i = pl.program_id(0)
    val = refs[0][...]
    for ref, start in zip(refs[1:], starts[1:]):
        val = jnp.where(i >= start, ref[...], val)
    return val


def _mod_mm_body(*refs, delta_starts, has_bias, rope_chunks, scale_chunks, act, emit_xm, tn, n_out):
    it = iter(refs)
    h_ref = next(it)
    has_delta = bool(delta_starts)
    delta_refs = gate_ref = None
    if has_delta:
        delta_refs = [next(it) for _ in delta_starts]
        gate_ref = next(it)
    shift_ref, scale_ref, w_ref = next(it), next(it), next(it)
    b_ref = next(it) if has_bias else None
    cos_ref = sin_ref = None
    if rope_chunks:
        cos_ref, sin_ref = next(it), next(it)
    y_ref = next(it)
    hn_ref = next(it) if has_delta else None
    xm_ref = next(it) if emit_xm else None

    h = h_ref[...]
    if has_delta:
        h = h + gate_ref[0] * _pick_piece(delta_refs, delta_starts)
        hn_ref[...] = h
    xm = _rms(h) * (1.0 + scale_ref[0]) + shift_ref[0]
    if emit_xm:
        xm_ref[...] = xm
    xb = xm.astype(BF16)
    for c in range(n_out // tn):
        cols = slice(c * tn, (c + 1) * tn)
        y = jnp.dot(xb, w_ref[:, cols], preferred_element_type=F32)
        if has_bias:
            y = y + b_ref[:, cols]
        if c in rope_chunks:
            lane = lax.broadcasted_iota(jnp.int32, y.shape, 1)
            partner = jnp.where((lane & 16) == 0, pltpu.roll(y, tn - 16, 1), pltpu.roll(y, 16, 1))
            y = y * cos_ref[...] + partner * sin_ref[...]
        if c in scale_chunks:
            y = y * 0.125
        if act == "gelu":
            y = _gelu(y)
        y_ref[:, cols] = y


def _mod_mm(h, shift, scale, w, *, rows_per_group, tm, delta=None, gate=None, bias=None, rope=None,
            rope_chunks=(), scale_chunks=(), act="none", emit_xm=False, name="mod_mm"):
    t, d = h.shape
    n_out = w.shape[1]
    tn = 512
    groups = shift.shape[0]
    assert t % tm == 0 and rows_per_group % tm == 0 and n_out % tn == 0

    def gidx(i):
        return ((i * tm) // rows_per_group if groups > 1 else 0, 0, 0)

    row_spec = pl.BlockSpec((tm, d), lambda i: (i, 0))
    mod_spec = pl.BlockSpec((1, 1, d), gidx)
    args, specs = [h], [row_spec]
    delta_starts = ()
    if delta is not None:
        piece_specs, delta_starts = _piece_specs(delta, tm, d)
        args += [*delta, gate]
        specs += [*piece_specs, mod_spec]
    args += [shift, scale, w]
    specs += [mod_spec, mod_spec, pl.BlockSpec((d, n_out), lambda i: (0, 0))]
    if bias is not None:
        args.append(bias.reshape(1, n_out))
        specs.append(pl.BlockSpec((1, n_out), lambda i: (0, 0)))
    if rope_chunks:
        cos, sin = rope
        pos_blocks = cos.shape[0] // tm
        args += [cos, sin]
        specs += [pl.BlockSpec((tm, tn), lambda i: (i % pos_blocks, 0))] * 2
    out_shape = [jax.ShapeDtypeStruct((t, n_out), F32)]
    out_specs = [pl.BlockSpec((tm, n_out), lambda i: (i, 0))]
    if delta is not None:
        out_shape.append(jax.ShapeDtypeStruct((t, d), F32))
        out_specs.append(row_spec)
    if emit_xm:
        out_shape.append(jax.ShapeDtypeStruct((t, d), F32))
        out_specs.append(row_spec)
    body = functools.partial(_mod_mm_body, delta_starts=delta_starts, has_bias=bias is not None,
                             rope_chunks=tuple(rope_chunks), scale_chunks=tuple(scale_chunks), act=act,
                             emit_xm=emit_xm, tn=tn, n_out=n_out)
    return pl.pallas_call(body, grid=(t // tm,), in_specs=specs, out_specs=out_specs, out_shape=out_shape,
                          compiler_params=_cparams("parallel"), name=name)(*args)


def _rope_tables(n_tokens, width):
    axis_dim = A_QK_DIM // 2
    inv_freq = 1.0 / (ROPE_BASE ** (jnp.arange(0, axis_dim, 2, dtype=F32) / axis_dim))
    t = jnp.arange(n_tokens)
    row = (t // GRID_W).astype(F32)
    col = (t % GRID_W).astype(F32)
    ang_r = row[:, None] * inv_freq
    ang_c = col[:, None] * inv_freq
    ang = jnp.concatenate([ang_r, ang_r, ang_c, ang_c], axis=1)
    sign = jnp.tile(jnp.concatenate([-jnp.ones(16, F32), jnp.ones(16, F32)]), 2)
    reps = width // A_QK_DIM
    return jnp.tile(jnp.cos(ang), (1, reps)), jnp.tile(jnp.sin(ang) * sign, (1, reps))


def _diff_attn_body(lam_ref, q_ref, k_ref, v_ref, kc_ref, vc_ref, g_ref, o_ref, *, out_scale):
    lam = lam_ref[0, 0]
    q = q_ref[0]
    k = k_ref[0].astype(BF16)
    kc = kc_ref[0].astype(BF16)
    lane = lax.broadcasted_iota(jnp.int32, q.shape, 1)
    nt = (((1,), (1,)), ((), ()))
    parts = []
    for m in range(2):
        keep = (lane < A_QK_DIM) if m == 0 else (lane >= A_QK_DIM)
        qm = jnp.where(keep, q, 0.0).astype(BF16)
        s_l = lax.dot_general(qm, k, nt, preferred_element_type=F32)
        s_c = lax.dot_general(qm, kc, nt, preferred_element_type=F32)
        mx = jnp.maximum(jnp.max(s_l, axis=-1, keepdims=True), jnp.max(s_c, axis=-1, keepdims=True))
        e_l = jnp.exp(s_l - mx)
        e_c = jnp.exp(s_c - mx)
        inv = 1.0 / (jnp.sum(e_l, axis=-1, keepdims=True) + jnp.sum(e_c, axis=-1, keepdims=True))
        parts.append((e_l, e_c, inv))
    (e1l, e1c, inv1), (e2l, e2c, inv2) = parts
    w2 = lam * inv2
    a_l = (e1l * inv1 - e2l * w2).astype(BF16)
    a_c = (e1c * inv1 - e2c * w2).astype(BF16)
    o = (jnp.dot(a_l, v_ref[0].astype(BF16), preferred_element_type=F32)
         + jnp.dot(a_c, vc_ref[0].astype(BF16), preferred_element_type=F32))
    o_ref[0] = _rms(o) * g_ref[...] * out_scale


def _diff_attn(z, zc, lam, subln_g, lam_init, *, tq=256):
    b, n, _ = z.shape
    c = zc.shape[1]
    kcol, vcol = A_HEADS, 2 * A_HEADS
    return pl.pallas_call(
        functools.partial(_diff_attn_body, out_scale=1.0 - lam_init),
        grid=(b, A_HEADS, n // tq),
        in_specs=[pl.BlockSpec((1, 1), lambda bi, h, i: (0, 0), memory_space=pltpu.SMEM),
                  pl.BlockSpec((1, tq, 128), lambda bi, h, i: (bi, i, h)),
                  pl.BlockSpec((1, n, 128), lambda bi, h, i: (bi, 0, kcol + h)),
                  pl.BlockSpec((1, n, 128), lambda bi, h, i: (bi, 0, vcol + h)),
                  pl.BlockSpec((1, c, 128), lambda bi, h, i: (bi, 0, kcol + h)),
                  pl.BlockSpec((1, c, 128), lambda bi, h, i: (bi, 0, vcol + h)),
                  pl.BlockSpec((1, 128), lambda bi, h, i: (0, 0))],
        out_specs=pl.BlockSpec((1, tq, 128), lambda bi, h, i: (bi, i, h)),
        out_shape=jax.ShapeDtypeStruct((b, n, A_HEADS * 128), F32),
        compiler_params=_cparams("parallel", "parallel", "arbitrary"),
        name="diff_attn",
    )(lam.reshape(1, 1), z, z, z, zc, zc, subln_g.reshape(1, 128))


NA_QROWS = 8
NA_KROWS = 16
NA_SEG = 4 * GRID_W


def _na_window_start(j, rows):
    return jnp.clip(NA_QROWS * j - NA_ROWS // 2, 0, rows - NA_KROWS)


def _na_bias_tables(rpb, rows):
    n_blocks = rows // NA_QROWS
    h = rpb.shape[0]
    ic = np.clip(np.arange(GRID_W)[None, :] - np.arange(GRID_W)[:, None] + (NA_COLS - 1), 0, 2 * NA_COLS - 2)
    toep = jnp.pad(rpb[:, :, ic], ((0, 0), (NA_KROWS, NA_KROWS), (0, 0), (0, 0)))
    tabs, oks = [], []
    for j in (0, 1, n_blocks - 1):
        ks = int(np.clip(NA_QROWS * j - NA_ROWS // 2, 0, rows - NA_KROWS))
        slabs = []
        for rq in range(NA_QROWS):
            dr0 = ks - (NA_QROWS * j + rq) + (NA_ROWS - 1) + NA_KROWS
            slab = toep[:, dr0:dr0 + NA_KROWS]
            slabs.append(slab.transpose(0, 2, 1, 3).reshape(h, GRID_W, NA_KROWS * GRID_W))
        tabs.append(jnp.concatenate(slabs, axis=1))
        r = (NA_QROWS * j + np.arange(NA_QROWS))[:, None, None, None]
        cq = np.arange(GRID_W)[None, :, None, None]
        kr = (ks + np.arange(NA_KROWS))[None, None, :, None]
        kc = np.arange(GRID_W)[None, None, None, :]
        r0 = np.clip(r - NA_ROWS // 2, 0, rows - NA_ROWS)
        c0 = np.clip(cq - NA_COLS // 2, 0, GRID_W - NA_COLS)
        ok = (kr >= r0) & (kr < r0 + NA_ROWS) & (kc >= c0) & (kc < c0 + NA_COLS)
        oks.append(ok.reshape(NA_QROWS * GRID_W, NA_KROWS * GRID_W))
    return jnp.where(np.stack(oks)[None], jnp.stack(tabs, axis=1), MASK_VALUE)


def _na_body(q_ref, k0, k1, k2, k3, v0, v1, v2, v3, kc_ref, vc_ref, bias_ref, o_ref):
    q = q_ref[0]
    lane = lax.broadcasted_iota(jnp.int32, q.shape, 1)
    nt = (((1,), (1,)), ((), ()))
    ks = [r[0].astype(BF16) for r in (k0, k1, k2, k3)]
    vs = [r[0].astype(BF16) for r in (v0, v1, v2, v3)]
    kc = kc_ref[0].astype(BF16)
    vc = vc_ref[0].astype(BF16)
    outs = []
    for hh in range(2):
        keep = (lane < B_HEAD_DIM) if hh == 0 else (lane >= B_HEAD_DIM)
        qh = jnp.where(keep, q, 0.0).astype(BF16)
        segs = [lax.dot_general(qh, ks[i], nt, preferred_element_type=F32)
                + bias_ref[hh, 0, :, i * NA_SEG:(i + 1) * NA_SEG] for i in range(4)]
        s_c = lax.dot_general(qh, kc, nt, preferred_element_type=F32)
        mx = jnp.max(s_c, axis=-1, keepdims=True)
        for s in segs:
            mx = jnp.maximum(mx, jnp.max(s, axis=-1, keepdims=True))
        e_c = jnp.exp(s_c - mx)
        den = jnp.sum(e_c, axis=-1, keepdims=True)
        es = []
        for s in segs:
            e = jnp.exp(s - mx)
            den = den + jnp.sum(e, axis=-1, keepdims=True)
            es.append(e)
        inv = 1.0 / den
        o = jnp.dot((e_c * inv).astype(BF16), vc, preferred_element_type=F32)
        for e, v in zip(es, vs):
            o = o + jnp.dot((e * inv).astype(BF16), v, preferred_element_type=F32)
        outs.append(o)
    o_ref[0] = jnp.where(lane < B_HEAD_DIM, outs[0], outs[1])


def _na_attn(z, zc, bias_tabs):
    b, n, _ = z.shape
    c = zc.shape[1]
    rows = n // GRID_W
    n_blocks = rows // NA_QROWS
    tq = NA_QROWS * GRID_W
    qcol, kcol, vcol = 12, 16, 20
    seg_rows = NA_SEG // GRID_W

    def kv_spec(col0, i):
        return pl.BlockSpec((1, NA_SEG, 128),
                            lambda hp, j, bi: (bi, _na_window_start(j, rows) // seg_rows + i, col0 + hp))

    def variant(j):
        return jnp.where(j == 0, 0, jnp.where(j == n_blocks - 1, 2, 1))

    in_specs = [pl.BlockSpec((1, tq, 128), lambda hp, j, bi: (bi, j, qcol + hp))]
    in_specs += [kv_spec(kcol, i) for i in range(4)]
    in_specs += [kv_spec(vcol, i) for i in range(4)]
    in_specs += [pl.BlockSpec((1, c, 128), lambda hp, j, bi: (bi, 0, kcol + hp)),
                 pl.BlockSpec((1, c, 128), lambda hp, j, bi: (bi, 0, vcol + hp)),
                 pl.BlockSpec((2, 1, tq, NA_KROWS * GRID_W), lambda hp, j, bi: (hp, variant(j), 0, 0))]
    return pl.pallas_call(
        _na_body,
        grid=(B_HEADS // 2, n_blocks, b),
        in_specs=in_specs,
        out_specs=pl.BlockSpec((1, tq, 128), lambda hp, j, bi: (bi, j, hp)),
        out_shape=jax.ShapeDtypeStruct((b, n, B_HEADS * B_HEAD_DIM), F32),
        compiler_params=_cparams("parallel", "parallel", "arbitrary"),
        name="na_attn",
    )(z, z, z, z, z, z, z, z, z, zc, zc, bias_tabs)


def _out_proj_body(h_ref, gate_ref, xa_ref, xb_ref, wa_ref, wb_ref, o_ref):
    y = (jnp.dot(xa_ref[...].astype(BF16), wa_ref[...], preferred_element_type=F32)
         + jnp.dot(xb_ref[...].astype(BF16), wb_ref[...], preferred_element_type=F32))
    o_ref[...] = h_ref[...] + gate_ref[0] * y


def _out_proj(h, gate, xa, xb, wa, wb, *, rows_per_group, tm=512):
    t, d = h.shape
    ka, kb = xa.shape[1], xb.shape[1]
    return pl.pallas_call(
        _out_proj_body,
        grid=(t // tm,),
        in_specs=[pl.BlockSpec((tm, d), lambda i: (i, 0)),
                  pl.BlockSpec((1, 1, d), lambda i: ((i * tm) // rows_per_group, 0, 0)),
                  pl.BlockSpec((tm, ka), lambda i: (i, 0)),
                  pl.BlockSpec((tm, kb), lambda i: (i, 0)),
                  pl.BlockSpec((ka, d), lambda i: (0, 0)),
                  pl.BlockSpec((kb, d), lambda i: (0, 0))],
        out_specs=pl.BlockSpec((tm, d), lambda i: (i, 0)),
        out_shape=jax.ShapeDtypeStruct((t, d), F32),
        compiler_params=_cparams("parallel"),
        name="out_proj",
    )(h, gate, xa, xb, wa, wb)


def _gmlp_tail_body(z_ref, h_ref, gate_ref, lng_ref, lnb_ref, ws_ref, bs_ref, wo_ref, o_ref, gated_ref, *, tm, width):
    gw = width // C_GROUPS
    for ci in range(tm // CHUNK):
        rows = slice(ci * CHUNK, (ci + 1) * CHUNK)
        v = z_ref[rows, width:]
        mu = jnp.mean(v, axis=-1, keepdims=True)
        var = jnp.mean(jnp.square(v - mu), axis=-1, keepdims=True)
        vn = ((v - mu) * lax.rsqrt(var + EPS)) * lng_ref[...] + lnb_ref[...]
        vb = vn.astype(BF16)
        for g in range(C_GROUPS):
            cols = slice(g * gw, (g + 1) * gw)
            sv = jnp.dot(ws_ref[g], vb[:, cols], preferred_element_type=F32) + bs_ref[g]
            gated_ref[rows, cols] = (z_ref[rows, cols] * sv).astype(BF16)
    y = jnp.dot(gated_ref[...], wo_ref[...], preferred_element_type=F32)
    o_ref[...] = h_ref[...] + gate_ref[0] * y


def _gmlp_tail(z, h, gate, ln_g, ln_b, w_s, b_s, w_out, *, rows_per_group, tm=512):
    t, d = h.shape
    width = z.shape[1] // 2
    gw = width // C_GROUPS
    bsb = jnp.broadcast_to(b_s[:, :, None], (C_GROUPS, CHUNK, gw))
    return pl.pallas_call(
        functools.partial(_gmlp_tail_body, tm=tm, width=width),
        grid=(t // tm,),
        in_specs=[pl.BlockSpec((tm, 2 * width), lambda i: (i, 0)),
                  pl.BlockSpec((tm, d), lambda i: (i, 0)),
                  pl.BlockSpec((1, 1, d), lambda i: ((i * tm) // rows_per_group, 0, 0)),
                  pl.BlockSpec((1, width), lambda i: (0, 0)),
                  pl.BlockSpec((1, width), lambda i: (0, 0)),
                  pl.BlockSpec((C_GROUPS, CHUNK, CHUNK), lambda i: (0, 0, 0)),
                  pl.BlockSpec((C_GROUPS, CHUNK, gw), lambda i: (0, 0, 0)),
                  pl.BlockSpec((width, d), lambda i: (0, 0))],
        out_specs=pl.BlockSpec((tm, d), lambda i: (i, 0)),
        out_shape=jax.ShapeDtypeStruct((t, d), F32),
        scratch_shapes=[pltpu.VMEM((tm, width), BF16)],
        compiler_params=_cparams("parallel"),
        name="gmlp_tail",
    )(z, h, gate, ln_g.reshape(1, width), ln_b.reshape(1, width), w_s.astype(BF16), bsb, w_out)


def _topk_rows(scores, payloads, k):
    n_rows = scores[0].shape[0]
    riota = lax.broadcasted_iota(jnp.int32, scores[0].shape, 0)
    scores = list(scores)
    vals = [[] for _ in scores]
    pays = [[] for _ in scores]
    for _ in range(k):
        for c, (s, payload) in enumerate(zip(scores, payloads)):
            m = jnp.max(s, axis=0, keepdims=True)
            first = jnp.min(jnp.where(s == m, riota, n_rows), axis=0, keepdims=True)
            sel = riota == first
            vals[c].append(m)
            pays[c].append(first if payload is None
                           else jnp.max(jnp.where(sel, payload, -1), axis=0, keepdims=True))
            scores[c] = jnp.where(sel, -jnp.inf, s)
    return [(jnp.concatenate(v, axis=0), jnp.concatenate(p, axis=0)) for v, p in zip(vals, pays)]


def _pair_candidates(x1, x2, combine, fill):
    k = PEER_TOPK
    sub = lax.broadcasted_iota(jnp.int32, (8, x1.shape[1]), 0)
    blocks = [combine(x1[0:1], x2)]
    for a in range(1, 8):
        blocks.append(jnp.where(sub < k // (a + 1), combine(x1[a:a + 1], x2[0:8]), fill))
    blocks.append(combine(x1[8:k], x2[0:1]))
    return jnp.concatenate(blocks, axis=0)


def _peer_topk_body(q_ref, keys_ref, e_ref, g_ref, *, tt):
    nt = (((1,), (1,)), ((), ()))
    es, gs = [], []
    for h0 in range(0, PEER_HEADS, 2):
        cands, experts = [], []
        for h in (h0, h0 + 1):
            scores = []
            for p in range(2):
                hp = 2 * h + p
                qhp = q_ref[:, hp * 128:(hp + 1) * 128].astype(BF16)
                scores.append(lax.dot_general(keys_ref[hp], qhp, nt, preferred_element_type=F32))
            (s1, i1), (s2, i2) = _topk_rows(scores, [None, None], PEER_TOPK)
            cands.append(_pair_candidates(s1, s2, lambda x, y: x + y, -jnp.inf))
            experts.append(_pair_candidates(i1, i2, lambda x, y: 2 * (x * N_KEYS + y), -1))
        for sc, e in _topk_rows(cands, experts, PEER_TOPK):
            ex = jnp.exp(sc - sc[0:1])
            es.append(e)
            gs.append(ex / jnp.sum(ex, axis=0, keepdims=True))
    e_ref[...] = jnp.concatenate(es, axis=0).T
    g_ref[...] = jnp.concatenate(gs, axis=0).T


def _peer_topk(q, sub_keys, *, tt=128):
    t, qw = q.shape
    keys = sub_keys.reshape(2 * PEER_HEADS, N_KEYS, qw // (2 * PEER_HEADS)).astype(BF16)
    return pl.pallas_call(
        functools.partial(_peer_topk_body, tt=tt),
        grid=(t // tt,),
        in_specs=[pl.BlockSpec((tt, qw), lambda i: (i, 0)),
                  pl.BlockSpec(keys.shape, lambda i: (0, 0, 0))],
        out_specs=[pl.BlockSpec((tt, PEER_SEL), lambda i: (i, 0)),
                   pl.BlockSpec((tt, PEER_SEL), lambda i: (i, 0))],
        out_shape=[jax.ShapeDtypeStruct((t, PEER_SEL), jnp.int32),
                   jax.ShapeDtypeStruct((t, PEER_SEL), F32)],
        compiler_params=_cparams("parallel"),
        name="peer_topk",
    )(q, keys)


SC_CORES = 2
SC_SUBCORES = 16
SC_LANES = 16
SC_TILES = SC_CORES * SC_SUBCORES
ROW_SUB = 8
ROW_LANE = 128
VECS_PER_SUB = ROW_LANE // SC_LANES
D_BLOCKS = 4
VECS_PER_BLOCK = ROW_SUB * VECS_PER_SUB // D_BLOCKS


def _sc_gelu(s):
    z = 0.7978845608028654 * (s + 0.044715 * (s * s * s))
    tanh_z = 1.0 - 2.0 / (jnp.exp(2.0 * z) + 1.0)
    return 0.5 * s * (1.0 + tanh_z)


def _vec_slot(jb, k):
    v = jb * VECS_PER_BLOCK + k
    return v // VECS_PER_SUB, pl.ds((v % VECS_PER_SUB) * SC_LANES, SC_LANES)


SC_TOKB = 8
PART_SUMS = 4
SC_RING = 3
SC_TOKEN_STEP = SC_TILES * SC_TOKB


def _peer_sc_body(e_hbm, g_hbm, x_hbm, uv_hbm, o_hbm, idx_v, idx1_v, g_v, x_v, out_v, acc_v, w_v, ubuf, vbuf,
                  sem_u, sem_v, sem_o, *, batches_per_tile):
    wid = lax.axis_index("s") * SC_CORES + lax.axis_index("c")
    base = wid * batches_per_tile
    lane = lax.iota(jnp.int32, SC_LANES)
    n_groups = PEER_SEL // SC_LANES
    gpb = SC_TOKB * n_groups

    def sel16(ref, gi):
        return ref.at[gi // n_groups, pl.ds((gi % n_groups) * SC_LANES, SC_LANES)]

    def gather(gi, slot):
        return (pltpu.make_async_copy(uv_hbm.at[sel16(idx_v, gi)], ubuf.at[slot], sem_u.at[slot]),
                pltpu.make_async_copy(uv_hbm.at[sel16(idx1_v, gi)], vbuf.at[slot], sem_v.at[slot]))

    def start(gi, slot):
        for cp in gather(gi, slot):
            cp.start()

    def out_copy(batch, par):
        return pltpu.make_async_copy(out_v.at[par], o_hbm.at[batch], sem_o.at[par])

    def u_phase(slot, t, gi):
        for jb in range(D_BLOCKS):
            xs = []
            for k in range(VECS_PER_BLOCK):
                sub, ls = _vec_slot(jb, k)
                xs.append(x_v[sub, t, ls])

            @plsc.parallel_loop(0, SC_LANES, unroll=2)
            def _(r, jb=jb, xs=xs):
                parts = []
                per = VECS_PER_BLOCK // PART_SUMS
                for q in range(PART_SUMS):
                    p = xs[q * per] * ubuf[(slot, r) + _vec_slot(jb, q * per)]
                    for k in range(q * per + 1, (q + 1) * per):
                        p = p + xs[k] * ubuf[(slot, r) + _vec_slot(jb, k)]
                    parts.append(p)
                a = (parts[0] + parts[1]) + (parts[2] + parts[3])
                if jb > 0:
                    a = a + acc_v[r, :]
                acc_v[r, :] = a
        cols = [plsc.load_gather(acc_v, [lane, jnp.full((SC_LANES,), l, jnp.int32)]) for l in range(SC_LANES)]
        while len(cols) > 1:
            cols = [cols[i] + cols[i + 1] for i in range(0, len(cols), 2)]
        w_v[...] = sel16(g_v, gi)[...] * _sc_gelu(cols[0])

    def v_phase(slot, par, t):
        for jb in range(D_BLOCKS):
            slots = [_vec_slot(jb, k) for k in range(VECS_PER_BLOCK)]
            init = tuple(out_v[par, sub, t, ls] for sub, ls in slots)

            def add_row(r, os, jb=jb):
                wb = plsc.load_gather(w_v, [jnp.full((SC_LANES,), r, jnp.int32)])
                return tuple(os[k] + wb * vbuf[(slot, r) + _vec_slot(jb, k)] for k in range(VECS_PER_BLOCK))

            os = plsc.parallel_loop(0, SC_LANES, unroll=2, carry=init)(add_row)
            for (sub, ls), o in zip(slots, os):
                out_v[par, sub, t, ls] = o

    @pl.loop(0, batches_per_tile)
    def _(bi):
        batch = base + bi
        par = bi % 2
        pltpu.sync_copy(e_hbm.at[pl.ds(batch * SC_TOKB, SC_TOKB)], idx_v)
        pltpu.sync_copy(g_hbm.at[pl.ds(batch * SC_TOKB, SC_TOKB)], g_v)
        pltpu.sync_copy(x_hbm.at[batch], x_v)
        for tt in range(SC_TOKB):
            for c in range(n_groups):
                cols = pl.ds(c * SC_LANES, SC_LANES)
                idx1_v[tt, cols] = idx_v[tt, cols] + 1
        for gi0 in range(SC_RING - 1):
            start(gi0, gi0)

        @pl.when(bi >= 2)
        def _():
            out_copy(batch, par).wait()

        def step(gi, slot):
            gi = jnp.asarray(gi, jnp.int32)
            t = gi // n_groups

            @pl.when(gi + SC_RING - 1 < gpb)
            def _():
                start(gi + SC_RING - 1, (slot + SC_RING - 1) % SC_RING)

            @pl.when(gi % n_groups == 0)
            def _():
                for j in range(ROW_SUB):
                    for l in range(VECS_PER_SUB):
                        out_v[par, j, t, pl.ds(l * SC_LANES, SC_LANES)] = jnp.zeros((SC_LANES,), F32)

            cu, cv = gather(gi, slot)
            cu.wait()
            u_phase(slot, t, gi)
            cv.wait()
            v_phase(slot, par, t)

        @pl.loop(0, gpb // SC_RING)
        def _(it):
            for slot in range(SC_RING):
                step(it * SC_RING + slot, slot)

        for gi in range(gpb - gpb % SC_RING, gpb):
            step(gi, gi % SC_RING)

        out_copy(batch, par).start()

    for par in range(2):
        out_copy(base, par).wait()


def _expert_rows(u, v):
    n_e = u.shape[0]
    rows = jnp.stack([u.reshape(n_e, ROW_SUB, ROW_LANE), v.reshape(n_e, ROW_SUB, ROW_LANE)], axis=1)
    return rows.reshape(2 * n_e, ROW_SUB, ROW_LANE)


def _peer_experts(e, g, xm, uv):
    t, d = xm.shape
    assert d == ROW_SUB * ROW_LANE and t % SC_TOKEN_STEP == 0 and t >= 2 * SC_TOKEN_STEP and e.shape == (t, PEER_SEL)
    n_batches = t // SC_TOKB

    def tiled(a):
        return a.reshape(n_batches, SC_TOKB, ROW_SUB, ROW_LANE).transpose(0, 2, 1, 3)

    call = pl.kernel(
        functools.partial(_peer_sc_body, batches_per_tile=n_batches // SC_TILES),
        out_type=jax.ShapeDtypeStruct((n_batches, ROW_SUB, SC_TOKB, ROW_LANE), F32),
        mesh=plsc.VectorSubcoreMesh(core_axis_name="c", subcore_axis_name="s"),
        scratch_types=[pltpu.VMEM((SC_TOKB, PEER_SEL), jnp.int32),
                       pltpu.VMEM((SC_TOKB, PEER_SEL), jnp.int32),
                       pltpu.VMEM((SC_TOKB, PEER_SEL), F32),
                       pltpu.VMEM((ROW_SUB, SC_TOKB, ROW_LANE), F32),
                       pltpu.VMEM((2, ROW_SUB, SC_TOKB, ROW_LANE), F32),
                       pltpu.VMEM((SC_LANES, SC_LANES), F32),
                       pltpu.VMEM((SC_LANES,), F32),
                       pltpu.VMEM((SC_RING, SC_LANES, ROW_SUB, ROW_LANE), F32),
                       pltpu.VMEM((SC_RING, SC_LANES, ROW_SUB, ROW_LANE), F32),
                       pltpu.SemaphoreType.DMA((SC_RING,)),
                       pltpu.SemaphoreType.DMA((SC_RING,)),
                       pltpu.SemaphoreType.DMA((2,))],
        compiler_params=pltpu.CompilerParams(needs_layout_passes=False),
        name="peer_experts_sc",
    )
    out = call(e, g, tiled(xm), uv)
    return out.transpose(0, 2, 1, 3).reshape(t, d)


PEER_TB = 8
HALF_WORDS = ROW_SUB * ROW_LANE // 2
WORD_SUB = HALF_WORDS // ROW_LANE


def _expert_rows_bf16(u, v):
    def pack(a):
        bits = lax.bitcast_convert_type(a.astype(BF16), jnp.uint16).astype(jnp.uint32)
        return lax.bitcast_convert_type((bits[:, HALF_WORDS:] << 16) | bits[:, :HALF_WORDS], jnp.int32)

    n_e = u.shape[0]
    return jnp.stack([pack(u), pack(v)], axis=1).reshape(n_e, 2 * WORD_SUB, ROW_LANE)


def _bf16_pair(words):
    low = lax.bitcast_convert_type(words << 16, F32)
    high = lax.bitcast_convert_type(words & jnp.int32(-65536), F32)
    return low, high


def _peer_tc_body(e_cur, e_nxt, g_ref, x_ref, uv_hbm, o_ref, buf, sem, *, n_steps):
    i = pl.program_id(0)
    slot = i % 2

    def issue(e_ref, dst):
        for t in range(PEER_TB):
            def one(k, carry, t=t):
                row = e_ref[t, k] >> 1
                pltpu.make_async_copy(uv_hbm.at[row], buf.at[dst, :, t * PEER_SEL + k], sem.at[dst]).start()
                return carry
            lax.fori_loop(0, PEER_SEL, one, 0, unroll=8)

    @pl.when(i == 0)
    def _():
        issue(e_cur, slot)

    @pl.when(i + 1 < n_steps)
    def _():
        issue(e_nxt, 1 - slot)

    pltpu.make_async_copy(buf.at[slot], buf.at[slot], sem.at[slot]).wait()

    gates = g_ref[...].T
    for t in range(PEER_TB):
        rows = slice(t * PEER_SEL, (t + 1) * PEER_SEL)
        acc = None
        for j in range(WORD_SUB):
            lo_cols = slice(j * ROW_LANE, (j + 1) * ROW_LANE)
            hi_cols = slice(HALF_WORDS + j * ROW_LANE, HALF_WORDS + (j + 1) * ROW_LANE)
            low, high = _bf16_pair(buf[slot, j, rows, :])
            term = low * x_ref[t:t + 1, lo_cols] + high * x_ref[t:t + 1, hi_cols]
            acc = term if acc is None else acc + term
        w = gates[:, t:t + 1] * _gelu(jnp.sum(acc, axis=-1, keepdims=True))
        for j in range(WORD_SUB):
            lo_cols = slice(j * ROW_LANE, (j + 1) * ROW_LANE)
            hi_cols = slice(HALF_WORDS + j * ROW_LANE, HALF_WORDS + (j + 1) * ROW_LANE)
            low, high = _bf16_pair(buf[slot, WORD_SUB + j, rows, :])
            o_ref[t:t + 1, lo_cols] = jnp.sum(w * low, axis=0, keepdims=True)
            o_ref[t:t + 1, hi_cols] = jnp.sum(w * high, axis=0, keepdims=True)


def _peer_experts_tc(e, g, xm, uvb):
    t, d = xm.shape
    assert d == 2 * HALF_WORDS and t % PEER_TB == 0
    n_steps = t // PEER_TB
    last = n_steps - 1
    n_rows = PEER_TB * PEER_SEL
    return pl.pallas_call(
        functools.partial(_peer_tc_body, n_steps=n_steps),
        grid=(n_steps,),
        in_specs=[pl.BlockSpec((PEER_TB, PEER_SEL), lambda i: (i, 0), memory_space=pltpu.SMEM),
                  pl.BlockSpec((PEER_TB, PEER_SEL), lambda i: (jnp.minimum(i + 1, last), 0), memory_space=pltpu.SMEM),
                  pl.BlockSpec((PEER_TB, PEER_SEL), lambda i: (i, 0)),
                  pl.BlockSpec((PEER_TB, d), lambda i: (i, 0)),
                  pl.BlockSpec(memory_space=pl.ANY)],
        out_specs=pl.BlockSpec((PEER_TB, d), lambda i: (i, 0)),
        out_shape=jax.ShapeDtypeStruct((t, d), F32),
        scratch_shapes=[pltpu.VMEM((2, 2 * WORD_SUB, n_rows, ROW_LANE), jnp.int32),
                        pltpu.SemaphoreType.DMA((2,))],
        compiler_params=_cparams("arbitrary"),
        name="peer_experts_tc",
    )(e, e, g, xm, uvb)


def _peer_select(q, sub_keys):
    return tuple(_peer_topk(q, sub_keys))


def _final_body(*refs, delta_starts):
    h_ref, *delta_refs, gate_ref, g_ref, o_ref = refs
    h = h_ref[...] + gate_ref[0] * _pick_piece(delta_refs, delta_starts)
    o_ref[...] = _rms(h) * g_ref[...]


def _final(h, delta, gate, gain, *, rows_per_group, tm=256):
    t, d = h.shape
    row = pl.BlockSpec((tm, d), lambda i: (i, 0))
    piece_specs, delta_starts = _piece_specs(delta, tm, d)
    return pl.pallas_call(
        functools.partial(_final_body, delta_starts=delta_starts),
        grid=(t // tm,),
        in_specs=[row, *piece_specs, pl.BlockSpec((1, 1, d), lambda i: ((i * tm) // rows_per_group, 0, 0)),
                  pl.BlockSpec((1, d), lambda i: (0, 0))],
        out_specs=row,
        out_shape=jax.ShapeDtypeStruct((t, d), F32),
        compiler_params=_cparams("parallel"),
        name="final_norm",
    )(h, *delta, gate, gain.reshape(1, d))


def kernel(x, c, ctx, c_ctx, w_mod, b_mod, attn_w_in, attn_w_out, lambda_q1, lambda_k1, lambda_q2, lambda_k2,
           subln_g, na_rpb, chunk_w_in, chunk_b_in, chunk_ln_g, chunk_ln_b, chunk_w_s, chunk_b_s, chunk_w_out,
           peer_w_query, peer_sub_keys, peer_u, peer_v, final_norm_g):
    b, n, d = x.shape
    n_ctx = ctx.shape[1]

    cond = jnp.concatenate([c, c_ctx[None], jnp.zeros((-(b + 1) % 8, d), F32)], axis=0)
    mods = [_adaln(cond, w_mod[i], b_mod[i]) for i in range(2)]
    m0, m1 = mods

    w_in = attn_w_in[0].astype(BF16)
    w_out = attn_w_out[0].astype(BF16)
    w_q0 = peer_w_query[0].astype(BF16)
    w_q1 = peer_w_query[1].astype(BF16)
    w_gin = chunk_w_in[0].astype(BF16)
    w_gout = chunk_w_out[0].astype(BF16)
    rope = _rope_tables(n, 512)
    bias_tabs = _na_bias_tables(na_rpb[0], n // GRID_W)
    lam_init = 0.8 - 0.6 * math.exp(-0.3 * 0)
    lam = (jnp.exp(jnp.sum(lambda_q1[0] * lambda_k1[0])) - jnp.exp(jnp.sum(lambda_q2[0] * lambda_k2[0]))
           + lam_init).astype(F32)
    ka = A_HEADS * 2 * A_QK_DIM

    def cx(m, k):
        return m[b:b + 1, k * d:(k + 1) * d].reshape(1, 1, d)

    sizes = PIPE_BATCHES if sum(PIPE_BATCHES) == b else (b,)
    starts = [sum(sizes[:ci]) for ci in range(len(sizes))]
    n_chunks = len(sizes)

    def rows(a, ci):
        return a[starts[ci]:starts[ci] + sizes[ci]]

    def lat(ci, m, k):
        return rows(m, ci)[:, k * d:(k + 1) * d].reshape(sizes[ci], 1, d)

    def stage_a(ci, x2):
        bc = sizes[ci]
        zc = _mod_mm(rows(ctx, ci).reshape(bc * n_ctx, d), cx(m0, 0), cx(m0, 1), w_in, rows_per_group=n_ctx,
                     tm=256, name="attn_in_ctx")[0]
        z = _mod_mm(x2, lat(ci, m0, 0), lat(ci, m0, 1), w_in, rows_per_group=n, tm=256, rope=rope,
                    rope_chunks=(0, 1), scale_chunks=(0, 3), name="attn_in")[0]
        z3 = z.reshape(bc, n, -1)
        zc3 = zc.reshape(bc, n_ctx, -1)
        oa = _diff_attn(z3, zc3, lam, subln_g[0], lam_init)
        ob = _na_attn(z3, zc3, bias_tabs)
        h1 = _out_proj(x2, lat(ci, m0, 2), oa.reshape(bc * n, ka), ob.reshape(bc * n, -1), w_out[:ka], w_out[ka:],
                       rows_per_group=n)
        q0, xm0 = _mod_mm(h1, lat(ci, m0, 3), lat(ci, m0, 4), w_q0, rows_per_group=n, tm=256, emit_xm=True,
                          name="peer_q0")
        return h1, xm0, _peer_select(q0, peer_sub_keys[0])

    def stage_b(ci, h1, p0):
        zg, h1b = _mod_mm(h1, lat(ci, m1, 0), lat(ci, m1, 1), w_gin, rows_per_group=n, tm=256, delta=p0,
                          gate=lat(ci, m0, 5), bias=chunk_b_in[0], act="gelu", name="gmlp_in")
        h2 = _gmlp_tail(zg, h1b, lat(ci, m1, 2), chunk_ln_g[0], chunk_ln_b[0], chunk_w_s[0], chunk_b_s[0], w_gout,
                        rows_per_group=n)
        q1, xm1 = _mod_mm(h2, lat(ci, m1, 3), lat(ci, m1, 4), w_q1, rows_per_group=n, tm=256, emit_xm=True,
                          name="peer_q1")
        return h2, xm1, _peer_select(q1, peer_sub_keys[1])

    tables = [(_expert_rows(peer_u[i], peer_v[i]), _expert_rows_bf16(peer_u[i], peer_v[i])) for i in range(2)]

    def experts(sel, xm, tabs):
        e, g = sel
        t = xm.shape[0]
        k = int(t * PEER_TC_SHARE) // SC_TOKEN_STEP * SC_TOKEN_STEP
        if k == 0 or (t - k) % SC_TOKEN_STEP:
            return (_peer_experts(e, g, xm, tabs[0]),)
        p_sc = _peer_experts(e[k:], g[k:], xm[k:], tabs[0])
        p_tc = _peer_experts_tc(e[:k], g[:k], xm[:k], tabs[1])
        return p_tc, p_sc

    h1s, p0s = [], []
    prev_sel = None
    for ci in range(n_chunks):
        x2 = rows(x, ci).reshape(sizes[ci] * n, d)
        if prev_sel is not None:
            x2, prev_sel, tables = lax.optimization_barrier((x2, prev_sel, tables))
            p0s.append(experts(prev_sel, xm_prev, tables[0]))
        h1, xm_prev, prev_sel = stage_a(ci, x2)
        h1s.append(h1)
    p0s.append(experts(prev_sel, xm_prev, tables[0]))

    h2s, p1s = [], []
    prev_sel = None
    for ci in range(n_chunks):
        p0 = p0s[ci]
        if prev_sel is not None:
            p0, prev_sel = lax.optimization_barrier((p0, prev_sel))
            p1s.append(experts(prev_sel, xm_prev, tables[1]))
        h2, xm_prev, prev_sel = stage_b(ci, h1s[ci], p0)
        h2s.append(h2)
    p1s.append(experts(prev_sel, xm_prev, tables[1]))

    outs = [_final(h2s[ci], p1s[ci], lat(ci, m1, 5), final_norm_g, rows_per_group=n).reshape(sizes[ci], n, d)
            for ci in range(n_chunks)]
    return outs[0] if n_chunks == 1 else jnp.concatenate(outs, axis=0)
```

```python
import functools
import math

import numpy as np
import jax
import jax.numpy as jnp
from jax import lax
from jax.experimental import pallas as pl
from jax.experimental.pallas import tpu as pltpu
from jax.experimental.pallas import tpu_sc as plsc

F32 = jnp.float32
BF16 = jnp.bfloat16

EPS = 1e-6
GRID_W = 64
NA_ROWS = 8
NA_COLS = 16
ROPE_BASE = 10000.0
A_HEADS = 4
A_QK_DIM = 64
B_HEADS = 8
B_HEAD_DIM = 64
PEER_HEADS = 8
N_KEYS = 128
PEER_TOPK = 16
PEER_SEL = PEER_HEADS * PEER_TOPK
CHUNK = 128
C_GROUPS = 8
MASK_VALUE = -1e30

VMEM_LIMIT = 56 * 1024 * 1024
PIPE_BATCHES = (1, 2, 2, 3)
PEER_TC_SHARE = 21 / 64


def _cparams(*sem):
    return pltpu.CompilerParams(dimension_semantics=sem, vmem_limit_bytes=VMEM_LIMIT)


def _rms(x):
    return x * lax.rsqrt(jnp.mean(x * x, axis=-1, keepdims=True) + EPS)


def _gelu(x):
    return jax.nn.gelu(x)


def _adaln_body(c_ref, w_ref, b_ref, o_ref):
    cnd = c_ref[...]
    a = (cnd * jax.nn.sigmoid(cnd)).astype(BF16)
    o_ref[...] = jnp.dot(a, w_ref[...].astype(BF16), preferred_element_type=F32) + b_ref[...]


def _adaln(cond, w, b):
    m, d = cond.shape
    n = w.shape[1]
    tn = 1024
    return pl.pallas_call(
        _adaln_body,
        grid=(n // tn,),
        in_specs=[pl.BlockSpec((m, d), lambda j: (0, 0)),
                  pl.BlockSpec((d, tn), lambda j: (0, j)),
                  pl.BlockSpec((1, tn), lambda j: (0, j))],
        out_specs=pl.BlockSpec((m, tn), lambda j: (0, j)),
        out_shape=jax.ShapeDtypeStruct((m, n), F32),
        compiler_params=_cparams("parallel"),
        name="adaln",
    )(cond, w, b.reshape(1, n))


def _piece_specs(pieces, tm, d):
    specs, starts, b0 = [], [], 0
    for p in pieces:
        nb = p.shape[0] // tm
        assert nb * tm == p.shape[0]
        specs.append(pl.BlockSpec((tm, d), lambda i, b0=b0, nb=nb: (jnp.clip(i - b0, 0, nb - 1), 0)))
        starts.append(b0)
        b0 += nb
    return specs, tuple(starts)


def _pick_piece(refs, starts):
    i = pl.program_id(0)
    val = refs[0][...]
    for ref, start in zip(refs[1:], starts[1:]):
        val = jnp.where(i >= start, ref[...], val)
    return val


def _mod_mm_body(*refs, delta_starts, has_bias, rope_chunks, scale_chunks, act, emit_xm, tn, n_out):
    it = iter(refs)
    h_ref = next(it)
    has_delta = bool(delta_starts)
    delta_refs = gate_ref = None
    if has_delta:
        delta_refs = [next(it) for _ in delta_starts]
        gate_ref = next(it)
    shift_ref, scale_ref, w_ref = next(it), next(it), next(it)
    b_ref = next(it) if has_bias else None
    cos_ref = sin_ref = None
    if rope_chunks:
        cos_ref, sin_ref = next(it), next(it)
    y_ref = next(it)
    hn_ref = next(it) if has_delta else None
    xm_ref = next(it) if emit_xm else None

    h = h_ref[...]
    if has_delta:
        h = h + gate_ref[0] * _pick_piece(delta_refs, delta_starts)
        hn_ref[...] = h
    xm = _rms(h) * (1.0 + scale_ref[0]) + shift_ref[0]
    if emit_xm:
        xm_ref[...] = xm
    xb = xm.astype(BF16)
    for c in range(n_out // tn):
        cols = slice(c * tn, (c + 1) * tn)
        y = jnp.dot(xb, w_ref[:, cols], preferred_element_type=F32)
        if has_bias:
            y = y + b_ref[:, cols]
        if c in rope_chunks:
            lane = lax.broadcasted_iota(jnp.int32, y.shape, 1)
            partner = jnp.where((lane & 16) == 0, pltpu.roll(y, tn - 16, 1), pltpu.roll(y, 16, 1))
            y = y * cos_ref[...] + partner * sin_ref[...]
        if c in scale_chunks:
            y = y * 0.125
        if act == "gelu":
            y = _gelu(y)
        y_ref[:, cols] = y


def _mod_mm(h, shift, scale, w, *, rows_per_group, tm, delta=None, gate=None, bias=None, rope=None,
            rope_chunks=(), scale_chunks=(), act="none", emit_xm=False, name="mod_mm"):
    t, d = h.shape
    n_out = w.shape[1]
    tn = 512
    groups = shift.shape[0]
    assert t % tm == 0 and rows_per_group % tm == 0 and n_out % tn == 0

    def gidx(i):
        return ((i * tm) // rows_per_group if groups > 1 else 0, 0, 0)

    row_spec = pl.BlockSpec((tm, d), lambda i: (i, 0))
    mod_spec = pl.BlockSpec((1, 1, d), gidx)
    args, specs = [h], [row_spec]
    delta_starts = ()
    if delta is not None:
        piece_specs, delta_starts = _piece_specs(delta, tm, d)
        args += [*delta, gate]
        specs += [*piece_specs, mod_spec]
    args += [shift, scale, w]
    specs += [mod_spec, mod_spec, pl.BlockSpec((d, n_out), lambda i: (0, 0))]
    if bias is not None:
        args.append(bias.reshape(1, n_out))
        specs.append(pl.BlockSpec((1, n_out), lambda i: (0, 0)))
    if rope_chunks:
        cos, sin = rope
        pos_blocks = cos.shape[0] // tm
        args += [cos, sin]
        specs += [pl.BlockSpec((tm, tn), lambda i: (i % pos_blocks, 0))] * 2
    out_shape = [jax.ShapeDtypeStruct((t, n_out), F32)]
    out_specs = [pl.BlockSpec((tm, n_out), lambda i: (i, 0))]
    if delta is not None:
        out_shape.append(jax.ShapeDtypeStruct((t, d), F32))
        out_specs.append(row_spec)
    if emit_xm:
        out_shape.append(jax.ShapeDtypeStruct((t, d), F32))
        out_specs.append(row_spec)
    body = functools.partial(_mod_mm_body, delta_starts=delta_starts, has_bias=bias is not None,
                             rope_chunks=tuple(rope_chunks), scale_chunks=tuple(scale_chunks), act=act,
                             emit_xm=emit_xm, tn=tn, n_out=n_out)
    return pl.pallas_call(body, grid=(t // tm,), in_specs=specs, out_specs=out_specs, out_shape=out_shape,
                          compiler_params=_cparams("parallel"), name=name)(*args)


def _rope_tables(n_tokens, width):
    axis_dim = A_QK_DIM // 2
    inv_freq = 1.0 / (ROPE_BASE ** (jnp.arange(0, axis_dim, 2, dtype=F32) / axis_dim))
    t = jnp.arange(n_tokens)
    row = (t // GRID_W).astype(F32)
    col = (t % GRID_W).astype(F32)
    ang_r = row[:, None] * inv_freq
    ang_c = col[:, None] * inv_freq
    ang = jnp.concatenate([ang_r, ang_r, ang_c, ang_c], axis=1)
    sign = jnp.tile(jnp.concatenate([-jnp.ones(16, F32), jnp.ones(16, F32)]), 2)
    reps = width // A_QK_DIM
    return jnp.tile(jnp.cos(ang), (1, reps)), jnp.tile(jnp.sin(ang) * sign, (1, reps))


def _diff_attn_body(lam_ref, q_ref, k_ref, v_ref, kc_ref, vc_ref, g_ref, o_ref, *, out_scale):
    lam = lam_ref[0, 0]
    q = q_ref[0]
    k = k_ref[0].astype(BF16)
    kc = kc_ref[0].astype(BF16)
    lane = lax.broadcasted_iota(jnp.int32, q.shape, 1)
    nt = (((1,), (1,)), ((), ()))
    parts = []
    for m in range(2):
        keep = (lane < A_QK_DIM) if m == 0 else (lane >= A_QK_DIM)
        qm = jnp.where(keep, q, 0.0).astype(BF16)
        s_l = lax.dot_general(qm, k, nt, preferred_element_type=F32)
        s_c = lax.dot_general(qm, kc, nt, preferred_element_type=F32)
        mx = jnp.maximum(jnp.max(s_l, axis=-1, keepdims=True), jnp.max(s_c, axis=-1, keepdims=True))
        e_l = jnp.exp(s_l - mx)
        e_c = jnp.exp(s_c - mx)
        inv = 1.0 / (jnp.sum(e_l, axis=-1, keepdims=True) + jnp.sum(e_c, axis=-1, keepdims=True))
        parts.append((e_l, e_c, inv))
    (e1l, e1c, inv1), (e2l, e2c, inv2) = parts
    w2 = lam * inv2
    a_l = (e1l * inv1 - e2l * w2).astype(BF16)
    a_c = (e1c * inv1 - e2c * w2).astype(BF16)
    o = (jnp.dot(a_l, v_ref[0].astype(BF16), preferred_element_type=F32)
         + jnp.dot(a_c, vc_ref[0].astype(BF16), preferred_element_type=F32))
    o_ref[0] = _rms(o) * g_ref[...] * out_scale


def _diff_attn(z, zc, lam, subln_g, lam_init, *, tq=256):
    b, n, _ = z.shape
    c = zc.shape[1]
    kcol, vcol = A_HEADS, 2 * A_HEADS
    return pl.pallas_call(
        functools.partial(_diff_attn_body, out_scale=1.0 - lam_init),
        grid=(b, A_HEADS, n // tq),
        in_specs=[pl.BlockSpec((1, 1), lambda bi, h, i: (0, 0), memory_space=pltpu.SMEM),
                  pl.BlockSpec((1, tq, 128), lambda bi, h, i: (bi, i, h)),
                  pl.BlockSpec((1, n, 128), lambda bi, h, i: (bi, 0, kcol + h)),
                  pl.BlockSpec((1, n, 128), lambda bi, h, i: (bi, 0, vcol + h)),
                  pl.BlockSpec((1, c, 128), lambda bi, h, i: (bi, 0, kcol + h)),
                  pl.BlockSpec((1, c, 128), lambda bi, h, i: (bi, 0, vcol + h)),
                  pl.BlockSpec((1, 128), lambda bi, h, i: (0, 0))],
        out_specs=pl.BlockSpec((1, tq, 128), lambda bi, h, i: (bi, i, h)),
        out_shape=jax.ShapeDtypeStruct((b, n, A_HEADS * 128), F32),
        compiler_params=_cparams("parallel", "parallel", "arbitrary"),
        name="diff_attn",
    )(lam.reshape(1, 1), z, z, z, zc, zc, subln_g.reshape(1, 128))


NA_QROWS = 8
NA_KROWS = 16
NA_SEG = 4 * GRID_W


def _na_window_start(j, rows):
    return jnp.clip(NA_QROWS * j - NA_ROWS // 2, 0, rows - NA_KROWS)


def _na_bias_tables(rpb, rows):
    n_blocks = rows // NA_QROWS
    h = rpb.shape[0]
    ic = np.clip(np.arange(GRID_W)[None, :] - np.arange(GRID_W)[:, None] + (NA_COLS - 1), 0, 2 * NA_COLS - 2)
    toep = jnp.pad(rpb[:, :, ic], ((0, 0), (NA_KROWS, NA_KROWS), (0, 0), (0, 0)))
    tabs, oks = [], []
    for j in (0, 1, n_blocks - 1):
        ks = int(np.clip(NA_QROWS * j - NA_ROWS // 2, 0, rows - NA_KROWS))
        slabs = []
        for rq in range(NA_QROWS):
            dr0 = ks - (NA_QROWS * j + rq) + (NA_ROWS - 1) + NA_KROWS
            slab = toep[:, dr0:dr0 + NA_KROWS]
            slabs.append(slab.transpose(0, 2, 1, 3).reshape(h, GRID_W, NA_KROWS * GRID_W))
        tabs.append(jnp.concatenate(slabs, axis=1))
        r = (NA_QROWS * j + np.arange(NA_QROWS))[:, None, None, None]
        cq = np.arange(GRID_W)[None, :, None, None]
        kr = (ks + np.arange(NA_KROWS))[None, None, :, None]
        kc = np.arange(GRID_W)[None, None, None, :]
        r0 = np.clip(r - NA_ROWS // 2, 0, rows - NA_ROWS)
        c0 = np.clip(cq - NA_COLS // 2, 0, GRID_W - NA_COLS)
        ok = (kr >= r0) & (kr < r0 + NA_ROWS) & (kc >= c0) & (kc < c0 + NA_COLS)
        oks.append(ok.reshape(NA_QROWS * GRID_W, NA_KROWS * GRID_W))
    return jnp.where(np.stack(oks)[None], jnp.stack(tabs, axis=1), MASK_VALUE)


def _na_body(q_ref, k0, k1, k2, k3, v0, v1, v2, v3, kc_ref, vc_ref, bias_ref, o_ref):
    q = q_ref[0]
    lane = lax.broadcasted_iota(jnp.int32, q.shape, 1)
    nt = (((1,), (1,)), ((), ()))
    ks = [r[0].astype(BF16) for r in (k0, k1, k2, k3)]
    vs = [r[0].astype(BF16) for r in (v0, v1, v2, v3)]
    kc = kc_ref[0].astype(BF16)
    vc = vc_ref[0].astype(BF16)
    outs = []
    for hh in range(2):
        keep = (lane < B_HEAD_DIM) if hh == 0 else (lane >= B_HEAD_DIM)
        qh = jnp.where(keep, q, 0.0).astype(BF16)
        segs = [lax.dot_general(qh, ks[i], nt, preferred_element_type=F32)
                + bias_ref[hh, 0, :, i * NA_SEG:(i + 1) * NA_SEG] for i in range(4)]
        s_c = lax.dot_general(qh, kc, nt, preferred_element_type=F32)
        mx = jnp.max(s_c, axis=-1, keepdims=True)
        for s in segs:
            mx = jnp.maximum(mx, jnp.max(s, axis=-1, keepdims=True))
        e_c = jnp.exp(s_c - mx)
        den = jnp.sum(e_c, axis=-1, keepdims=True)
        es = []
        for s in segs:
            e = jnp.exp(s - mx)
            den = den + jnp.sum(e, axis=-1, keepdims=True)
            es.append(e)
        inv = 1.0 / den
        o = jnp.dot((e_c * inv).astype(BF16), vc, preferred_element_type=F32)
        for e, v in zip(es, vs):
            o = o + jnp.dot((e * inv).astype(BF16), v, preferred_element_type=F32)
        outs.append(o)
    o_ref[0] = jnp.where(lane < B_HEAD_DIM, outs[0], outs[1])


def _na_attn(z, zc, bias_tabs):
    b, n, _ = z.shape
    c = zc.shape[1]
    rows = n // GRID_W
    n_blocks = rows // NA_QROWS
    tq = NA_QROWS * GRID_W
    qcol, kcol, vcol = 12, 16, 20
    seg_rows = NA_SEG // GRID_W

    def kv_spec(col0, i):
        return pl.BlockSpec((1, NA_SEG, 128),
                            lambda hp, j, bi: (bi, _na_window_start(j, rows) // seg_rows + i, col0 + hp))

    def variant(j):
        return jnp.where(j == 0, 0, jnp.where(j == n_blocks - 1, 2, 1))

    in_specs = [pl.BlockSpec((1, tq, 128), lambda hp, j, bi: (bi, j, qcol + hp))]
    in_specs += [kv_spec(kcol, i) for i in range(4)]
    in_specs += [kv_spec(vcol, i) for i in range(4)]
    in_specs += [pl.BlockSpec((1, c, 128), lambda hp, j, bi: (bi, 0, kcol + hp)),
                 pl.BlockSpec((1, c, 128), lambda hp, j, bi: (bi, 0, vcol + hp)),
                 pl.BlockSpec((2, 1, tq, NA_KROWS * GRID_W), lambda hp, j, bi: (hp, variant(j), 0, 0))]
    return pl.pallas_call(
        _na_body,
        grid=(B_HEADS // 2, n_blocks, b),
        in_specs=in_specs,
        out_specs=pl.BlockSpec((1, tq, 128), lambda hp, j, bi: (bi, j, hp)),
        out_shape=jax.ShapeDtypeStruct((b, n, B_HEADS * B_HEAD_DIM), F32),
        compiler_params=_cparams("parallel", "parallel", "arbitrary"),
        name="na_attn",
    )(z, z, z, z, z, z, z, z, z, zc, zc, bias_tabs)


def _out_proj_body(h_ref, gate_ref, xa_ref, xb_ref, wa_ref, wb_ref, o_ref):
    y = (jnp.dot(xa_ref[...].astype(BF16), wa_ref[...], preferred_element_type=F32)
         + jnp.dot(xb_ref[...].astype(BF16), wb_ref[...], preferred_element_type=F32))
    o_ref[...] = h_ref[...] + gate_ref[0] * y


def _out_proj(h, gate, xa, xb, wa, wb, *, rows_per_group, tm=512):
    t, d = h.shape
    ka, kb = xa.shape[1], xb.shape[1]
    return pl.pallas_call(
        _out_proj_body,
        grid=(t // tm,),
        in_specs=[pl.BlockSpec((tm, d), lambda i: (i, 0)),
                  pl.BlockSpec((1, 1, d), lambda i: ((i * tm) // rows_per_group, 0, 0)),
                  pl.BlockSpec((tm, ka), lambda i: (i, 0)),
                  pl.BlockSpec((tm, kb), lambda i: (i, 0)),
                  pl.BlockSpec((ka, d), lambda i: (0, 0)),
                  pl.BlockSpec((kb, d), lambda i: (0, 0))],
        out_specs=pl.BlockSpec((tm, d), lambda i: (i, 0)),
        out_shape=jax.ShapeDtypeStruct((t, d), F32),
        compiler_params=_cparams("parallel"),
        name="out_proj",
    )(h, gate, xa, xb, wa, wb)


def _gmlp_tail_body(z_ref, h_ref, gate_ref, lng_ref, lnb_ref, ws_ref, bs_ref, wo_ref, o_ref, gated_ref, *, tm, width):
    gw = width // C_GROUPS
    for ci in range(tm // CHUNK):
        rows = slice(ci * CHUNK, (ci + 1) * CHUNK)
        v = z_ref[rows, width:]
        mu = jnp.mean(v, axis=-1, keepdims=True)
        var = jnp.mean(jnp.square(v - mu), axis=-1, keepdims=True)
        vn = ((v - mu) * lax.rsqrt(var + EPS)) * lng_ref[...] + lnb_ref[...]
        vb = vn.astype(BF16)
        for g in range(C_GROUPS):
            cols = slice(g * gw, (g + 1) * gw)
            sv = jnp.dot(ws_ref[g], vb[:, cols], preferred_element_type=F32) + bs_ref[g]
            gated_ref[rows, cols] = (z_ref[rows, cols] * sv).astype(BF16)
    y = jnp.dot(gated_ref[...], wo_ref[...], preferred_element_type=F32)
    o_ref[...] = h_ref[...] + gate_ref[0] * y


def _gmlp_tail(z, h, gate, ln_g, ln_b, w_s, b_s, w_out, *, rows_per_group, tm=512):
    t, d = h.shape
    width = z.shape[1] // 2
    gw = width // C_GROUPS
    bsb = jnp.broadcast_to(b_s[:, :, None], (C_GROUPS, CHUNK, gw))
    return pl.pallas_call(
        functools.partial(_gmlp_tail_body, tm=tm, width=width),
        grid=(t // tm,),
        in_specs=[pl.BlockSpec((tm, 2 * width), lambda i: (i, 0)),
                  pl.BlockSpec((tm, d), lambda i: (i, 0)),
                  pl.BlockSpec((1, 1, d), lambda i: ((i * tm) // rows_per_group, 0, 0)),
                  pl.BlockSpec((1, width), lambda i: (0, 0)),
                  pl.BlockSpec((1, width), lambda i: (0, 0)),
                  pl.BlockSpec((C_GROUPS, CHUNK, CHUNK), lambda i: (0, 0, 0)),
                  pl.BlockSpec((C_GROUPS, CHUNK, gw), lambda i: (0, 0, 0)),
                  pl.BlockSpec((width, d), lambda i: (0, 0))],
        out_specs=pl.BlockSpec((tm, d), lambda i: (i, 0)),
        out_shape=jax.ShapeDtypeStruct((t, d), F32),
        scratch_shapes=[pltpu.VMEM((tm, width), BF16)],
        compiler_params=_cparams("parallel"),
        name="gmlp_tail",
    )(z, h, gate, ln_g.reshape(1, width), ln_b.reshape(1, width), w_s.astype(BF16), bsb, w_out)


def _topk_rows(scores, payloads, k):
    n_rows = scores[0].shape[0]
    riota = lax.broadcasted_iota(jnp.int32, scores[0].shape, 0)
    scores = list(scores)
    vals = [[] for _ in scores]
    pays = [[] for _ in scores]
    for _ in range(k):
        for c, (s, payload) in enumerate(zip(scores, payloads)):
            m = jnp.max(s, axis=0, keepdims=True)
            first = jnp.min(jnp.where(s == m, riota, n_rows), axis=0, keepdims=True)
            sel = riota == first
            vals[c].append(m)
            pays[c].append(first if payload is None
                           else jnp.max(jnp.where(sel, payload, -1), axis=0, keepdims=True))
            scores[c] = jnp.where(sel, -jnp.inf, s)
    return [(jnp.concatenate(v, axis=0), jnp.concatenate(p, axis=0)) for v, p in zip(vals, pays)]


def _pair_candidates(x1, x2, combine, fill):
    k = PEER_TOPK
    sub = lax.broadcasted_iota(jnp.int32, (8, x1.shape[1]), 0)
    blocks = [combine(x1[0:1], x2)]
    for a in range(1, 8):
        blocks.append(jnp.where(sub < k // (a + 1), combine(x1[a:a + 1], x2[0:8]), fill))
    blocks.append(combine(x1[8:k], x2[0:1]))
    return jnp.concatenate(blocks, axis=0)


def _peer_topk_body(q_ref, keys_ref, e_ref, g_ref, *, tt):
    nt = (((1,), (1,)), ((), ()))
    es, gs = [], []
    for h0 in range(0, PEER_HEADS, 2):
        cands, experts = [], []
        for h in (h0, h0 + 1):
            scores = []
            for p in range(2):
                hp = 2 * h + p
                qhp = q_ref[:, hp * 128:(hp + 1) * 128].astype(BF16)
                scores.append(lax.dot_general(keys_ref[hp], qhp, nt, preferred_element_type=F32))
            (s1, i1), (s2, i2) = _topk_rows(scores, [None, None], PEER_TOPK)
            cands.append(_pair_candidates(s1, s2, lambda x, y: x + y, -jnp.inf))
            experts.append(_pair_candidates(i1, i2, lambda x, y: 2 * (x * N_KEYS + y), -1))
        for sc, e in _topk_rows(cands, experts, PEER_TOPK):
            ex = jnp.exp(sc - sc[0:1])
            es.append(e)
            gs.append(ex / jnp.sum(ex, axis=0, keepdims=True))
    e_ref[...] = jnp.concatenate(es, axis=0).T
    g_ref[...] = jnp.concatenate(gs, axis=0).T


def _peer_topk(q, sub_keys, *, tt=128):
    t, qw = q.shape
    keys = sub_keys.reshape(2 * PEER_HEADS, N_KEYS, qw // (2 * PEER_HEADS)).astype(BF16)
    return pl.pallas_call(
        functools.partial(_peer_topk_body, tt=tt),
        grid=(t // tt,),
        in_specs=[pl.BlockSpec((tt, qw), lambda i: (i, 0)),
                  pl.BlockSpec(keys.shape, lambda i: (0, 0, 0))],
        out_specs=[pl.BlockSpec((tt, PEER_SEL), lambda i: (i, 0)),
                   pl.BlockSpec((tt, PEER_SEL), lambda i: (i, 0))],
        out_shape=[jax.ShapeDtypeStruct((t, PEER_SEL), jnp.int32),
                   jax.ShapeDtypeStruct((t, PEER_SEL), F32)],
        compiler_params=_cparams("parallel"),
        name="peer_topk",
    )(q, keys)


SC_CORES = 2
SC_SUBCORES = 16
SC_LANES = 16
SC_TILES = SC_CORES * SC_SUBCORES
ROW_SUB = 8
ROW_LANE = 128
VECS_PER_SUB = ROW_LANE // SC_LANES
D_BLOCKS = 4
VECS_PER_BLOCK = ROW_SUB * VECS_PER_SUB // D_BLOCKS


def _sc_gelu(s):
    z = 0.7978845608028654 * (s + 0.044715 * (s * s * s))
    tanh_z = 1.0 - 2.0 / (jnp.exp(2.0 * z) + 1.0)
    return 0.5 * s * (1.0 + tanh_z)


def _vec_slot(jb, k):
    v = jb * VECS_PER_BLOCK + k
    return v // VECS_PER_SUB, pl.ds((v % VECS_PER_SUB) * SC_LANES, SC_LANES)


SC_TOKB = 8
PART_SUMS = 4
SC_RING = 3
SC_TOKEN_STEP = SC_TILES * SC_TOKB


def _peer_sc_body(e_hbm, g_hbm, x_hbm, uv_hbm, o_hbm, idx_v, idx1_v, g_v, x_v, out_v, acc_v, w_v, ubuf, vbuf,
                  sem_u, sem_v, sem_o, *, batches_per_tile):
    wid = lax.axis_index("s") * SC_CORES + lax.axis_index("c")
    base = wid * batches_per_tile
    lane = lax.iota(jnp.int32, SC_LANES)
    n_groups = PEER_SEL // SC_LANES
    gpb = SC_TOKB * n_groups

    def sel16(ref, gi):
        return ref.at[gi // n_groups, pl.ds((gi % n_groups) * SC_LANES, SC_LANES)]

    def gather(gi, slot):
        return (pltpu.make_async_copy(uv_hbm.at[sel16(idx_v, gi)], ubuf.at[slot], sem_u.at[slot]),
                pltpu.make_async_copy(uv_hbm.at[sel16(idx1_v, gi)], vbuf.at[slot], sem_v.at[slot]))

    def start(gi, slot):
        for cp in gather(gi, slot):
            cp.start()

    def out_copy(batch, par):
        return pltpu.make_async_copy(out_v.at[par], o_hbm.at[batch], sem_o.at[par])

    def u_phase(slot, t, gi):
        for jb in range(D_BLOCKS):
            xs = []
            for k in range(VECS_PER_BLOCK):
                sub, ls = _vec_slot(jb, k)
                xs.append(x_v[sub, t, ls])

            @plsc.parallel_loop(0, SC_LANES, unroll=2)
            def _(r, jb=jb, xs=xs):
                parts = []
                per = VECS_PER_BLOCK // PART_SUMS
                for q in range(PART_SUMS):
                    p = xs[q * per] * ubuf[(slot, r) + _vec_slot(jb, q * per)]
                    for k in range(q * per + 1, (q + 1) * per):
                        p = p + xs[k] * ubuf[(slot, r) + _vec_slot(jb, k)]
                    parts.append(p)
                a = (parts[0] + parts[1]) + (parts[2] + parts[3])
                if jb > 0:
                    a = a + acc_v[r, :]
                acc_v[r, :] = a
        cols = [plsc.load_gather(acc_v, [lane, jnp.full((SC_LANES,), l, jnp.int32)]) for l in range(SC_LANES)]
        while len(cols) > 1:
            cols = [cols[i] + cols[i + 1] for i in range(0, len(cols), 2)]
        w_v[...] = sel16(g_v, gi)[...] * _sc_gelu(cols[0])

    def v_phase(slot, par, t):
        for jb in range(D_BLOCKS):
            slots = [_vec_slot(jb, k) for k in range(VECS_PER_BLOCK)]
            init = tuple(jnp.zeros((SC_LANES,), F32) for _ in slots)

            def add_row(r, os, jb=jb):
                wb = plsc.load_gather(w_v, [jnp.full((SC_LANES,), r, jnp.int32)])
                return tuple(os[k] + wb * vbuf[(slot, r) + _vec_slot(jb, k)] for k in range(VECS_PER_BLOCK))

            os = plsc.parallel_loop(0, SC_LANES, unroll=2, carry=init)(add_row)
            for (sub, ls), o in zip(slots, os):
                plsc.addupdate(out_v.at[par, sub, t, ls], o)

    @pl.loop(0, batches_per_tile)
    def _(bi):
        batch = base + bi
        par = bi % 2
        pltpu.sync_copy(e_hbm.at[pl.ds(batch * SC_TOKB, SC_TOKB)], idx_v)
        pltpu.sync_copy(g_hbm.at[pl.ds(batch * SC_TOKB, SC_TOKB)], g_v)
        pltpu.sync_copy(x_hbm.at[batch], x_v)
        for tt in range(SC_TOKB):
            for c in range(n_groups):
                cols = pl.ds(c * SC_LANES, SC_LANES)
                idx1_v[tt, cols] = idx_v[tt, cols] + 1
        for gi0 in range(SC_RING - 1):
            start(gi0, gi0)

        @pl.when(bi >= 2)
        def _():
            out_copy(batch, par).wait()

        def step(gi, slot):
            gi = jnp.asarray(gi, jnp.int32)
            t = gi // n_groups

            @pl.when(gi + SC_RING - 1 < gpb)
            def _():
                start(gi + SC_RING - 1, (slot + SC_RING - 1) % SC_RING)

            @pl.when(gi % n_groups == 0)
            def _():
                for j in range(ROW_SUB):
                    for l in range(VECS_PER_SUB):
                        out_v[par, j, t, pl.ds(l * SC_LANES, SC_LANES)] = jnp.zeros((SC_LANES,), F32)

            cu, cv = gather(gi, slot)
            cu.wait()
            u_phase(slot, t, gi)
            cv.wait()
            v_phase(slot, par, t)

        @pl.loop(0, gpb // SC_RING)
        def _(it):
            for slot in range(SC_RING):
                step(it * SC_RING + slot, slot)

        for gi in range(gpb - gpb % SC_RING, gpb):
            step(gi, gi % SC_RING)

        out_copy(batch, par).start()

    for par in range(2):
        out_copy(base, par).wait()


def _expert_rows(u, v):
    n_e = u.shape[0]
    rows = jnp.stack([u.reshape(n_e, ROW_SUB, ROW_LANE), v.reshape(n_e, ROW_SUB, ROW_LANE)], axis=1)
    return rows.reshape(2 * n_e, ROW_SUB, ROW_LANE)


def _peer_experts(e, g, xm, uv):
    t, d = xm.shape
    assert d == ROW_SUB * ROW_LANE and t % SC_TOKEN_STEP == 0 and t >= 2 * SC_TOKEN_STEP and e.shape == (t, PEER_SEL)
    n_batches = t // SC_TOKB

    def tiled(a):
        return a.reshape(n_batches, SC_TOKB, ROW_SUB, ROW_LANE).transpose(0, 2, 1, 3)

    call = pl.kernel(
        functools.partial(_peer_sc_body, batches_per_tile=n_batches // SC_TILES),
        out_type=jax.ShapeDtypeStruct((n_batches, ROW_SUB, SC_TOKB, ROW_LANE), F32),
        mesh=plsc.VectorSubcoreMesh(core_axis_name="c", subcore_axis_name="s"),
        scratch_types=[pltpu.VMEM((SC_TOKB, PEER_SEL), jnp.int32),
                       pltpu.VMEM((SC_TOKB, PEER_SEL), jnp.int32),
                       pltpu.VMEM((SC_TOKB, PEER_SEL), F32),
                       pltpu.VMEM((ROW_SUB, SC_TOKB, ROW_LANE), F32),
                       pltpu.VMEM((2, ROW_SUB, SC_TOKB, ROW_LANE), F32),
                       pltpu.VMEM((SC_LANES, SC_LANES), F32),
                       pltpu.VMEM((SC_LANES,), F32),
                       pltpu.VMEM((SC_RING, SC_LANES, ROW_SUB, ROW_LANE), F32),
                       pltpu.VMEM((SC_RING, SC_LANES, ROW_SUB, ROW_LANE), F32),
                       pltpu.SemaphoreType.DMA((SC_RING,)),
                       pltpu.SemaphoreType.DMA((SC_RING,)),
                       pltpu.SemaphoreType.DMA((2,))],
        compiler_params=pltpu.CompilerParams(needs_layout_passes=False),
        name="peer_experts_sc",
    )
    out = call(e, g, tiled(xm), uv)
    return out.transpose(0, 2, 1, 3).reshape(t, d)


PEER_TB = 8
HALF_WORDS = ROW_SUB * ROW_LANE // 2
WORD_SUB = HALF_WORDS // ROW_LANE


def _expert_rows_bf16(u, v):
    def pack(a):
        bits = lax.bitcast_convert_type(a.astype(BF16), jnp.uint16).astype(jnp.uint32)
        return lax.bitcast_convert_type((bits[:, HALF_WORDS:] << 16) | bits[:, :HALF_WORDS], jnp.int32)

    n_e = u.shape[0]
    return jnp.stack([pack(u), pack(v)], axis=1).reshape(n_e, 2 * WORD_SUB, ROW_LANE)


def _bf16_pair(words):
    low = lax.bitcast_convert_type(words << 16, F32)
    high = lax.bitcast_convert_type(words & jnp.int32(-65536), F32)
    return low, high


def _peer_tc_body(e_cur, e_nxt, g_ref, x_ref, uv_hbm, o_ref, buf, sem, *, n_steps):
    i = pl.program_id(0)
    slot = i % 2

    def issue(e_ref, dst):
        for t in range(PEER_TB):
            def one(k, carry, t=t):
                row = e_ref[t, k] >> 1
                pltpu.make_async_copy(uv_hbm.at[row], buf.at[dst, :, t * PEER_SEL + k], sem.at[dst]).start()
                return carry
            lax.fori_loop(0, PEER_SEL, one, 0, unroll=8)

    @pl.when(i == 0)
    def _():
        issue(e_cur, slot)

    @pl.when(i + 1 < n_steps)
    def _():
        issue(e_nxt, 1 - slot)

    pltpu.make_async_copy(buf.at[slot], buf.at[slot], sem.at[slot]).wait()

    gates = g_ref[...].T
    for t in range(PEER_TB):
        rows = slice(t * PEER_SEL, (t + 1) * PEER_SEL)
        acc = None
        for j in range(WORD_SUB):
            lo_cols = slice(j * ROW_LANE, (j + 1) * ROW_LANE)
            hi_cols = slice(HALF_WORDS + j * ROW_LANE, HALF_WORDS + (j + 1) * ROW_LANE)
            low, high = _bf16_pair(buf[slot, j, rows, :])
            term = low * x_ref[t:t + 1, lo_cols] + high * x_ref[t:t + 1, hi_cols]
            acc = term if acc is None else acc + term
        w = gates[:, t:t + 1] * _gelu(jnp.sum(acc, axis=-1, keepdims=True))
        for j in range(WORD_SUB):
            lo_cols = slice(j * ROW_LANE, (j + 1) * ROW_LANE)
            hi_cols = slice(HALF_WORDS + j * ROW_LANE, HALF_WORDS + (j + 1) * ROW_LANE)
            low, high = _bf16_pair(buf[slot, WORD_SUB + j, rows, :])
            o_ref[t:t + 1, lo_cols] = jnp.sum(w * low, axis=0, keepdims=True)
            o_ref[t:t + 1, hi_cols] = jnp.sum(w * high, axis=0, keepdims=True)


def _peer_experts_tc(e, g, xm, uvb):
    t, d = xm.shape
    assert d == 2 * HALF_WORDS and t % PEER_TB == 0
    n_steps = t // PEER_TB
    last = n_steps - 1
    n_rows = PEER_TB * PEER_SEL
    return pl.pallas_call(
        functools.partial(_peer_tc_body, n_steps=n_steps),
        grid=(n_steps,),
        in_specs=[pl.BlockSpec((PEER_TB, PEER_SEL), lambda i: (i, 0), memory_space=pltpu.SMEM),
                  pl.BlockSpec((PEER_TB, PEER_SEL), lambda i: (jnp.minimum(i + 1, last), 0), memory_space=pltpu.SMEM),
                  pl.BlockSpec((PEER_TB, PEER_SEL), lambda i: (i, 0)),
                  pl.BlockSpec((PEER_TB, d), lambda i: (i, 0)),
                  pl.BlockSpec(memory_space=pl.ANY)],
        out_specs=pl.BlockSpec((PEER_TB, d), lambda i: (i, 0)),
        out_shape=jax.ShapeDtypeStruct((t, d), F32),
        scratch_shapes=[pltpu.VMEM((2, 2 * WORD_SUB, n_rows, ROW_LANE), jnp.int32),
                        pltpu.SemaphoreType.DMA((2,))],
        compiler_params=_cparams("arbitrary"),
        name="peer_experts_tc",
    )(e, e, g, xm, uvb)


def _peer_select(q, sub_keys):
    return tuple(_peer_topk(q, sub_keys))


def _final_body(*refs, delta_starts):
    h_ref, *delta_refs, gate_ref, g_ref, o_ref = refs
    h = h_ref[...] + gate_ref[0] * _pick_piece(delta_refs, delta_starts)
    o_ref[...] = _rms(h) * g_ref[...]


def _final(h, delta, gate, gain, *, rows_per_group, tm=256):
    t, d = h.shape
    row = pl.BlockSpec((tm, d), lambda i: (i, 0))
    piece_specs, delta_starts = _piece_specs(delta, tm, d)
    return pl.pallas_call(
        functools.partial(_final_body, delta_starts=delta_starts),
        grid=(t // tm,),
        in_specs=[row, *piece_specs, pl.BlockSpec((1, 1, d), lambda i: ((i * tm) // rows_per_group, 0, 0)),
                  pl.BlockSpec((1, d), lambda i: (0, 0))],
        out_specs=row,
        out_shape=jax.ShapeDtypeStruct((t, d), F32),
        compiler_params=_cparams("parallel"),
        name="final_norm",
    )(h, *delta, gate, gain.reshape(1, d))


def kernel(x, c, ctx, c_ctx, w_mod, b_mod, attn_w_in, attn_w_out, lambda_q1, lambda_k1, lambda_q2, lambda_k2,
           subln_g, na_rpb, chunk_w_in, chunk_b_in, chunk_ln_g, chunk_ln_b, chunk_w_s, chunk_b_s, chunk_w_out,
           peer_w_query, peer_sub_keys, peer_u, peer_v, final_norm_g):
    b, n, d = x.shape
    n_ctx = ctx.shape[1]

    cond = jnp.concatenate([c, c_ctx[None], jnp.zeros((-(b + 1) % 8, d), F32)], axis=0)
    mods = [_adaln(cond, w_mod[i], b_mod[i]) for i in range(2)]
    m0, m1 = mods

    w_in = attn_w_in[0].astype(BF16)
    w_out = attn_w_out[0].astype(BF16)
    w_q0 = peer_w_query[0].astype(BF16)
    w_q1 = peer_w_query[1].astype(BF16)
    w_gin = chunk_w_in[0].astype(BF16)
    w_gout = chunk_w_out[0].astype(BF16)
    rope = _rope_tables(n, 512)
    bias_tabs = _na_bias_tables(na_rpb[0], n // GRID_W)
    lam_init = 0.8 - 0.6 * math.exp(-0.3 * 0)
    lam = (jnp.exp(jnp.sum(lambda_q1[0] * lambda_k1[0])) - jnp.exp(jnp.sum(lambda_q2[0] * lambda_k2[0]))
           + lam_init).astype(F32)
    ka = A_HEADS * 2 * A_QK_DIM

    def cx(m, k):
        return m[b:b + 1, k * d:(k + 1) * d].reshape(1, 1, d)

    sizes = PIPE_BATCHES if sum(PIPE_BATCHES) == b else (b,)
    starts = [sum(sizes[:ci]) for ci in range(len(sizes))]
    n_chunks = len(sizes)

    def rows(a, ci):
        return a[starts[ci]:starts[ci] + sizes[ci]]

    def lat(ci, m, k):
        return rows(m, ci)[:, k * d:(k + 1) * d].reshape(sizes[ci], 1, d)

    def stage_a(ci, x2):
        bc = sizes[ci]
        zc = _mod_mm(rows(ctx, ci).reshape(bc * n_ctx, d), cx(m0, 0), cx(m0, 1), w_in, rows_per_group=n_ctx,
                     tm=256, name="attn_in_ctx")[0]
        z = _mod_mm(x2, lat(ci, m0, 0), lat(ci, m0, 1), w_in, rows_per_group=n, tm=256, rope=rope,
                    rope_chunks=(0, 1), scale_chunks=(0, 3), name="attn_in")[0]
        z3 = z.reshape(bc, n, -1)
        zc3 = zc.reshape(bc, n_ctx, -1)
        oa = _diff_attn(z3, zc3, lam, subln_g[0], lam_init)
        ob = _na_attn(z3, zc3, bias_tabs)
        h1 = _out_proj(x2, lat(ci, m0, 2), oa.reshape(bc * n, ka), ob.reshape(bc * n, -1), w_out[:ka], w_out[ka:],
                       rows_per_group=n)
        q0, xm0 = _mod_mm(h1, lat(ci, m0, 3), lat(ci, m0, 4), w_q0, rows_per_group=n, tm=256, emit_xm=True,
                          name="peer_q0")
        return h1, xm0, _peer_select(q0, peer_sub_keys[0])

    def stage_b(ci, h1, p0):
        zg, h1b = _mod_mm(h1, lat(ci, m1, 0), lat(ci, m1, 1), w_gin, rows_per_group=n, tm=256, delta=p0,
                          gate=lat(ci, m0, 5), bias=chunk_b_in[0], act="gelu", name="gmlp_in")
        h2 = _gmlp_tail(zg, h1b, lat(ci, m1, 2), chunk_ln_g[0], chunk_ln_b[0], chunk_w_s[0], chunk_b_s[0], w_gout,
                        rows_per_group=n)
        q1, xm1 = _mod_mm(h2, lat(ci, m1, 3), lat(ci, m1, 4), w_q1, rows_per_group=n, tm=256, emit_xm=True,
                          name="peer_q1")
        return h2, xm1, _peer_select(q1, peer_sub_keys[1])

    tables = [(_expert_rows(peer_u[i], peer_v[i]), _expert_rows_bf16(peer_u[i], peer_v[i])) for i in range(2)]

    def experts(sel, xm, tabs):
        e, g = sel
        t = xm.shape[0]
        k = int(t * PEER_TC_SHARE) // SC_TOKEN_STEP * SC_TOKEN_STEP
        if k == 0 or (t - k) % SC_TOKEN_STEP:
            return (_peer_experts(e, g, xm, tabs[0]),)
        p_sc = _peer_experts(e[k:], g[k:], xm[k:], tabs[0])
        p_tc = _peer_experts_tc(e[:k], g[:k], xm[:k], tabs[1])
        return p_tc, p_sc

    h1s, p0s = [], []
    prev_sel = None
    for ci in range(n_chunks):
        x2 = rows(x, ci).reshape(sizes[ci] * n, d)
        if prev_sel is not None:
            x2, prev_sel, tables = lax.optimization_barrier((x2, prev_sel, tables))
            p0s.append(experts(prev_sel, xm_prev, tables[0]))
        h1, xm_prev, prev_sel = stage_a(ci, x2)
        h1s.append(h1)
    p0s.append(experts(prev_sel, xm_prev, tables[0]))

    h2s, p1s = [], []
    prev_sel = None
    for ci in range(n_chunks):
        p0 = p0s[ci]
        if prev_sel is not None:
            p0, prev_sel = lax.optimization_barrier((p0, prev_sel))
            p1s.append(experts(prev_sel, xm_prev, tables[1]))
        h2, xm_prev, prev_sel = stage_b(ci, h1s[ci], p0)
        h2s.append(h2)
    p1s.append(experts(prev_sel, xm_prev, tables[1]))

    outs = [_final(h2s[ci], p1s[ci], lat(ci, m1, 5), final_norm_g, rows_per_group=n).reshape(sizes[ci], n, d)
            for ci in range(n_chunks)]
    return outs[0] if n_chunks == 1 else jnp.concatenate(outs, axis=0)
```

```python
import functools
import math

import numpy as np
import jax
import jax.numpy as jnp
from jax import lax
from jax.experimental import pallas as pl
from jax.experimental.pallas import tpu as pltpu
from jax.experimental.pallas import tpu_sc as plsc

F32 = jnp.float32
BF16 = jnp.bfloat16

EPS = 1e-6
GRID_W = 64
NA_ROWS = 8
NA_COLS = 16
ROPE_BASE = 10000.0
A_HEADS = 4
A_QK_DIM = 64
B_HEADS = 8
B_HEAD_DIM = 64
PEER_HEADS = 8
N_KEYS = 128
PEER_TOPK = 16
PEER_SEL = PEER_HEADS * PEER_TOPK
CHUNK = 128
C_GROUPS = 8
MASK_VALUE = -1e30

VMEM_LIMIT = 56 * 1024 * 1024
PIPE_BATCHES = (1, 2, 2, 3)
PEER_TC_SHARE = 21 / 64


def _cparams(*sem):
    return pltpu.CompilerParams(dimension_semantics=sem, vmem_limit_bytes=VMEM_LIMIT)


def _rms(x):
    return x * lax.rsqrt(jnp.mean(x * x, axis=-1, keepdims=True) + EPS)


def _gelu(x):
    return jax.nn.gelu(x)


def _adaln_body(c_ref, w_ref, b_ref, o_ref):
    cnd = c_ref[...]
    a = (cnd * jax.nn.sigmoid(cnd)).astype(BF16)
    o_ref[...] = jnp.dot(a, w_ref[...].astype(BF16), preferred_element_type=F32) + b_ref[...]


def _adaln(cond, w, b):
    m, d = cond.shape
    n = w.shape[1]
    tn = 1024
    return pl.pallas_call(
        _adaln_body,
        grid=(n // tn,),
        in_specs=[pl.BlockSpec((m, d), lambda j: (0, 0)),
                  pl.BlockSpec((d, tn), lambda j: (0, j)),
                  pl.BlockSpec((1, tn), lambda j: (0, j))],
        out_specs=pl.BlockSpec((m, tn), lambda j: (0, j)),
        out_shape=jax.ShapeDtypeStruct((m, n), F32),
        compiler_params=_cparams("parallel"),
        name="adaln",
    )(cond, w, b.reshape(1, n))


def _piece_specs(pieces, tm, d):
    specs, starts, b0 = [], [], 0
    for p in pieces:
        nb = p.shape[0] // tm
        assert nb * tm == p.shape[0]
        specs.append(pl.BlockSpec((tm, d), lambda i, b0=b0, nb=nb: (jnp.clip(i - b0, 0, nb - 1), 0)))
        starts.append(b0)
        b0 += nb
    return specs, tuple(starts)


def _pick_piece(refs, starts):
    i = pl.program_id(0)
    val = refs[0][...]
    for ref, start in zip(refs[1:], starts[1:]):
        val = jnp.where(i >= start, ref[...], val)
    return val


def _mod_mm_body(*refs, delta_starts, has_bias, rope_chunks, scale_chunks, act, emit_xm, tn, n_out):
    it = iter(refs)
    h_ref = next(it)
    has_delta = bool(delta_starts)
    delta_refs = gate_ref = None
    if has_delta:
        delta_refs = [next(it) for _ in delta_starts]
        gate_ref = next(it)
    shift_ref, scale_ref, w_ref = next(it), next(it), next(it)
    b_ref = next(it) if has_bias else None
    cos_ref = sin_ref = None
    if rope_chunks:
        cos_ref, sin_ref = next(it), next(it)
    y_ref = next(it)
    hn_ref = next(it) if has_delta else None
    xm_ref = next(it) if emit_xm else None

    h = h_ref[...]
    if has_delta:
        h = h + gate_ref[0] * _pick_piece(delta_refs, delta_starts)
        hn_ref[...] = h
    xm = _rms(h) * (1.0 + scale_ref[0]) + shift_ref[0]
    if emit_xm:
        xm_ref[...] = xm
    xb = xm.astype(BF16)
    for c in range(n_out // tn):
        cols = slice(c * tn, (c + 1) * tn)
        y = jnp.dot(xb, w_ref[:, cols], preferred_element_type=F32)
        if has_bias:
            y = y + b_ref[:, cols]
        if c in rope_chunks:
            lane = lax.broadcasted_iota(jnp.int32, y.shape, 1)
            partner = jnp.where((lane & 16) == 0, pltpu.roll(y, tn - 16, 1), pltpu.roll(y, 16, 1))
            y = y * cos_ref[...] + partner * sin_ref[...]
        if c in scale_chunks:
            y = y * 0.125
        if act == "gelu":
            y = _gelu(y)
        y_ref[:, cols] = y


def _mod_mm(h, shift, scale, w, *, rows_per_group, tm, delta=None, gate=None, bias=None, rope=None,
            rope_chunks=(), scale_chunks=(), act="none", emit_xm=False, name="mod_mm"):
    t, d = h.shape
    n_out = w.shape[1]
    tn = 512
    groups = shift.shape[0]
    assert t % tm == 0 and rows_per_group % tm == 0 and n_out % tn == 0

    def gidx(i):
        return ((i * tm) // rows_per_group if groups > 1 else 0, 0, 0)

    row_spec = pl.BlockSpec((tm, d), lambda i: (i, 0))
    mod_spec = pl.BlockSpec((1, 1, d), gidx)
    args, specs = [h], [row_spec]
    delta_starts = ()
    if delta is not None:
        piece_specs, delta_starts = _piece_specs(delta, tm, d)
        args += [*delta, gate]
        specs += [*piece_specs, mod_spec]
    args += [shift, scale, w]
    specs += [mod_spec, mod_spec, pl.BlockSpec((d, n_out), lambda i: (0, 0))]
    if bias is not None:
        args.append(bias.reshape(1, n_out))
        specs.append(pl.BlockSpec((1, n_out), lambda i: (0, 0)))
    if rope_chunks:
        cos, sin = rope
        pos_blocks = cos.shape[0] // tm
        args += [cos, sin]
        specs += [pl.BlockSpec((tm, tn), lambda i: (i % pos_blocks, 0))] * 2
    out_shape = [jax.ShapeDtypeStruct((t, n_out), F32)]
    out_specs = [pl.BlockSpec((tm, n_out), lambda i: (i, 0))]
    if delta is not None:
        out_shape.append(jax.ShapeDtypeStruct((t, d), F32))
        out_specs.append(row_spec)
    if emit_xm:
        out_shape.append(jax.ShapeDtypeStruct((t, d), F32))
        out_specs.append(row_spec)
    body = functools.partial(_mod_mm_body, delta_starts=delta_starts, has_bias=bias is not None,
                             rope_chunks=tuple(rope_chunks), scale_chunks=tuple(scale_chunks), act=act,
                             emit_xm=emit_xm, tn=tn, n_out=n_out)
    return pl.pallas_call(body, grid=(t // tm,), in_specs=specs, out_specs=out_specs, out_shape=out_shape,
                          compiler_params=_cparams("parallel"), name=name)(*args)


def _rope_tables(n_tokens, width):
    axis_dim = A_QK_DIM // 2
    inv_freq = 1.0 / (ROPE_BASE ** (jnp.arange(0, axis_dim, 2, dtype=F32) / axis_dim))
    t = jnp.arange(n_tokens)
    row = (t // GRID_W).astype(F32)
    col = (t % GRID_W).astype(F32)
    ang_r = row[:, None] * inv_freq
    ang_c = col[:, None] * inv_freq
    ang = jnp.concatenate([ang_r, ang_r, ang_c, ang_c], axis=1)
    sign = jnp.tile(jnp.concatenate([-jnp.ones(16, F32), jnp.ones(16, F32)]), 2)
    reps = width // A_QK_DIM
    return jnp.tile(jnp.cos(ang), (1, reps)), jnp.tile(jnp.sin(ang) * sign, (1, reps))


def _diff_attn_body(lam_ref, q_ref, k_ref, v_ref, kc_ref, vc_ref, g_ref, o_ref, *, out_scale):
    lam = lam_ref[0, 0]
    q = q_ref[0]
    k = k_ref[0].astype(BF16)
    kc = kc_ref[0].astype(BF16)
    lane = lax.broadcasted_iota(jnp.int32, q.shape, 1)
    nt = (((1,), (1,)), ((), ()))
    parts = []
    for m in range(2):
        keep = (lane < A_QK_DIM) if m == 0 else (lane >= A_QK_DIM)
        qm = jnp.where(keep, q, 0.0).astype(BF16)
        s_l = lax.dot_general(qm, k, nt, preferred_element_type=F32)
        s_c = lax.dot_general(qm, kc, nt, preferred_element_type=F32)
        mx = jnp.maximum(jnp.max(s_l, axis=-1, keepdims=True), jnp.max(s_c, axis=-1, keepdims=True))
        e_l = jnp.exp(s_l - mx)
        e_c = jnp.exp(s_c - mx)
        inv = 1.0 / (jnp.sum(e_l, axis=-1, keepdims=True) + jnp.sum(e_c, axis=-1, keepdims=True))
        parts.append((e_l, e_c, inv))
    (e1l, e1c, inv1), (e2l, e2c, inv2) = parts
    w2 = lam * inv2
    a_l = (e1l * inv1 - e2l * w2).astype(BF16)
    a_c = (e1c * inv1 - e2c * w2).astype(BF16)
    o = (jnp.dot(a_l, v_ref[0].astype(BF16), preferred_element_type=F32)
         + jnp.dot(a_c, vc_ref[0].astype(BF16), preferred_element_type=F32))
    o_ref[0] = _rms(o) * g_ref[...] * out_scale


def _diff_attn(z, zc, lam, subln_g, lam_init, *, tq=256):
    b, n, _ = z.shape
    c = zc.shape[1]
    kcol, vcol = A_HEADS, 2 * A_HEADS
    return pl.pallas_call(
        functools.partial(_diff_attn_body, out_scale=1.0 - lam_init),
        grid=(b, A_HEADS, n // tq),
        in_specs=[pl.BlockSpec((1, 1), lambda bi, h, i: (0, 0), memory_space=pltpu.SMEM),
                  pl.BlockSpec((1, tq, 128), lambda bi, h, i: (bi, i, h)),
                  pl.BlockSpec((1, n, 128), lambda bi, h, i: (bi, 0, kcol + h)),
                  pl.BlockSpec((1, n, 128), lambda bi, h, i: (bi, 0, vcol + h)),
                  pl.BlockSpec((1, c, 128), lambda bi, h, i: (bi, 0, kcol + h)),
                  pl.BlockSpec((1, c, 128), lambda bi, h, i: (bi, 0, vcol + h)),
                  pl.BlockSpec((1, 128), lambda bi, h, i: (0, 0))],
        out_specs=pl.BlockSpec((1, tq, 128), lambda bi, h, i: (bi, i, h)),
        out_shape=jax.ShapeDtypeStruct((b, n, A_HEADS * 128), F32),
        compiler_params=_cparams("parallel", "parallel", "arbitrary"),
        name="diff_attn",
    )(lam.reshape(1, 1), z, z, z, zc, zc, subln_g.reshape(1, 128))


NA_QROWS = 8
NA_KROWS = 16
NA_SEG = 4 * GRID_W


def _na_window_start(j, rows):
    return jnp.clip(NA_QROWS * j - NA_ROWS // 2, 0, rows - NA_KROWS)


def _na_bias_tables(rpb, rows):
    n_blocks = rows // NA_QROWS
    h = rpb.shape[0]
    ic = np.clip(np.arange(GRID_W)[None, :] - np.arange(GRID_W)[:, None] + (NA_COLS - 1), 0, 2 * NA_COLS - 2)
    toep = jnp.pad(rpb[:, :, ic], ((0, 0), (NA_KROWS, NA_KROWS), (0, 0), (0, 0)))
    tabs, oks = [], []
    for j in (0, 1, n_blocks - 1):
        ks = int(np.clip(NA_QROWS * j - NA_ROWS // 2, 0, rows - NA_KROWS))
        slabs = []
        for rq in range(NA_QROWS):
            dr0 = ks - (NA_QROWS * j + rq) + (NA_ROWS - 1) + NA_KROWS
            slab = toep[:, dr0:dr0 + NA_KROWS]
            slabs.append(slab.transpose(0, 2, 1, 3).reshape(h, GRID_W, NA_KROWS * GRID_W))
        tabs.append(jnp.concatenate(slabs, axis=1))
        r = (NA_QROWS * j + np.arange(NA_QROWS))[:, None, None, None]
        cq = np.arange(GRID_W)[None, :, None, None]
        kr = (ks + np.arange(NA_KROWS))[None, None, :, None]
        kc = np.arange(GRID_W)[None, None, None, :]
        r0 = np.clip(r - NA_ROWS // 2, 0, rows - NA_ROWS)
        c0 = np.clip(cq - NA_COLS // 2, 0, GRID_W - NA_COLS)
        ok = (kr >= r0) & (kr < r0 + NA_ROWS) & (kc >= c0) & (kc < c0 + NA_COLS)
        oks.append(ok.reshape(NA_QROWS * GRID_W, NA_KROWS * GRID_W))
    return jnp.where(np.stack(oks)[None], jnp.stack(tabs, axis=1), MASK_VALUE)


def _na_body(q_ref, k0, k1, k2, k3, v0, v1, v2, v3, kc_ref, vc_ref, bias_ref, o_ref):
    q = q_ref[0]
    lane = lax.broadcasted_iota(jnp.int32, q.shape, 1)
    nt = (((1,), (1,)), ((), ()))
    ks = [r[0].astype(BF16) for r in (k0, k1, k2, k3)]
    vs = [r[0].astype(BF16) for r in (v0, v1, v2, v3)]
    kc = kc_ref[0].astype(BF16)
    vc = vc_ref[0].astype(BF16)
    outs = []
    for hh in range(2):
        keep = (lane < B_HEAD_DIM) if hh == 0 else (lane >= B_HEAD_DIM)
        qh = jnp.where(keep, q, 0.0).astype(BF16)
        segs = [lax.dot_general(qh, ks[i], nt, preferred_element_type=F32)
                + bias_ref[hh, 0, :, i * NA_SEG:(i + 1) * NA_SEG] for i in range(4)]
        s_c = lax.dot_general(qh, kc, nt, preferred_element_type=F32)
        mx = jnp.max(s_c, axis=-1, keepdims=True)
        for s in segs:
            mx = jnp.maximum(mx, jnp.max(s, axis=-1, keepdims=True))
        e_c = jnp.exp(s_c - mx)
        den = jnp.sum(e_c, axis=-1, keepdims=True)
        es = []
        for s in segs:
            e = jnp.exp(s - mx)
            den = den + jnp.sum(e, axis=-1, keepdims=True)
            es.append(e)
        inv = 1.0 / den
        o = jnp.dot((e_c * inv).astype(BF16), vc, preferred_element_type=F32)
        for e, v in zip(es, vs):
            o = o + jnp.dot((e * inv).astype(BF16), v, preferred_element_type=F32)
        outs.append(o)
    o_ref[0] = jnp.where(lane < B_HEAD_DIM, outs[0], outs[1])


def _na_attn(z, zc, bias_tabs):
    b, n, _ = z.shape
    c = zc.shape[1]
    rows = n // GRID_W
    n_blocks = rows // NA_QROWS
    tq = NA_QROWS * GRID_W
    qcol, kcol, vcol = 12, 16, 20
    seg_rows = NA_SEG // GRID_W

    def kv_spec(col0, i):
        return pl.BlockSpec((1, NA_SEG, 128),
                            lambda hp, j, bi: (bi, _na_window_start(j, rows) // seg_rows + i, col0 + hp))

    def variant(j):
        return jnp.where(j == 0, 0, jnp.where(j == n_blocks - 1, 2, 1))

    in_specs = [pl.BlockSpec((1, tq, 128), lambda hp, j, bi: (bi, j, qcol + hp))]
    in_specs += [kv_spec(kcol, i) for i in range(4)]
    in_specs += [kv_spec(vcol, i) for i in range(4)]
    in_specs += [pl.BlockSpec((1, c, 128), lambda hp, j, bi: (bi, 0, kcol + hp)),
                 pl.BlockSpec((1, c, 128), lambda hp, j, bi: (bi, 0, vcol + hp)),
                 pl.BlockSpec((2, 1, tq, NA_KROWS * GRID_W), lambda hp, j, bi: (hp, variant(j), 0, 0))]
    return pl.pallas_call(
        _na_body,
        grid=(B_HEADS // 2, n_blocks, b),
        in_specs=in_specs,
        out_specs=pl.BlockSpec((1, tq, 128), lambda hp, j, bi: (bi, j, hp)),
        out_shape=jax.ShapeDtypeStruct((b, n, B_HEADS * B_HEAD_DIM), F32),
        compiler_params=_cparams("parallel", "parallel", "arbitrary"),
        name="na_attn",
    )(z, z, z, z, z, z, z, z, z, zc, zc, bias_tabs)


def _out_proj_body(h_ref, gate_ref, xa_ref, xb_ref, wa_ref, wb_ref, o_ref):
    y = (jnp.dot(xa_ref[...].astype(BF16), wa_ref[...], preferred_element_type=F32)
         + jnp.dot(xb_ref[...].astype(BF16), wb_ref[...], preferred_element_type=F32))
    o_ref[...] = h_ref[...] + gate_ref[0] * y


def _out_proj(h, gate, xa, xb, wa, wb, *, rows_per_group, tm=512):
    t, d = h.shape
    ka, kb = xa.shape[1], xb.shape[1]
    return pl.pallas_call(
        _out_proj_body,
        grid=(t // tm,),
        in_specs=[pl.BlockSpec((tm, d), lambda i: (i, 0)),
                  pl.BlockSpec((1, 1, d), lambda i: ((i * tm) // rows_per_group, 0, 0)),
                  pl.BlockSpec((tm, ka), lambda i: (i, 0)),
                  pl.BlockSpec((tm, kb), lambda i: (i, 0)),
                  pl.BlockSpec((ka, d), lambda i: (0, 0)),
                  pl.BlockSpec((kb, d), lambda i: (0, 0))],
        out_specs=pl.BlockSpec((tm, d), lambda i: (i, 0)),
        out_shape=jax.ShapeDtypeStruct((t, d), F32),
        compiler_params=_cparams("parallel"),
        name="out_proj",
    )(h, gate, xa, xb, wa, wb)


def _gmlp_tail_body(z_ref, h_ref, gate_ref, lng_ref, lnb_ref, ws_ref, bs_ref, wo_ref, o_ref, gated_ref, *, tm, width):
    gw = width // C_GROUPS
    for ci in range(tm // CHUNK):
        rows = slice(ci * CHUNK, (ci + 1) * CHUNK)
        v = z_ref[rows, width:]
        mu = jnp.mean(v, axis=-1, keepdims=True)
        var = jnp.mean(jnp.square(v - mu), axis=-1, keepdims=True)
        vn = ((v - mu) * lax.rsqrt(var + EPS)) * lng_ref[...] + lnb_ref[...]
        vb = vn.astype(BF16)
        for g in range(C_GROUPS):
            cols = slice(g * gw, (g + 1) * gw)
            sv = jnp.dot(ws_ref[g], vb[:, cols], preferred_element_type=F32) + bs_ref[g]
            gated_ref[rows, cols] = (z_ref[rows, cols] * sv).astype(BF16)
    y = jnp.dot(gated_ref[...], wo_ref[...], preferred_element_type=F32)
    o_ref[...] = h_ref[...] + gate_ref[0] * y


def _gmlp_tail(z, h, gate, ln_g, ln_b, w_s, b_s, w_out, *, rows_per_group, tm=512):
    t, d = h.shape
    width = z.shape[1] // 2
    gw = width // C_GROUPS
    bsb = jnp.broadcast_to(b_s[:, :, None], (C_GROUPS, CHUNK, gw))
    return pl.pallas_call(
        functools.partial(_gmlp_tail_body, tm=tm, width=width),
        grid=(t // tm,),
        in_specs=[pl.BlockSpec((tm, 2 * width), lambda i: (i, 0)),
                  pl.BlockSpec((tm, d), lambda i: (i, 0)),
                  pl.BlockSpec((1, 1, d), lambda i: ((i * tm) // rows_per_group, 0, 0)),
                  pl.BlockSpec((1, width), lambda i: (0, 0)),
                  pl.BlockSpec((1, width), lambda i: (0, 0)),
                  pl.BlockSpec((C_GROUPS, CHUNK, CHUNK), lambda i: (0, 0, 0)),
                  pl.BlockSpec((C_GROUPS, CHUNK, gw), lambda i: (0, 0, 0)),
                  pl.BlockSpec((width, d), lambda i: (0, 0))],
        out_specs=pl.BlockSpec((tm, d), lambda i: (i, 0)),
        out_shape=jax.ShapeDtypeStruct((t, d), F32),
        scratch_shapes=[pltpu.VMEM((tm, width), BF16)],
        compiler_params=_cparams("parallel"),
        name="gmlp_tail",
    )(z, h, gate, ln_g.reshape(1, width), ln_b.reshape(1, width), w_s.astype(BF16), bsb, w_out)


def _topk_rows(scores, payloads, k):
    n_rows = scores[0].shape[0]
    riota = lax.broadcasted_iota(jnp.int32, scores[0].shape, 0)
    scores = list(scores)
    vals = [[] for _ in scores]
    pays = [[] for _ in scores]
    for _ in range(k):
        for c, (s, payload) in enumerate(zip(scores, payloads)):
            m = jnp.max(s, axis=0, keepdims=True)
            first = jnp.min(jnp.where(s == m, riota, n_rows), axis=0, keepdims=True)
            sel = riota == first
            vals[c].append(m)
            pays[c].append(first if payload is None
                           else jnp.max(jnp.where(sel, payload, -1), axis=0, keepdims=True))
            scores[c] = jnp.where(sel, -jnp.inf, s)
    return [(jnp.concatenate(v, axis=0), jnp.concatenate(p, axis=0)) for v, p in zip(vals, pays)]


def _pair_candidates(x1, x2, combine, fill):
    k = PEER_TOPK
    sub = lax.broadcasted_iota(jnp.int32, (8, x1.shape[1]), 0)
    blocks = [combine(x1[0:1], x2)]
    for a in range(1, 8):
        blocks.append(jnp.where(sub < k // (a + 1), combine(x1[a:a + 1], x2[0:8]), fill))
    blocks.append(combine(x1[8:k], x2[0:1]))
    return jnp.concatenate(blocks, axis=0)


def _peer_topk_body(q_ref, keys_ref, e_ref, g_ref, *, tt):
    nt = (((1,), (1,)), ((), ()))
    es, gs = [], []
    for h0 in range(0, PEER_HEADS, 2):
        cands, experts = [], []
        for h in (h0, h0 + 1):
            scores = []
            for p in range(2):
                hp = 2 * h + p
                qhp = q_ref[:, hp * 128:(hp + 1) * 128].astype(BF16)
                scores.append(lax.dot_general(keys_ref[hp], qhp, nt, preferred_element_type=F32))
            (s1, i1), (s2, i2) = _topk_rows(scores, [None, None], PEER_TOPK)
            cands.append(_pair_candidates(s1, s2, lambda x, y: x + y, -jnp.inf))
            experts.append(_pair_candidates(i1, i2, lambda x, y: 2 * (x * N_KEYS + y), -1))
        for sc, e in _topk_rows(cands, experts, PEER_TOPK):
            ex = jnp.exp(sc - sc[0:1])
            es.append(e)
            gs.append(ex / jnp.sum(ex, axis=0, keepdims=True))
    e_ref[...] = jnp.concatenate(es, axis=0).T
    g_ref[...] = jnp.concatenate(gs, axis=0).T


def _peer_topk(q, sub_keys, *, tt=128):
    t, qw = q.shape
    keys = sub_keys.reshape(2 * PEER_HEADS, N_KEYS, qw // (2 * PEER_HEADS)).astype(BF16)
    return pl.pallas_call(
        functools.partial(_peer_topk_body, tt=tt),
        grid=(t // tt,),
        in_specs=[pl.BlockSpec((tt, qw), lambda i: (i, 0)),
                  pl.BlockSpec(keys.shape, lambda i: (0, 0, 0))],
        out_specs=[pl.BlockSpec((tt, PEER_SEL), lambda i: (i, 0)),
                   pl.BlockSpec((tt, PEER_SEL), lambda i: (i, 0))],
        out_shape=[jax.ShapeDtypeStruct((t, PEER_SEL), jnp.int32),
                   jax.ShapeDtypeStruct((t, PEER_SEL), F32)],
        compiler_params=_cparams("parallel"),
        name="peer_topk",
    )(q, keys)


SC_CORES = 2
SC_SUBCORES = 16
SC_LANES = 16
SC_TILES = SC_CORES * SC_SUBCORES
ROW_SUB = 8
ROW_LANE = 128
VECS_PER_SUB = ROW_LANE // SC_LANES
D_BLOCKS = 4
VECS_PER_BLOCK = ROW_SUB * VECS_PER_SUB // D_BLOCKS


def _sc_gelu(s):
    z = 0.7978845608028654 * (s + 0.044715 * (s * s * s))
    tanh_z = 1.0 - 2.0 / (jnp.exp(2.0 * z) + 1.0)
    return 0.5 * s * (1.0 + tanh_z)


def _vec_slot(jb, k):
    v = jb * VECS_PER_BLOCK + k
    return v // VECS_PER_SUB, pl.ds((v % VECS_PER_SUB) * SC_LANES, SC_LANES)


SC_TOKB = 8
PART_SUMS = 4
SC_RING = 3
SC_TOKEN_STEP = SC_TILES * SC_TOKB


def _peer_sc_body(e_hbm, g_hbm, x_hbm, uv_hbm, o_hbm, idx_v, idx1_v, g_v, x_v, out_v, acc_v, w_v, ubuf, vbuf,
                  sem_u, sem_v, sem_o, *, batches_per_tile, first_batch):
    wid = lax.axis_index("s") * SC_CORES + lax.axis_index("c")
    base = wid * batches_per_tile
    lane = lax.iota(jnp.int32, SC_LANES)
    n_groups = PEER_SEL // SC_LANES
    gpb = SC_TOKB * n_groups

    def sel16(ref, gi):
        return ref.at[gi // n_groups, pl.ds((gi % n_groups) * SC_LANES, SC_LANES)]

    def gather(gi, slot):
        return (pltpu.make_async_copy(uv_hbm.at[sel16(idx_v, gi)], ubuf.at[slot], sem_u.at[slot]),
                pltpu.make_async_copy(uv_hbm.at[sel16(idx1_v, gi)], vbuf.at[slot], sem_v.at[slot]))

    def start(gi, slot):
        for cp in gather(gi, slot):
            cp.start()

    def out_copy(batch, par):
        return pltpu.make_async_copy(out_v.at[par], o_hbm.at[batch], sem_o.at[par])

    def u_phase(slot, t, gi):
        for jb in range(D_BLOCKS):
            xs = []
            for k in range(VECS_PER_BLOCK):
                sub, ls = _vec_slot(jb, k)
                xs.append(x_v[sub, t, ls])

            @plsc.parallel_loop(0, SC_LANES, unroll=2)
            def _(r, jb=jb, xs=xs):
                parts = []
                per = VECS_PER_BLOCK // PART_SUMS
                for q in range(PART_SUMS):
                    p = xs[q * per] * ubuf[(slot, r) + _vec_slot(jb, q * per)]
                    for k in range(q * per + 1, (q + 1) * per):
                        p = p + xs[k] * ubuf[(slot, r) + _vec_slot(jb, k)]
                    parts.append(p)
                a = (parts[0] + parts[1]) + (parts[2] + parts[3])
                if jb > 0:
                    a = a + acc_v[r, :]
                acc_v[r, :] = a
        cols = [plsc.load_gather(acc_v, [lane, jnp.full((SC_LANES,), l, jnp.int32)]) for l in range(SC_LANES)]
        while len(cols) > 1:
            cols = [cols[i] + cols[i + 1] for i in range(0, len(cols), 2)]
        w_v[...] = sel16(g_v, gi)[...] * _sc_gelu(cols[0])

    def v_phase(slot, par, t):
        for jb in range(D_BLOCKS):
            slots = [_vec_slot(jb, k) for k in range(VECS_PER_BLOCK)]
            init = tuple(jnp.zeros((SC_LANES,), F32) for _ in slots)

            def add_row(r, os, jb=jb):
                wb = plsc.load_gather(w_v, [jnp.full((SC_LANES,), r, jnp.int32)])
                return tuple(os[k] + wb * vbuf[(slot, r) + _vec_slot(jb, k)] for k in range(VECS_PER_BLOCK))

            os = plsc.parallel_loop(0, SC_LANES, unroll=2, carry=init)(add_row)
            for (sub, ls), o in zip(slots, os):
                plsc.addupdate(out_v.at[par, sub, t, ls], o)

    @pl.loop(0, batches_per_tile)
    def _(bi):
        batch = base + bi
        src = first_batch + batch
        par = bi % 2
        pltpu.sync_copy(e_hbm.at[pl.ds(src * SC_TOKB, SC_TOKB)], idx_v)
        pltpu.sync_copy(g_hbm.at[pl.ds(src * SC_TOKB, SC_TOKB)], g_v)
        pltpu.sync_copy(x_hbm.at[src], x_v)
        for tt in range(SC_TOKB):
            for c in range(n_groups):
                cols = pl.ds(c * SC_LANES, SC_LANES)
                idx1_v[tt, cols] = idx_v[tt, cols] + 1
        for gi0 in range(SC_RING - 1):
            start(gi0, gi0)

        @pl.when(bi >= 2)
        def _():
            out_copy(batch, par).wait()

        def step(gi, slot):
            gi = jnp.asarray(gi, jnp.int32)
            t = gi // n_groups

            @pl.when(gi + SC_RING - 1 < gpb)
            def _():
                start(gi + SC_RING - 1, (slot + SC_RING - 1) % SC_RING)

            @pl.when(gi % n_groups == 0)
            def _():
                for j in range(ROW_SUB):
                    for l in range(VECS_PER_SUB):
                        out_v[par, j, t, pl.ds(l * SC_LANES, SC_LANES)] = jnp.zeros((SC_LANES,), F32)

            cu, cv = gather(gi, slot)
            cu.wait()
            u_phase(slot, t, gi)
            cv.wait()
            v_phase(slot, par, t)

        @pl.loop(0, gpb // SC_RING)
        def _(it):
            for slot in range(SC_RING):
                step(it * SC_RING + slot, slot)

        for gi in range(gpb - gpb % SC_RING, gpb):
            step(gi, gi % SC_RING)

        out_copy(batch, par).start()

    for par in range(2):
        out_copy(base, par).wait()


def _expert_rows(u, v):
    n_e = u.shape[0]
    rows = jnp.stack([u.reshape(n_e, ROW_SUB, ROW_LANE), v.reshape(n_e, ROW_SUB, ROW_LANE)], axis=1)
    return rows.reshape(2 * n_e, ROW_SUB, ROW_LANE)


def _peer_experts(e, g, xm, uv, first_token=0):
    t_all, d = xm.shape
    t = t_all - first_token
    assert d == ROW_SUB * ROW_LANE and e.shape == (t_all, PEER_SEL) and first_token % SC_TOKB == 0
    assert t % SC_TOKEN_STEP == 0 and t >= 2 * SC_TOKEN_STEP
    n_batches = t // SC_TOKB

    def tiled(a):
        return a.reshape(a.shape[0] // SC_TOKB, SC_TOKB, ROW_SUB, ROW_LANE).transpose(0, 2, 1, 3)

    call = pl.kernel(
        functools.partial(_peer_sc_body, batches_per_tile=n_batches // SC_TILES,
                          first_batch=first_token // SC_TOKB),
        out_type=jax.ShapeDtypeStruct((n_batches, ROW_SUB, SC_TOKB, ROW_LANE), F32),
        mesh=plsc.VectorSubcoreMesh(core_axis_name="c", subcore_axis_name="s"),
        scratch_types=[pltpu.VMEM((SC_TOKB, PEER_SEL), jnp.int32),
                       pltpu.VMEM((SC_TOKB, PEER_SEL), jnp.int32),
                       pltpu.VMEM((SC_TOKB, PEER_SEL), F32),
                       pltpu.VMEM((ROW_SUB, SC_TOKB, ROW_LANE), F32),
                       pltpu.VMEM((2, ROW_SUB, SC_TOKB, ROW_LANE), F32),
                       pltpu.VMEM((SC_LANES, SC_LANES), F32),
                       pltpu.VMEM((SC_LANES,), F32),
                       pltpu.VMEM((SC_RING, SC_LANES, ROW_SUB, ROW_LANE), F32),
                       pltpu.VMEM((SC_RING, SC_LANES, ROW_SUB, ROW_LANE), F32),
                       pltpu.SemaphoreType.DMA((SC_RING,)),
                       pltpu.SemaphoreType.DMA((SC_RING,)),
                       pltpu.SemaphoreType.DMA((2,))],
        compiler_params=pltpu.CompilerParams(needs_layout_passes=False),
        name="peer_experts_sc",
    )
    out = call(e, g, tiled(xm), uv)
    return out.transpose(0, 2, 1, 3).reshape(t, d)


PEER_TB = 8
HALF_WORDS = ROW_SUB * ROW_LANE // 2
WORD_SUB = HALF_WORDS // ROW_LANE


def _expert_rows_bf16(u, v):
    def pack(a):
        bits = lax.bitcast_convert_type(a.astype(BF16), jnp.uint16).astype(jnp.uint32)
        return lax.bitcast_convert_type((bits[:, HALF_WORDS:] << 16) | bits[:, :HALF_WORDS], jnp.int32)

    n_e = u.shape[0]
    return jnp.stack([pack(u), pack(v)], axis=1).reshape(n_e, 2 * WORD_SUB, ROW_LANE)


def _bf16_pair(words):
    low = lax.bitcast_convert_type(words << 16, F32)
    high = lax.bitcast_convert_type(words & jnp.int32(-65536), F32)
    return low, high


def _peer_tc_body(e_cur, e_nxt, g_ref, x_ref, uv_hbm, o_ref, buf, sem, *, n_steps):
    i = pl.program_id(0)
    slot = i % 2

    def issue(e_ref, dst):
        for t in range(PEER_TB):
            def one(k, carry, t=t):
                row = e_ref[t, k] >> 1
                pltpu.make_async_copy(uv_hbm.at[row], buf.at[dst, :, t * PEER_SEL + k], sem.at[dst]).start()
                return carry
            lax.fori_loop(0, PEER_SEL, one, 0, unroll=8)

    @pl.when(i == 0)
    def _():
        issue(e_cur, slot)

    @pl.when(i + 1 < n_steps)
    def _():
        issue(e_nxt, 1 - slot)

    pltpu.make_async_copy(buf.at[slot], buf.at[slot], sem.at[slot]).wait()

    gates = g_ref[...].T
    for t in range(PEER_TB):
        rows = slice(t * PEER_SEL, (t + 1) * PEER_SEL)
        acc = None
        for j in range(WORD_SUB):
            lo_cols = slice(j * ROW_LANE, (j + 1) * ROW_LANE)
            hi_cols = slice(HALF_WORDS + j * ROW_LANE, HALF_WORDS + (j + 1) * ROW_LANE)
            low, high = _bf16_pair(buf[slot, j, rows, :])
            term = low * x_ref[t:t + 1, lo_cols] + high * x_ref[t:t + 1, hi_cols]
            acc = term if acc is None else acc + term
        w = gates[:, t:t + 1] * _gelu(jnp.sum(acc, axis=-1, keepdims=True))
        for j in range(WORD_SUB):
            lo_cols = slice(j * ROW_LANE, (j + 1) * ROW_LANE)
            hi_cols = slice(HALF_WORDS + j * ROW_LANE, HALF_WORDS + (j + 1) * ROW_LANE)
            low, high = _bf16_pair(buf[slot, WORD_SUB + j, rows, :])
            o_ref[t:t + 1, lo_cols] = jnp.sum(w * low, axis=0, keepdims=True)
            o_ref[t:t + 1, hi_cols] = jnp.sum(w * high, axis=0, keepdims=True)


def _peer_experts_tc(e, g, xm, uvb, n_tokens):
    t, d = n_tokens, xm.shape[1]
    assert d == 2 * HALF_WORDS and t % PEER_TB == 0 and t <= xm.shape[0]
    n_steps = t // PEER_TB
    last = n_steps - 1
    n_rows = PEER_TB * PEER_SEL
    return pl.pallas_call(
        functools.partial(_peer_tc_body, n_steps=n_steps),
        grid=(n_steps,),
        in_specs=[pl.BlockSpec((PEER_TB, PEER_SEL), lambda i: (i, 0), memory_space=pltpu.SMEM),
                  pl.BlockSpec((PEER_TB, PEER_SEL), lambda i: (jnp.minimum(i + 1, last), 0), memory_space=pltpu.SMEM),
                  pl.BlockSpec((PEER_TB, PEER_SEL), lambda i: (i, 0)),
                  pl.BlockSpec((PEER_TB, d), lambda i: (i, 0)),
                  pl.BlockSpec(memory_space=pl.ANY)],
        out_specs=pl.BlockSpec((PEER_TB, d), lambda i: (i, 0)),
        out_shape=jax.ShapeDtypeStruct((t, d), F32),
        scratch_shapes=[pltpu.VMEM((2, 2 * WORD_SUB, n_rows, ROW_LANE), jnp.int32),
                        pltpu.SemaphoreType.DMA((2,))],
        compiler_params=_cparams("arbitrary"),
        name="peer_experts_tc",
    )(e, e, g, xm, uvb)


def _peer_select(q, sub_keys):
    return tuple(_peer_topk(q, sub_keys))


def _final_body(*refs, delta_starts):
    h_ref, *delta_refs, gate_ref, g_ref, o_ref = refs
    h = h_ref[...] + gate_ref[0] * _pick_piece(delta_refs, delta_starts)
    o_ref[...] = _rms(h) * g_ref[...]


def _final(h, delta, gate, gain, *, rows_per_group, tm=256):
    t, d = h.shape
    row = pl.BlockSpec((tm, d), lambda i: (i, 0))
    piece_specs, delta_starts = _piece_specs(delta, tm, d)
    return pl.pallas_call(
        functools.partial(_final_body, delta_starts=delta_starts),
        grid=(t // tm,),
        in_specs=[row, *piece_specs, pl.BlockSpec((1, 1, d), lambda i: ((i * tm) // rows_per_group, 0, 0)),
                  pl.BlockSpec((1, d), lambda i: (0, 0))],
        out_specs=row,
        out_shape=jax.ShapeDtypeStruct((t, d), F32),
        compiler_params=_cparams("parallel"),
        name="final_norm",
    )(h, *delta, gate, gain.reshape(1, d))


def kernel(x, c, ctx, c_ctx, w_mod, b_mod, attn_w_in, attn_w_out, lambda_q1, lambda_k1, lambda_q2, lambda_k2,
           subln_g, na_rpb, chunk_w_in, chunk_b_in, chunk_ln_g, chunk_ln_b, chunk_w_s, chunk_b_s, chunk_w_out,
           peer_w_query, peer_sub_keys, peer_u, peer_v, final_norm_g):
    b, n, d = x.shape
    n_ctx = ctx.shape[1]

    cond = jnp.concatenate([c, c_ctx[None], jnp.zeros((-(b + 1) % 8, d), F32)], axis=0)
    mods = [_adaln(cond, w_mod[i], b_mod[i]) for i in range(2)]
    m0, m1 = mods

    w_in = attn_w_in[0].astype(BF16)
    w_out = attn_w_out[0].astype(BF16)
    w_q0 = peer_w_query[0].astype(BF16)
    w_q1 = peer_w_query[1].astype(BF16)
    w_gin = chunk_w_in[0].astype(BF16)
    w_gout = chunk_w_out[0].astype(BF16)
    rope = _rope_tables(n, 512)
    bias_tabs = _na_bias_tables(na_rpb[0], n // GRID_W)
    lam_init = 0.8 - 0.6 * math.exp(-0.3 * 0)
    lam = (jnp.exp(jnp.sum(lambda_q1[0] * lambda_k1[0])) - jnp.exp(jnp.sum(lambda_q2[0] * lambda_k2[0]))
           + lam_init).astype(F32)
    ka = A_HEADS * 2 * A_QK_DIM

    def cx(m, k):
        return m[b:b + 1, k * d:(k + 1) * d].reshape(1, 1, d)

    sizes = PIPE_BATCHES if sum(PIPE_BATCHES) == b else (b,)
    starts = [sum(sizes[:ci]) for ci in range(len(sizes))]
    n_chunks = len(sizes)

    def rows(a, ci):
        return a[starts[ci]:starts[ci] + sizes[ci]]

    def lat(ci, m, k):
        return rows(m, ci)[:, k * d:(k + 1) * d].reshape(sizes[ci], 1, d)

    def stage_a(ci, x2):
        bc = sizes[ci]
        zc = _mod_mm(rows(ctx, ci).reshape(bc * n_ctx, d), cx(m0, 0), cx(m0, 1), w_in, rows_per_group=n_ctx,
                     tm=256, name="attn_in_ctx")[0]
        z = _mod_mm(x2, lat(ci, m0, 0), lat(ci, m0, 1), w_in, rows_per_group=n, tm=256, rope=rope,
                    rope_chunks=(0, 1), scale_chunks=(0, 3), name="attn_in")[0]
        z3 = z.reshape(bc, n, -1)
        zc3 = zc.reshape(bc, n_ctx, -1)
        oa = _diff_attn(z3, zc3, lam, subln_g[0], lam_init)
        ob = _na_attn(z3, zc3, bias_tabs)
        h1 = _out_proj(x2, lat(ci, m0, 2), oa.reshape(bc * n, ka), ob.reshape(bc * n, -1), w_out[:ka], w_out[ka:],
                       rows_per_group=n)
        q0, xm0 = _mod_mm(h1, lat(ci, m0, 3), lat(ci, m0, 4), w_q0, rows_per_group=n, tm=256, emit_xm=True,
                          name="peer_q0")
        return h1, xm0, _peer_select(q0, peer_sub_keys[0])

    def stage_b(ci, h1, p0):
        zg, h1b = _mod_mm(h1, lat(ci, m1, 0), lat(ci, m1, 1), w_gin, rows_per_group=n, tm=256, delta=p0,
                          gate=lat(ci, m0, 5), bias=chunk_b_in[0], act="gelu", name="gmlp_in")
        h2 = _gmlp_tail(zg, h1b, lat(ci, m1, 2), chunk_ln_g[0], chunk_ln_b[0], chunk_w_s[0], chunk_b_s[0], w_gout,
                        rows_per_group=n)
        q1, xm1 = _mod_mm(h2, lat(ci, m1, 3), lat(ci, m1, 4), w_q1, rows_per_group=n, tm=256, emit_xm=True,
                          name="peer_q1")
        return h2, xm1, _peer_select(q1, peer_sub_keys[1])

    tables = [(_expert_rows(peer_u[i], peer_v[i]), _expert_rows_bf16(peer_u[i], peer_v[i])) for i in range(2)]

    def experts(sel, xm, tabs):
        e, g = sel
        t = xm.shape[0]
        k = int(t * PEER_TC_SHARE) // SC_TOKEN_STEP * SC_TOKEN_STEP
        if k == 0 or (t - k) % SC_TOKEN_STEP:
            return (_peer_experts(e, g, xm, tabs[0]),)
        p_sc = _peer_experts(e, g, xm, tabs[0], first_token=k)
        p_tc = _peer_experts_tc(e, g, xm, tabs[1], n_tokens=k)
        return p_tc, p_sc

    h1s, p0s = [], []
    prev_sel = None
    for ci in range(n_chunks):
        x2 = rows(x, ci).reshape(sizes[ci] * n, d)
        if prev_sel is not None:
            x2, prev_sel, tables = lax.optimization_barrier((x2, prev_sel, tables))
            p0s.append(experts(prev_sel, xm_prev, tables[0]))
        h1, xm_prev, prev_sel = stage_a(ci, x2)
        h1s.append(h1)
    p0s.append(experts(prev_sel, xm_prev, tables[0]))

    h2s, p1s = [], []
    prev_sel = None
    for ci in range(n_chunks):
        p0 = p0s[ci]
        if prev_sel is not None:
            p0, prev_sel = lax.optimization_barrier((p0, prev_sel))
            p1s.append(experts(prev_sel, xm_prev, tables[1]))
        h2, xm_prev, prev_sel = stage_b(ci, h1s[ci], p0)
        h2s.append(h2)
    p1s.append(experts(prev_sel, xm_prev, tables[1]))

    outs = [_final(h2s[ci], p1s[ci], lat(ci, m1, 5), final_norm_g, rows_per_group=n).reshape(sizes[ci], n, d)
            for ci in range(n_chunks)]
    return outs[0] if n_chunks == 1 else jnp.concatenate(outs, axis=0)
```

```python
import functools
import math

import numpy as np
import jax
import jax.numpy as jnp
from jax import lax
from jax.experimental import pallas as pl
from jax.experimental.pallas import tpu as pltpu
from jax.experimental.pallas import tpu_sc as plsc

F32 = jnp.float32
BF16 = jnp.bfloat16

EPS = 1e-6
GRID_W = 64
NA_ROWS = 8
NA_COLS = 16
ROPE_BASE = 10000.0
A_HEADS = 4
A_QK_DIM = 64
B_HEADS = 8
B_HEAD_DIM = 64
PEER_HEADS = 8
N_KEYS = 128
PEER_TOPK = 16
PEER_SEL = PEER_HEADS * PEER_TOPK
CHUNK = 128
C_GROUPS = 8
MASK_VALUE = -1e30

VMEM_LIMIT = 56 * 1024 * 1024
PIPE_BATCHES = (1, 2, 2, 3)
PEER_TC_BLOCKS = (5, 11, 11, 15)


def _cparams(*sem):
    return pltpu.CompilerParams(dimension_semantics=sem, vmem_limit_bytes=VMEM_LIMIT)


def _rms(x):
    return x * lax.rsqrt(jnp.mean(x * x, axis=-1, keepdims=True) + EPS)


def _gelu(x):
    return jax.nn.gelu(x)


def _adaln_body(c_ref, w_ref, b_ref, o_ref):
    cnd = c_ref[...]
    a = (cnd * jax.nn.sigmoid(cnd)).astype(BF16)
    o_ref[...] = jnp.dot(a, w_ref[...].astype(BF16), preferred_element_type=F32) + b_ref[...]


def _adaln(cond, w, b):
    m, d = cond.shape
    n = w.shape[1]
    tn = 1024
    return pl.pallas_call(
        _adaln_body,
        grid=(n // tn,),
        in_specs=[pl.BlockSpec((m, d), lambda j: (0, 0)),
                  pl.BlockSpec((d, tn), lambda j: (0, j)),
                  pl.BlockSpec((1, tn), lambda j: (0, j))],
        out_specs=pl.BlockSpec((m, tn), lambda j: (0, j)),
        out_shape=jax.ShapeDtypeStruct((m, n), F32),
        compiler_params=_cparams("parallel"),
        name="adaln",
    )(cond, w, b.reshape(1, n))


def _piece_specs(pieces, tm, d):
    specs, starts, b0 = [], [], 0
    for p in pieces:
        nb = p.shape[0] // tm
        assert nb * tm == p.shape[0]
        specs.append(pl.BlockSpec((tm, d), lambda i, b0=b0, nb=nb: (jnp.clip(i - b0, 0, nb - 1), 0)))
        starts.append(b0)
        b0 += nb
    return specs, tuple(starts)


def _pick_piece(refs, starts):
    i = pl.program_id(0)
    val = refs[0][...]
    for ref, start in zip(refs[1:], starts[1:]):
        val = jnp.where(i >= start, ref[...], val)
    return val


def _mod_mm_body(*refs, delta_starts, has_bias, rope_chunks, scale_chunks, act, emit_xm, tn, n_out):
    it = iter(refs)
    h_ref = next(it)
    has_delta = bool(delta_starts)
    delta_refs = gate_ref = None
    if has_delta:
        delta_refs = [next(it) for _ in delta_starts]
        gate_ref = next(it)
    shift_ref, scale_ref, w_ref = next(it), next(it), next(it)
    b_ref = next(it) if has_bias else None
    cos_ref = sin_ref = None
    if rope_chunks:
        cos_ref, sin_ref = next(it), next(it)
    y_ref = next(it)
    hn_ref = next(it) if has_delta else None
    xm_ref = next(it) if emit_xm else None

    h = h_ref[...]
    if has_delta:
        h = h + gate_ref[0] * _pick_piece(delta_refs, delta_starts)
        hn_ref[...] = h
    xm = _rms(h) * (1.0 + scale_ref[0]) + shift_ref[0]
    if emit_xm:
        xm_ref[...] = xm
    xb = xm.astype(BF16)
    for c in range(n_out // tn):
        cols = slice(c * tn, (c + 1) * tn)
        y = jnp.dot(xb, w_ref[:, cols], preferred_element_type=F32)
        if has_bias:
            y = y + b_ref[:, cols]
        if c in rope_chunks:
            lane = lax.broadcasted_iota(jnp.int32, y.shape, 1)
            partner = jnp.where((lane & 16) == 0, pltpu.roll(y, tn - 16, 1), pltpu.roll(y, 16, 1))
            y = y * cos_ref[...] + partner * sin_ref[...]
        if c in scale_chunks:
            y = y * 0.125
        if act == "gelu":
            y = _gelu(y)
        y_ref[:, cols] = y


def _mod_mm(h, shift, scale, w, *, rows_per_group, tm, delta=None, gate=None, bias=None, rope=None,
            rope_chunks=(), scale_chunks=(), act="none", emit_xm=False, name="mod_mm"):
    t, d = h.shape
    n_out = w.shape[1]
    tn = 512
    groups = shift.shape[0]
    assert t % tm == 0 and rows_per_group % tm == 0 and n_out % tn == 0

    def gidx(i):
        return ((i * tm) // rows_per_group if groups > 1 else 0, 0, 0)

    row_spec = pl.BlockSpec((tm, d), lambda i: (i, 0))
    mod_spec = pl.BlockSpec((1, 1, d), gidx)
    args, specs = [h], [row_spec]
    delta_starts = ()
    if delta is not None:
        piece_specs, delta_starts = _piece_specs(delta, tm, d)
        args += [*delta, gate]
        specs += [*piece_specs, mod_spec]
    args += [shift, scale, w]
    specs += [mod_spec, mod_spec, pl.BlockSpec((d, n_out), lambda i: (0, 0))]
    if bias is not None:
        args.append(bias.reshape(1, n_out))
        specs.append(pl.BlockSpec((1, n_out), lambda i: (0, 0)))
    if rope_chunks:
        cos, sin = rope
        pos_blocks = cos.shape[0] // tm
        args += [cos, sin]
        specs += [pl.BlockSpec((tm, tn), lambda i: (i % pos_blocks, 0))] * 2
    out_shape = [jax.ShapeDtypeStruct((t, n_out), F32)]
    out_specs = [pl.BlockSpec((tm, n_out), lambda i: (i, 0))]
    if delta is not None:
        out_shape.append(jax.ShapeDtypeStruct((t, d), F32))
        out_specs.append(row_spec)
    if emit_xm:
        out_shape.append(jax.ShapeDtypeStruct((t, d), F32))
        out_specs.append(row_spec)
    body = functools.partial(_mod_mm_body, delta_starts=delta_starts, has_bias=bias is not None,
                             rope_chunks=tuple(rope_chunks), scale_chunks=tuple(scale_chunks), act=act,
                             emit_xm=emit_xm, tn=tn, n_out=n_out)
    return pl.pallas_call(body, grid=(t // tm,), in_specs=specs, out_specs=out_specs, out_shape=out_shape,
                          compiler_params=_cparams("parallel"), name=name)(*args)


def _rope_tables(n_tokens, width):
    axis_dim = A_QK_DIM // 2
    inv_freq = 1.0 / (ROPE_BASE ** (jnp.arange(0, axis_dim, 2, dtype=F32) / axis_dim))
    t = jnp.arange(n_tokens)
    row = (t // GRID_W).astype(F32)
    col = (t % GRID_W).astype(F32)
    ang_r = row[:, None] * inv_freq
    ang_c = col[:, None] * inv_freq
    ang = jnp.concatenate([ang_r, ang_r, ang_c, ang_c], axis=1)
    sign = jnp.tile(jnp.concatenate([-jnp.ones(16, F32), jnp.ones(16, F32)]), 2)
    reps = width // A_QK_DIM
    return jnp.tile(jnp.cos(ang), (1, reps)), jnp.tile(jnp.sin(ang) * sign, (1, reps))


def _diff_attn_body(lam_ref, q_ref, k_ref, v_ref, kc_ref, vc_ref, g_ref, o_ref, *, out_scale):
    lam = lam_ref[0, 0]
    q = q_ref[0]
    k = k_ref[0].astype(BF16)
    kc = kc_ref[0].astype(BF16)
    lane = lax.broadcasted_iota(jnp.int32, q.shape, 1)
    nt = (((1,), (1,)), ((), ()))
    parts = []
    for m in range(2):
        keep = (lane < A_QK_DIM) if m == 0 else (lane >= A_QK_DIM)
        qm = jnp.where(keep, q, 0.0).astype(BF16)
        s_l = lax.dot_general(qm, k, nt, preferred_element_type=F32)
        s_c = lax.dot_general(qm, kc, nt, preferred_element_type=F32)
        mx = jnp.maximum(jnp.max(s_l, axis=-1, keepdims=True), jnp.max(s_c, axis=-1, keepdims=True))
        e_l = jnp.exp(s_l - mx)
        e_c = jnp.exp(s_c - mx)
        inv = 1.0 / (jnp.sum(e_l, axis=-1, keepdims=True) + jnp.sum(e_c, axis=-1, keepdims=True))
        parts.append((e_l, e_c, inv))
    (e1l, e1c, inv1), (e2l, e2c, inv2) = parts
    w2 = lam * inv2
    a_l = (e1l * inv1 - e2l * w2).astype(BF16)
    a_c = (e1c * inv1 - e2c * w2).astype(BF16)
    o = (jnp.dot(a_l, v_ref[0].astype(BF16), preferred_element_type=F32)
         + jnp.dot(a_c, vc_ref[0].astype(BF16), preferred_element_type=F32))
    o_ref[0] = _rms(o) * g_ref[...] * out_scale


def _diff_attn(z, zc, lam, subln_g, lam_init, *, tq=256):
    b, n, _ = z.shape
    c = zc.shape[1]
    kcol, vcol = A_HEADS, 2 * A_HEADS
    return pl.pallas_call(
        functools.partial(_diff_attn_body, out_scale=1.0 - lam_init),
        grid=(b, A_HEADS, n // tq),
        in_specs=[pl.BlockSpec((1, 1), lambda bi, h, i: (0, 0), memory_space=pltpu.SMEM),
                  pl.BlockSpec((1, tq, 128), lambda bi, h, i: (bi, i, h)),
                  pl.BlockSpec((1, n, 128), lambda bi, h, i: (bi, 0, kcol + h)),
                  pl.BlockSpec((1, n, 128), lambda bi, h, i: (bi, 0, vcol + h)),
                  pl.BlockSpec((1, c, 128), lambda bi, h, i: (bi, 0, kcol + h)),
                  pl.BlockSpec((1, c, 128), lambda bi, h, i: (bi, 0, vcol + h)),
                  pl.BlockSpec((1, 128), lambda bi, h, i: (0, 0))],
        out_specs=pl.BlockSpec((1, tq, 128), lambda bi, h, i: (bi, i, h)),
        out_shape=jax.ShapeDtypeStruct((b, n, A_HEADS * 128), F32),
        compiler_params=_cparams("parallel", "parallel", "arbitrary"),
        name="diff_attn",
    )(lam.reshape(1, 1), z, z, z, zc, zc, subln_g.reshape(1, 128))


NA_QROWS = 8
NA_KROWS = 16
NA_SEG = 4 * GRID_W


def _na_window_start(j, rows):
    return jnp.clip(NA_QROWS * j - NA_ROWS // 2, 0, rows - NA_KROWS)


def _na_bias_tables(rpb, rows):
    n_blocks = rows // NA_QROWS
    h = rpb.shape[0]
    ic = np.clip(np.arange(GRID_W)[None, :] - np.arange(GRID_W)[:, None] + (NA_COLS - 1), 0, 2 * NA_COLS - 2)
    toep = jnp.pad(rpb[:, :, ic], ((0, 0), (NA_KROWS, NA_KROWS), (0, 0), (0, 0)))
    tabs, oks = [], []
    for j in (0, 1, n_blocks - 1):
        ks = int(np.clip(NA_QROWS * j - NA_ROWS // 2, 0, rows - NA_KROWS))
        slabs = []
        for rq in range(NA_QROWS):
            dr0 = ks - (NA_QROWS * j + rq) + (NA_ROWS - 1) + NA_KROWS
            slab = toep[:, dr0:dr0 + NA_KROWS]
            slabs.append(slab.transpose(0, 2, 1, 3).reshape(h, GRID_W, NA_KROWS * GRID_W))
        tabs.append(jnp.concatenate(slabs, axis=1))
        r = (NA_QROWS * j + np.arange(NA_QROWS))[:, None, None, None]
        cq = np.arange(GRID_W)[None, :, None, None]
        kr = (ks + np.arange(NA_KROWS))[None, None, :, None]
        kc = np.arange(GRID_W)[None, None, None, :]
        r0 = np.clip(r - NA_ROWS // 2, 0, rows - NA_ROWS)
        c0 = np.clip(cq - NA_COLS // 2, 0, GRID_W - NA_COLS)
        ok = (kr >= r0) & (kr < r0 + NA_ROWS) & (kc >= c0) & (kc < c0 + NA_COLS)
        oks.append(ok.reshape(NA_QROWS * GRID_W, NA_KROWS * GRID_W))
    return jnp.where(np.stack(oks)[None], jnp.stack(tabs, axis=1), MASK_VALUE)


def _na_body(q_ref, k0, k1, k2, k3, v0, v1, v2, v3, kc_ref, vc_ref, bias_ref, o_ref):
    q = q_ref[0]
    lane = lax.broadcasted_iota(jnp.int32, q.shape, 1)
    nt = (((1,), (1,)), ((), ()))
    ks = [r[0].astype(BF16) for r in (k0, k1, k2, k3)]
    vs = [r[0].astype(BF16) for r in (v0, v1, v2, v3)]
    kc = kc_ref[0].astype(BF16)
    vc = vc_ref[0].astype(BF16)
    outs = []
    for hh in range(2):
        keep = (lane < B_HEAD_DIM) if hh == 0 else (lane >= B_HEAD_DIM)
        qh = jnp.where(keep, q, 0.0).astype(BF16)
        segs = [lax.dot_general(qh, ks[i], nt, preferred_element_type=F32)
                + bias_ref[hh, 0, :, i * NA_SEG:(i + 1) * NA_SEG] for i in range(4)]
        s_c = lax.dot_general(qh, kc, nt, preferred_element_type=F32)
        mx = jnp.max(s_c, axis=-1, keepdims=True)
        for s in segs:
            mx = jnp.maximum(mx, jnp.max(s, axis=-1, keepdims=True))
        e_c = jnp.exp(s_c - mx)
        den = jnp.sum(e_c, axis=-1, keepdims=True)
        es = []
        for s in segs:
            e = jnp.exp(s - mx)
            den = den + jnp.sum(e, axis=-1, keepdims=True)
            es.append(e)
        inv = 1.0 / den
        o = jnp.dot((e_c * inv).astype(BF16), vc, preferred_element_type=F32)
        for e, v in zip(es, vs):
            o = o + jnp.dot((e * inv).astype(BF16), v, preferred_element_type=F32)
        outs.append(o)
    o_ref[0] = jnp.where(lane < B_HEAD_DIM, outs[0], outs[1])


def _na_attn(z, zc, bias_tabs):
    b, n, _ = z.shape
    c = zc.shape[1]
    rows = n // GRID_W
    n_blocks = rows // NA_QROWS
    tq = NA_QROWS * GRID_W
    qcol, kcol, vcol = 12, 16, 20
    seg_rows = NA_SEG // GRID_W

    def kv_spec(col0, i):
        return pl.BlockSpec((1, NA_SEG, 128),
                            lambda hp, j, bi: (bi, _na_window_start(j, rows) // seg_rows + i, col0 + hp))

    def variant(j):
        return jnp.where(j == 0, 0, jnp.where(j == n_blocks - 1, 2, 1))

    in_specs = [pl.BlockSpec((1, tq, 128), lambda hp, j, bi: (bi, j, qcol + hp))]
    in_specs += [kv_spec(kcol, i) for i in range(4)]
    in_specs += [kv_spec(vcol, i) for i in range(4)]
    in_specs += [pl.BlockSpec((1, c, 128), lambda hp, j, bi: (bi, 0, kcol + hp)),
                 pl.BlockSpec((1, c, 128), lambda hp, j, bi: (bi, 0, vcol + hp)),
                 pl.BlockSpec((2, 1, tq, NA_KROWS * GRID_W), lambda hp, j, bi: (hp, variant(j), 0, 0))]
    return pl.pallas_call(
        _na_body,
        grid=(B_HEADS // 2, n_blocks, b),
        in_specs=in_specs,
        out_specs=pl.BlockSpec((1, tq, 128), lambda hp, j, bi: (bi, j, hp)),
        out_shape=jax.ShapeDtypeStruct((b, n, B_HEADS * B_HEAD_DIM), F32),
        compiler_params=_cparams("parallel", "parallel", "arbitrary"),
        name="na_attn",
    )(z, z, z, z, z, z, z, z, z, zc, zc, bias_tabs)


def _out_proj_body(h_ref, gate_ref, xa_ref, xb_ref, wa_ref, wb_ref, o_ref):
    y = (jnp.dot(xa_ref[...].astype(BF16), wa_ref[...], preferred_element_type=F32)
         + jnp.dot(xb_ref[...].astype(BF16), wb_ref[...], preferred_element_type=F32))
    o_ref[...] = h_ref[...] + gate_ref[0] * y


def _out_proj(h, gate, xa, xb, wa, wb, *, rows_per_group, tm=512):
    t, d = h.shape
    ka, kb = xa.shape[1], xb.shape[1]
    return pl.pallas_call(
        _out_proj_body,
        grid=(t // tm,),
        in_specs=[pl.BlockSpec((tm, d), lambda i: (i, 0)),
                  pl.BlockSpec((1, 1, d), lambda i: ((i * tm) // rows_per_group, 0, 0)),
                  pl.BlockSpec((tm, ka), lambda i: (i, 0)),
                  pl.BlockSpec((tm, kb), lambda i: (i, 0)),
                  pl.BlockSpec((ka, d), lambda i: (0, 0)),
                  pl.BlockSpec((kb, d), lambda i: (0, 0))],
        out_specs=pl.BlockSpec((tm, d), lambda i: (i, 0)),
        out_shape=jax.ShapeDtypeStruct((t, d), F32),
        compiler_params=_cparams("parallel"),
        name="out_proj",
    )(h, gate, xa, xb, wa, wb)


def _gmlp_tail_body(z_ref, h_ref, gate_ref, lng_ref, lnb_ref, ws_ref, bs_ref, wo_ref, o_ref, gated_ref, *, tm, width):
    gw = width // C_GROUPS
    for ci in range(tm // CHUNK):
        rows = slice(ci * CHUNK, (ci + 1) * CHUNK)
        v = z_ref[rows, width:]
        mu = jnp.mean(v, axis=-1, keepdims=True)
        var = jnp.mean(jnp.square(v - mu), axis=-1, keepdims=True)
        vn = ((v - mu) * lax.rsqrt(var + EPS)) * lng_ref[...] + lnb_ref[...]
        vb = vn.astype(BF16)
        for g in range(C_GROUPS):
            cols = slice(g * gw, (g + 1) * gw)
            sv = jnp.dot(ws_ref[g], vb[:, cols], preferred_element_type=F32) + bs_ref[g]
            gated_ref[rows, cols] = (z_ref[rows, cols] * sv).astype(BF16)
    y = jnp.dot(gated_ref[...], wo_ref[...], preferred_element_type=F32)
    o_ref[...] = h_ref[...] + gate_ref[0] * y


def _gmlp_tail(z, h, gate, ln_g, ln_b, w_s, b_s, w_out, *, rows_per_group, tm=512):
    t, d = h.shape
    width = z.shape[1] // 2
    gw = width // C_GROUPS
    bsb = jnp.broadcast_to(b_s[:, :, None], (C_GROUPS, CHUNK, gw))
    return pl.pallas_call(
        functools.partial(_gmlp_tail_body, tm=tm, width=width),
        grid=(t // tm,),
        in_specs=[pl.BlockSpec((tm, 2 * width), lambda i: (i, 0)),
                  pl.BlockSpec((tm, d), lambda i: (i, 0)),
                  pl.BlockSpec((1, 1, d), lambda i: ((i * tm) // rows_per_group, 0, 0)),
                  pl.BlockSpec((1, width), lambda i: (0, 0)),
                  pl.BlockSpec((1, width), lambda i: (0, 0)),
                  pl.BlockSpec((C_GROUPS, CHUNK, CHUNK), lambda i: (0, 0, 0)),
                  pl.BlockSpec((C_GROUPS, CHUNK, gw), lambda i: (0, 0, 0)),
                  pl.BlockSpec((width, d), lambda i: (0, 0))],
        out_specs=pl.BlockSpec((tm, d), lambda i: (i, 0)),
        out_shape=jax.ShapeDtypeStruct((t, d), F32),
        scratch_shapes=[pltpu.VMEM((tm, width), BF16)],
        compiler_params=_cparams("parallel"),
        name="gmlp_tail",
    )(z, h, gate, ln_g.reshape(1, width), ln_b.reshape(1, width), w_s.astype(BF16), bsb, w_out)


def _topk_rows(scores, payloads, k):
    n_rows = scores[0].shape[0]
    riota = lax.broadcasted_iota(jnp.int32, scores[0].shape, 0)
    scores = list(scores)
    vals = [[] for _ in scores]
    pays = [[] for _ in scores]
    for _ in range(k):
        for c, (s, payload) in enumerate(zip(scores, payloads)):
            m = jnp.max(s, axis=0, keepdims=True)
            first = jnp.min(jnp.where(s == m, riota, n_rows), axis=0, keepdims=True)
            sel = riota == first
            vals[c].append(m)
            pays[c].append(first if payload is None
                           else jnp.max(jnp.where(sel, payload, -1), axis=0, keepdims=True))
            scores[c] = jnp.where(sel, -jnp.inf, s)
    return [(jnp.concatenate(v, axis=0), jnp.concatenate(p, axis=0)) for v, p in zip(vals, pays)]


def _pair_candidates(x1, x2, combine, fill):
    k = PEER_TOPK
    sub = lax.broadcasted_iota(jnp.int32, (8, x1.shape[1]), 0)
    blocks = [combine(x1[0:1], x2)]
    for a in range(1, 8):
        blocks.append(jnp.where(sub < k // (a + 1), combine(x1[a:a + 1], x2[0:8]), fill))
    blocks.append(combine(x1[8:k], x2[0:1]))
    return jnp.concatenate(blocks, axis=0)


def _peer_topk_body(q_ref, keys_ref, e_ref, g_ref, *, tt):
    nt = (((1,), (1,)), ((), ()))
    es, gs = [], []
    for h0 in range(0, PEER_HEADS, 2):
        cands, experts = [], []
        for h in (h0, h0 + 1):
            scores = []
            for p in range(2):
                hp = 2 * h + p
                qhp = q_ref[:, hp * 128:(hp + 1) * 128].astype(BF16)
                scores.append(lax.dot_general(keys_ref[hp], qhp, nt, preferred_element_type=F32))
            (s1, i1), (s2, i2) = _topk_rows(scores, [None, None], PEER_TOPK)
            cands.append(_pair_candidates(s1, s2, lambda x, y: x + y, -jnp.inf))
            experts.append(_pair_candidates(i1, i2, lambda x, y: 2 * (x * N_KEYS + y), -1))
        for sc, e in _topk_rows(cands, experts, PEER_TOPK):
            ex = jnp.exp(sc - sc[0:1])
            es.append(e)
            gs.append(ex / jnp.sum(ex, axis=0, keepdims=True))
    e_ref[...] = jnp.concatenate(es, axis=0).T
    g_ref[...] = jnp.concatenate(gs, axis=0).T


def _peer_topk(q, sub_keys, *, tt=128):
    t, qw = q.shape
    keys = sub_keys.reshape(2 * PEER_HEADS, N_KEYS, qw // (2 * PEER_HEADS)).astype(BF16)
    return pl.pallas_call(
        functools.partial(_peer_topk_body, tt=tt),
        grid=(t // tt,),
        in_specs=[pl.BlockSpec((tt, qw), lambda i: (i, 0)),
                  pl.BlockSpec(keys.shape, lambda i: (0, 0, 0))],
        out_specs=[pl.BlockSpec((tt, PEER_SEL), lambda i: (i, 0)),
                   pl.BlockSpec((tt, PEER_SEL), lambda i: (i, 0))],
        out_shape=[jax.ShapeDtypeStruct((t, PEER_SEL), jnp.int32),
                   jax.ShapeDtypeStruct((t, PEER_SEL), F32)],
        compiler_params=_cparams("parallel"),
        name="peer_topk",
    )(q, keys)


SC_CORES = 2
SC_SUBCORES = 16
SC_LANES = 16
SC_TILES = SC_CORES * SC_SUBCORES
ROW_SUB = 8
ROW_LANE = 128
VECS_PER_SUB = ROW_LANE // SC_LANES
D_BLOCKS = 4
VECS_PER_BLOCK = ROW_SUB * VECS_PER_SUB // D_BLOCKS


def _sc_gelu(s):
    z = 0.7978845608028654 * (s + 0.044715 * (s * s * s))
    tanh_z = 1.0 - 2.0 / (jnp.exp(2.0 * z) + 1.0)
    return 0.5 * s * (1.0 + tanh_z)


def _vec_slot(jb, k):
    v = jb * VECS_PER_BLOCK + k
    return v // VECS_PER_SUB, pl.ds((v % VECS_PER_SUB) * SC_LANES, SC_LANES)


SC_TOKB = 8
PART_SUMS = 4
SC_RING = 3
SC_TOKEN_STEP = SC_TILES * SC_TOKB


def _peer_sc_body(e_hbm, g_hbm, x_hbm, uv_hbm, o_hbm, idx_v, idx1_v, g_v, x_v, out_v, acc_v, w_v, ubuf, vbuf,
                  sem_u, sem_v, sem_o, *, batches_per_tile, first_batch):
    wid = lax.axis_index("s") * SC_CORES + lax.axis_index("c")
    base = wid * batches_per_tile
    lane = lax.iota(jnp.int32, SC_LANES)
    n_groups = PEER_SEL // SC_LANES
    gpb = SC_TOKB * n_groups

    def sel16(ref, gi):
        return ref.at[gi // n_groups, pl.ds((gi % n_groups) * SC_LANES, SC_LANES)]

    def gather(gi, slot):
        return (pltpu.make_async_copy(uv_hbm.at[sel16(idx_v, gi)], ubuf.at[slot], sem_u.at[slot]),
                pltpu.make_async_copy(uv_hbm.at[sel16(idx1_v, gi)], vbuf.at[slot], sem_v.at[slot]))

    def start(gi, slot):
        for cp in gather(gi, slot):
            cp.start()

    def out_copy(batch, par):
        return pltpu.make_async_copy(out_v.at[par], o_hbm.at[batch], sem_o.at[par])

    def u_phase(slot, t, gi):
        for jb in range(D_BLOCKS):
            xs = []
            for k in range(VECS_PER_BLOCK):
                sub, ls = _vec_slot(jb, k)
                xs.append(x_v[sub, t, ls])

            @plsc.parallel_loop(0, SC_LANES, unroll=2)
            def _(r, jb=jb, xs=xs):
                parts = []
                per = VECS_PER_BLOCK // PART_SUMS
                for q in range(PART_SUMS):
                    p = xs[q * per] * ubuf[(slot, r) + _vec_slot(jb, q * per)]
                    for k in range(q * per + 1, (q + 1) * per):
                        p = p + xs[k] * ubuf[(slot, r) + _vec_slot(jb, k)]
                    parts.append(p)
                a = (parts[0] + parts[1]) + (parts[2] + parts[3])
                if jb > 0:
                    a = a + acc_v[r, :]
                acc_v[r, :] = a
        cols = [plsc.load_gather(acc_v, [lane, jnp.full((SC_LANES,), l, jnp.int32)]) for l in range(SC_LANES)]
        while len(cols) > 1:
            cols = [cols[i] + cols[i + 1] for i in range(0, len(cols), 2)]
        w_v[...] = sel16(g_v, gi)[...] * _sc_gelu(cols[0])

    def v_phase(slot, par, t):
        for jb in range(D_BLOCKS):
            slots = [_vec_slot(jb, k) for k in range(VECS_PER_BLOCK)]
            init = tuple(jnp.zeros((SC_LANES,), F32) for _ in slots)

            def add_row(r, os, jb=jb):
                wb = plsc.load_gather(w_v, [jnp.full((SC_LANES,), r, jnp.int32)])
                return tuple(os[k] + wb * vbuf[(slot, r) + _vec_slot(jb, k)] for k in range(VECS_PER_BLOCK))

            os = plsc.parallel_loop(0, SC_LANES, unroll=2, carry=init)(add_row)
            for (sub, ls), o in zip(slots, os):
                plsc.addupdate(out_v.at[par, sub, t, ls], o)

    @pl.loop(0, batches_per_tile)
    def _(bi):
        batch = base + bi
        src = first_batch + batch
        par = bi % 2
        pltpu.sync_copy(e_hbm.at[pl.ds(src * SC_TOKB, SC_TOKB)], idx_v)
        pltpu.sync_copy(g_hbm.at[pl.ds(src * SC_TOKB, SC_TOKB)], g_v)
        pltpu.sync_copy(x_hbm.at[src], x_v)
        for tt in range(SC_TOKB):
            for c in range(n_groups):
                cols = pl.ds(c * SC_LANES, SC_LANES)
                idx1_v[tt, cols] = idx_v[tt, cols] + 1
        for gi0 in range(SC_RING - 1):
            start(gi0, gi0)

        @pl.when(bi >= 2)
        def _():
            out_copy(batch, par).wait()

        def step(gi, slot):
            gi = jnp.asarray(gi, jnp.int32)
            t = gi // n_groups

            @pl.when(gi + SC_RING - 1 < gpb)
            def _():
                start(gi + SC_RING - 1, (slot + SC_RING - 1) % SC_RING)

            @pl.when(gi % n_groups == 0)
            def _():
                for j in range(ROW_SUB):
                    for l in range(VECS_PER_SUB):
                        out_v[par, j, t, pl.ds(l * SC_LANES, SC_LANES)] = jnp.zeros((SC_LANES,), F32)

            cu, cv = gather(gi, slot)
            cu.wait()
            u_phase(slot, t, gi)
            cv.wait()
            v_phase(slot, par, t)

        @pl.loop(0, gpb // SC_RING)
        def _(it):
            for slot in range(SC_RING):
                step(it * SC_RING + slot, slot)

        for gi in range(gpb - gpb % SC_RING, gpb):
            step(gi, gi % SC_RING)

        out_copy(batch, par).start()

    for par in range(2):
        out_copy(base, par).wait()


def _expert_rows(u, v):
    n_e = u.shape[0]
    rows = jnp.stack([u.reshape(n_e, ROW_SUB, ROW_LANE), v.reshape(n_e, ROW_SUB, ROW_LANE)], axis=1)
    return rows.reshape(2 * n_e, ROW_SUB, ROW_LANE)


def _peer_experts(e, g, xm, uv, first_token=0):
    t_all, d = xm.shape
    t = t_all - first_token
    assert d == ROW_SUB * ROW_LANE and e.shape == (t_all, PEER_SEL) and first_token % SC_TOKB == 0
    assert t % SC_TOKEN_STEP == 0 and t >= 2 * SC_TOKEN_STEP
    n_batches = t // SC_TOKB

    def tiled(a):
        return a.reshape(a.shape[0] // SC_TOKB, SC_TOKB, ROW_SUB, ROW_LANE).transpose(0, 2, 1, 3)

    call = pl.kernel(
        functools.partial(_peer_sc_body, batches_per_tile=n_batches // SC_TILES,
                          first_batch=first_token // SC_TOKB),
        out_type=jax.ShapeDtypeStruct((n_batches, ROW_SUB, SC_TOKB, ROW_LANE), F32),
        mesh=plsc.VectorSubcoreMesh(core_axis_name="c", subcore_axis_name="s"),
        scratch_types=[pltpu.VMEM((SC_TOKB, PEER_SEL), jnp.int32),
                       pltpu.VMEM((SC_TOKB, PEER_SEL), jnp.int32),
                       pltpu.VMEM((SC_TOKB, PEER_SEL), F32),
                       pltpu.VMEM((ROW_SUB, SC_TOKB, ROW_LANE), F32),
                       pltpu.VMEM((2, ROW_SUB, SC_TOKB, ROW_LANE), F32),
                       pltpu.VMEM((SC_LANES, SC_LANES), F32),
                       pltpu.VMEM((SC_LANES,), F32),
                       pltpu.VMEM((SC_RING, SC_LANES, ROW_SUB, ROW_LANE), F32),
                       pltpu.VMEM((SC_RING, SC_LANES, ROW_SUB, ROW_LANE), F32),
                       pltpu.SemaphoreType.DMA((SC_RING,)),
                       pltpu.SemaphoreType.DMA((SC_RING,)),
                       pltpu.SemaphoreType.DMA((2,))],
        compiler_params=pltpu.CompilerParams(needs_layout_passes=False),
        name="peer_experts_sc",
    )
    out = call(e, g, tiled(xm), uv)
    return out.transpose(0, 2, 1, 3).reshape(t, d)


PEER_TB = 8
HALF_WORDS = ROW_SUB * ROW_LANE // 2
WORD_SUB = HALF_WORDS // ROW_LANE


def _expert_rows_bf16(u, v):
    def pack(a):
        bits = lax.bitcast_convert_type(a.astype(BF16), jnp.uint16).astype(jnp.uint32)
        return lax.bitcast_convert_type((bits[:, HALF_WORDS:] << 16) | bits[:, :HALF_WORDS], jnp.int32)

    n_e = u.shape[0]
    return jnp.stack([pack(u), pack(v)], axis=1).reshape(n_e, 2 * WORD_SUB, ROW_LANE)


def _bf16_pair(words):
    low = lax.bitcast_convert_type(words << 16, F32)
    high = lax.bitcast_convert_type(words & jnp.int32(-65536), F32)
    return low, high


def _peer_tc_body(e_cur, e_nxt, g_ref, x_ref, uv_hbm, o_ref, buf, sem, *, n_steps):
    i = pl.program_id(0)
    slot = i % 2

    def issue(e_ref, dst):
        for t in range(PEER_TB):
            def one(k, carry, t=t):
                row = e_ref[t, k] >> 1
                pltpu.make_async_copy(uv_hbm.at[row], buf.at[dst, :, t * PEER_SEL + k], sem.at[dst]).start()
                return carry
            lax.fori_loop(0, PEER_SEL, one, 0, unroll=8)

    @pl.when(i == 0)
    def _():
        issue(e_cur, slot)

    @pl.when(i + 1 < n_steps)
    def _():
        issue(e_nxt, 1 - slot)

    pltpu.make_async_copy(buf.at[slot], buf.at[slot], sem.at[slot]).wait()

    gates = g_ref[...].T
    for t in range(PEER_TB):
        rows = slice(t * PEER_SEL, (t + 1) * PEER_SEL)
        acc = None
        for j in range(WORD_SUB):
            lo_cols = slice(j * ROW_LANE, (j + 1) * ROW_LANE)
            hi_cols = slice(HALF_WORDS + j * ROW_LANE, HALF_WORDS + (j + 1) * ROW_LANE)
            low, high = _bf16_pair(buf[slot, j, rows, :])
            term = low * x_ref[t:t + 1, lo_cols] + high * x_ref[t:t + 1, hi_cols]
            acc = term if acc is None else acc + term
        w = gates[:, t:t + 1] * _gelu(jnp.sum(acc, axis=-1, keepdims=True))
        for j in range(WORD_SUB):
            lo_cols = slice(j * ROW_LANE, (j + 1) * ROW_LANE)
            hi_cols = slice(HALF_WORDS + j * ROW_LANE, HALF_WORDS + (j + 1) * ROW_LANE)
            low, high = _bf16_pair(buf[slot, WORD_SUB + j, rows, :])
            o_ref[t:t + 1, lo_cols] = jnp.sum(w * low, axis=0, keepdims=True)
            o_ref[t:t + 1, hi_cols] = jnp.sum(w * high, axis=0, keepdims=True)


def _peer_experts_tc(e, g, xm, uvb, n_tokens):
    t, d = n_tokens, xm.shape[1]
    assert d == 2 * HALF_WORDS and t % PEER_TB == 0 and t <= xm.shape[0]
    n_steps = t // PEER_TB
    last = n_steps - 1
    n_rows = PEER_TB * PEER_SEL
    return pl.pallas_call(
        functools.partial(_peer_tc_body, n_steps=n_steps),
        grid=(n_steps,),
        in_specs=[pl.BlockSpec((PEER_TB, PEER_SEL), lambda i: (i, 0), memory_space=pltpu.SMEM),
                  pl.BlockSpec((PEER_TB, PEER_SEL), lambda i: (jnp.minimum(i + 1, last), 0), memory_space=pltpu.SMEM),
                  pl.BlockSpec((PEER_TB, PEER_SEL), lambda i: (i, 0)),
                  pl.BlockSpec((PEER_TB, d), lambda i: (i, 0)),
                  pl.BlockSpec(memory_space=pl.ANY)],
        out_specs=pl.BlockSpec((PEER_TB, d), lambda i: (i, 0)),
        out_shape=jax.ShapeDtypeStruct((t, d), F32),
        scratch_shapes=[pltpu.VMEM((2, 2 * WORD_SUB, n_rows, ROW_LANE), jnp.int32),
                        pltpu.SemaphoreType.DMA((2,))],
        compiler_params=_cparams("arbitrary"),
        name="peer_experts_tc",
    )(e, e, g, xm, uvb)


def _peer_select(q, sub_keys):
    return tuple(_peer_topk(q, sub_keys))


def _final_body(*refs, delta_starts):
    h_ref, *delta_refs, gate_ref, g_ref, o_ref = refs
    h = h_ref[...] + gate_ref[0] * _pick_piece(delta_refs, delta_starts)
    o_ref[...] = _rms(h) * g_ref[...]


def _final(h, delta, gate, gain, *, rows_per_group, tm=256):
    t, d = h.shape
    row = pl.BlockSpec((tm, d), lambda i: (i, 0))
    piece_specs, delta_starts = _piece_specs(delta, tm, d)
    return pl.pallas_call(
        functools.partial(_final_body, delta_starts=delta_starts),
        grid=(t // tm,),
        in_specs=[row, *piece_specs, pl.BlockSpec((1, 1, d), lambda i: ((i * tm) // rows_per_group, 0, 0)),
                  pl.BlockSpec((1, d), lambda i: (0, 0))],
        out_specs=row,
        out_shape=jax.ShapeDtypeStruct((t, d), F32),
        compiler_params=_cparams("parallel"),
        name="final_norm",
    )(h, *delta, gate, gain.reshape(1, d))


def kernel(x, c, ctx, c_ctx, w_mod, b_mod, attn_w_in, attn_w_out, lambda_q1, lambda_k1, lambda_q2, lambda_k2,
           subln_g, na_rpb, chunk_w_in, chunk_b_in, chunk_ln_g, chunk_ln_b, chunk_w_s, chunk_b_s, chunk_w_out,
           peer_w_query, peer_sub_keys, peer_u, peer_v, final_norm_g):
    b, n, d = x.shape
    n_ctx = ctx.shape[1]

    cond = jnp.concatenate([c, c_ctx[None], jnp.zeros((-(b + 1) % 8, d), F32)], axis=0)
    mods = [_adaln(cond, w_mod[i], b_mod[i]) for i in range(2)]
    m0, m1 = mods

    w_in = attn_w_in[0].astype(BF16)
    w_out = attn_w_out[0].astype(BF16)
    w_q0 = peer_w_query[0].astype(BF16)
    w_q1 = peer_w_query[1].astype(BF16)
    w_gin = chunk_w_in[0].astype(BF16)
    w_gout = chunk_w_out[0].astype(BF16)
    rope = _rope_tables(n, 512)
    bias_tabs = _na_bias_tables(na_rpb[0], n // GRID_W)
    lam_init = 0.8 - 0.6 * math.exp(-0.3 * 0)
    lam = (jnp.exp(jnp.sum(lambda_q1[0] * lambda_k1[0])) - jnp.exp(jnp.sum(lambda_q2[0] * lambda_k2[0]))
           + lam_init).astype(F32)
    ka = A_HEADS * 2 * A_QK_DIM

    def cx(m, k):
        return m[b:b + 1, k * d:(k + 1) * d].reshape(1, 1, d)

    sizes = PIPE_BATCHES if sum(PIPE_BATCHES) == b else (b,)
    starts = [sum(sizes[:ci]) for ci in range(len(sizes))]
    n_chunks = len(sizes)

    def rows(a, ci):
        return a[starts[ci]:starts[ci] + sizes[ci]]

    def lat(ci, m, k):
        return rows(m, ci)[:, k * d:(k + 1) * d].reshape(sizes[ci], 1, d)

    def stage_a(ci, x2):
        bc = sizes[ci]
        zc = _mod_mm(rows(ctx, ci).reshape(bc * n_ctx, d), cx(m0, 0), cx(m0, 1), w_in, rows_per_group=n_ctx,
                     tm=256, name="attn_in_ctx")[0]
        z = _mod_mm(x2, lat(ci, m0, 0), lat(ci, m0, 1), w_in, rows_per_group=n, tm=256, rope=rope,
                    rope_chunks=(0, 1), scale_chunks=(0, 3), name="attn_in")[0]
        z3 = z.reshape(bc, n, -1)
        zc3 = zc.reshape(bc, n_ctx, -1)
        oa = _diff_attn(z3, zc3, lam, subln_g[0], lam_init)
        ob = _na_attn(z3, zc3, bias_tabs)
        h1 = _out_proj(x2, lat(ci, m0, 2), oa.reshape(bc * n, ka), ob.reshape(bc * n, -1), w_out[:ka], w_out[ka:],
                       rows_per_group=n)
        q0, xm0 = _mod_mm(h1, lat(ci, m0, 3), lat(ci, m0, 4), w_q0, rows_per_group=n, tm=256, emit_xm=True,
                          name="peer_q0")
        return h1, xm0, _peer_select(q0, peer_sub_keys[0])

    def stage_b(ci, h1, p0):
        zg, h1b = _mod_mm(h1, lat(ci, m1, 0), lat(ci, m1, 1), w_gin, rows_per_group=n, tm=256, delta=p0,
                          gate=lat(ci, m0, 5), bias=chunk_b_in[0], act="gelu", name="gmlp_in")
        h2 = _gmlp_tail(zg, h1b, lat(ci, m1, 2), chunk_ln_g[0], chunk_ln_b[0], chunk_w_s[0], chunk_b_s[0], w_gout,
                        rows_per_group=n)
        q1, xm1 = _mod_mm(h2, lat(ci, m1, 3), lat(ci, m1, 4), w_q1, rows_per_group=n, tm=256, emit_xm=True,
                          name="peer_q1")
        return h2, xm1, _peer_select(q1, peer_sub_keys[1])

    tables = [(_expert_rows(peer_u[i], peer_v[i]), _expert_rows_bf16(peer_u[i], peer_v[i])) for i in range(2)]

    def experts(ci, sel, xm, tabs):
        e, g = sel
        t = xm.shape[0]
        k = PEER_TC_BLOCKS[ci] * SC_TOKEN_STEP if sizes == PIPE_BATCHES else 0
        if k == 0 or (t - k) % SC_TOKEN_STEP:
            return (_peer_experts(e, g, xm, tabs[0]),)
        p_sc = _peer_experts(e, g, xm, tabs[0], first_token=k)
        p_tc = _peer_experts_tc(e, g, xm, tabs[1], n_tokens=k)
        return p_tc, p_sc

    h1s, p0s = [], []
    prev_sel = None
    for ci in range(n_chunks):
        x2 = rows(x, ci).reshape(sizes[ci] * n, d)
        if prev_sel is not None:
            x2, prev_sel, tables = lax.optimization_barrier((x2, prev_sel, tables))
            p0s.append(experts(ci - 1, prev_sel, xm_prev, tables[0]))
        h1, xm_prev, prev_sel = stage_a(ci, x2)
        h1s.append(h1)
    p0s.append(experts(n_chunks - 1, prev_sel, xm_prev, tables[0]))

    h2s, p1s = [], []
    prev_sel = None
    for ci in range(n_chunks):
        p0 = p0s[ci]
        if prev_sel is not None:
            p0, prev_sel = lax.optimization_barrier((p0, prev_sel))
            p1s.append(experts(ci - 1, prev_sel, xm_prev, tables[1]))
        h2, xm_prev, prev_sel = stage_b(ci, h1s[ci], p0)
        h2s.append(h2)
    p1s.append(experts(n_chunks - 1, prev_sel, xm_prev, tables[1]))

    outs = [_final(h2s[ci], p1s[ci], lat(ci, m1, 5), final_norm_g, rows_per_group=n).reshape(sizes[ci], n, d)
            for ci in range(n_chunks)]
    return outs[0] if n_chunks == 1 else jnp.concatenate(outs, axis=0)
```

```python
import functools
import math

import numpy as np
import jax
import jax.numpy as jnp
from jax import lax
from jax.experimental import pallas as pl
from jax.experimental.pallas import tpu as pltpu
from jax.experimental.pallas import tpu_sc as plsc

F32 = jnp.float32
BF16 = jnp.bfloat16

EPS = 1e-6
GRID_W = 64
NA_ROWS = 8
NA_COLS = 16
ROPE_BASE = 10000.0
A_HEADS = 4
A_QK_DIM = 64
B_HEADS = 8
B_HEAD_DIM = 64
PEER_HEADS = 8
N_KEYS = 128
PEER_TOPK = 16
PEER_SEL = PEER_HEADS * PEER_TOPK
CHUNK = 128
C_GROUPS = 8
MASK_VALUE = -1e30

VMEM_LIMIT = 56 * 1024 * 1024
PIPE_BATCHES = (1, 2, 2, 3)
PEER_TC_BLOCKS = (5, 11, 11, 15)


def _cparams(*sem):
    return pltpu.CompilerParams(dimension_semantics=sem, vmem_limit_bytes=VMEM_LIMIT)


def _rms(x):
    return x * lax.rsqrt(jnp.mean(x * x, axis=-1, keepdims=True) + EPS)


def _gelu(x):
    return jax.nn.gelu(x)


def _adaln_body(c_ref, w_ref, b_ref, o_ref):
    cnd = c_ref[...]
    a = (cnd * jax.nn.sigmoid(cnd)).astype(BF16)
    o_ref[...] = jnp.dot(a, w_ref[...].astype(BF16), preferred_element_type=F32) + b_ref[...]


def _adaln(cond, w, b):
    m, d = cond.shape
    n = w.shape[1]
    tn = 1024
    return pl.pallas_call(
        _adaln_body,
        grid=(n // tn,),
        in_specs=[pl.BlockSpec((m, d), lambda j: (0, 0)),
                  pl.BlockSpec((d, tn), lambda j: (0, j)),
                  pl.BlockSpec((1, tn), lambda j: (0, j))],
        out_specs=pl.BlockSpec((m, tn), lambda j: (0, j)),
        out_shape=jax.ShapeDtypeStruct((m, n), F32),
        compiler_params=_cparams("parallel"),
        name="adaln",
    )(cond, w, b.reshape(1, n))


def _piece_specs(pieces, tm, d):
    specs, starts, b0 = [], [], 0
    for p in pieces:
        nb = p.shape[0] // tm
        assert nb * tm == p.shape[0]
        specs.append(pl.BlockSpec((tm, d), lambda i, b0=b0, nb=nb: (jnp.clip(i - b0, 0, nb - 1), 0)))
        starts.append(b0)
        b0 += nb
    return specs, tuple(starts)


def _pick_piece(refs, starts):
    i = pl.program_id(0)
    val = refs[0][...]
    for ref, start in zip(refs[1:], starts[1:]):
        val = jnp.where(i >= start, ref[...], val)
    return val


def _mod_mm_body(*refs, delta_starts, has_bias, rope_chunks, scale_chunks, act, emit_xm, tn, n_out):
    it = iter(refs)
    h_ref = next(it)
    has_delta = bool(delta_starts)
    delta_refs = gate_ref = None
    if has_delta:
        delta_refs = [next(it) for _ in delta_starts]
        gate_ref = next(it)
    shift_ref, scale_ref, w_ref = next(it), next(it), next(it)
    b_ref = next(it) if has_bias else None
    cos_ref = sin_ref = None
    if rope_chunks:
        cos_ref, sin_ref = next(it), next(it)
    y_ref = next(it)
    hn_ref = next(it) if has_delta else None
    xm_ref = next(it) if emit_xm else None

    h = h_ref[...]
    if has_delta:
        h = h + gate_ref[0] * _pick_piece(delta_refs, delta_starts)
        hn_ref[...] = h
    xm = _rms(h) * (1.0 + scale_ref[0]) + shift_ref[0]
    if emit_xm:
        xm_ref[...] = xm
    xb = xm.astype(BF16)
    for c in range(n_out // tn):
        cols = slice(c * tn, (c + 1) * tn)
        y = jnp.dot(xb, w_ref[:, cols], preferred_element_type=F32)
        if has_bias:
            y = y + b_ref[:, cols]
        if c in rope_chunks:
            lane = lax.broadcasted_iota(jnp.int32, y.shape, 1)
            partner = jnp.where((lane & 16) == 0, pltpu.roll(y, tn - 16, 1), pltpu.roll(y, 16, 1))
            y = y * cos_ref[...] + partner * sin_ref[...]
        if c in scale_chunks:
            y = y * 0.125
        if act == "gelu":
            y = _gelu(y)
        y_ref[:, cols] = y


def _mod_mm(h, shift, scale, w, *, rows_per_group, tm, delta=None, gate=None, bias=None, rope=None,
            rope_chunks=(), scale_chunks=(), act="none", emit_xm=False, name="mod_mm"):
    t, d = h.shape
    n_out = w.shape[1]
    tn = 512
    groups = shift.shape[0]
    assert t % tm == 0 and rows_per_group % tm == 0 and n_out % tn == 0

    def gidx(i):
        return ((i * tm) // rows_per_group if groups > 1 else 0, 0, 0)

    row_spec = pl.BlockSpec((tm, d), lambda i: (i, 0))
    mod_spec = pl.BlockSpec((1, 1, d), gidx)
    args, specs = [h], [row_spec]
    delta_starts = ()
    if delta is not None:
        piece_specs, delta_starts = _piece_specs(delta, tm, d)
        args += [*delta, gate]
        specs += [*piece_specs, mod_spec]
    args += [shift, scale, w]
    specs += [mod_spec, mod_spec, pl.BlockSpec((d, n_out), lambda i: (0, 0))]
    if bias is not None:
        args.append(bias.reshape(1, n_out))
        specs.append(pl.BlockSpec((1, n_out), lambda i: (0, 0)))
    if rope_chunks:
        cos, sin = rope
        pos_blocks = cos.shape[0] // tm
        args += [cos, sin]
        specs += [pl.BlockSpec((tm, tn), lambda i: (i % pos_blocks, 0))] * 2
    out_shape = [jax.ShapeDtypeStruct((t, n_out), F32)]
    out_specs = [pl.BlockSpec((tm, n_out), lambda i: (i, 0))]
    if delta is not None:
        out_shape.append(jax.ShapeDtypeStruct((t, d), F32))
        out_specs.append(row_spec)
    if emit_xm:
        out_shape.append(jax.ShapeDtypeStruct((t, d), F32))
        out_specs.append(row_spec)
    body = functools.partial(_mod_mm_body, delta_starts=delta_starts, has_bias=bias is not None,
                             rope_chunks=tuple(rope_chunks), scale_chunks=tuple(scale_chunks), act=act,
                             emit_xm=emit_xm, tn=tn, n_out=n_out)
    return pl.pallas_call(body, grid=(t // tm,), in_specs=specs, out_specs=out_specs, out_shape=out_shape,
                          compiler_params=_cparams("parallel"), name=name)(*args)


def _rope_tables(n_tokens, width):
    axis_dim = A_QK_DIM // 2
    inv_freq = 1.0 / (ROPE_BASE ** (jnp.arange(0, axis_dim, 2, dtype=F32) / axis_dim))
    t = jnp.arange(n_tokens)
    row = (t // GRID_W).astype(F32)
    col = (t % GRID_W).astype(F32)
    ang_r = row[:, None] * inv_freq
    ang_c = col[:, None] * inv_freq
    ang = jnp.concatenate([ang_r, ang_r, ang_c, ang_c], axis=1)
    sign = jnp.tile(jnp.concatenate([-jnp.ones(16, F32), jnp.ones(16, F32)]), 2)
    reps = width // A_QK_DIM
    return jnp.tile(jnp.cos(ang), (1, reps)), jnp.tile(jnp.sin(ang) * sign, (1, reps))


def _diff_attn_body(lam_ref, q_ref, k_ref, v_ref, kc_ref, vc_ref, g_ref, o_ref, *, out_scale):
    lam = lam_ref[0, 0]
    q = q_ref[0]
    k = k_ref[0].astype(BF16)
    kc = kc_ref[0].astype(BF16)
    lane = lax.broadcasted_iota(jnp.int32, q.shape, 1)
    nt = (((1,), (1,)), ((), ()))
    parts = []
    for m in range(2):
        keep = (lane < A_QK_DIM) if m == 0 else (lane >= A_QK_DIM)
        qm = jnp.where(keep, q, 0.0).astype(BF16)
        s_l = lax.dot_general(qm, k, nt, preferred_element_type=F32)
        s_c = lax.dot_general(qm, kc, nt, preferred_element_type=F32)
        mx = jnp.maximum(jnp.max(s_l, axis=-1, keepdims=True), jnp.max(s_c, axis=-1, keepdims=True))
        e_l = jnp.exp(s_l - mx)
        e_c = jnp.exp(s_c - mx)
        inv = 1.0 / (jnp.sum(e_l, axis=-1, keepdims=True) + jnp.sum(e_c, axis=-1, keepdims=True))
        parts.append((e_l, e_c, inv))
    (e1l, e1c, inv1), (e2l, e2c, inv2) = parts
    w2 = lam * inv2
    a_l = (e1l * inv1 - e2l * w2).astype(BF16)
    a_c = (e1c * inv1 - e2c * w2).astype(BF16)
    o = (jnp.dot(a_l, v_ref[0].astype(BF16), preferred_element_type=F32)
         + jnp.dot(a_c, vc_ref[0].astype(BF16), preferred_element_type=F32))
    o_ref[0] = _rms(o) * g_ref[...] * out_scale


def _diff_attn(z, zc, lam, subln_g, lam_init, *, tq=256):
    b, n, _ = z.shape
    c = zc.shape[1]
    kcol, vcol = A_HEADS, 2 * A_HEADS
    return pl.pallas_call(
        functools.partial(_diff_attn_body, out_scale=1.0 - lam_init),
        grid=(b, A_HEADS, n // tq),
        in_specs=[pl.BlockSpec((1, 1), lambda bi, h, i: (0, 0), memory_space=pltpu.SMEM),
                  pl.BlockSpec((1, tq, 128), lambda bi, h, i: (bi, i, h)),
                  pl.BlockSpec((1, n, 128), lambda bi, h, i: (bi, 0, kcol + h)),
                  pl.BlockSpec((1, n, 128), lambda bi, h, i: (bi, 0, vcol + h)),
                  pl.BlockSpec((1, c, 128), lambda bi, h, i: (bi, 0, kcol + h)),
                  pl.BlockSpec((1, c, 128), lambda bi, h, i: (bi, 0, vcol + h)),
                  pl.BlockSpec((1, 128), lambda bi, h, i: (0, 0))],
        out_specs=pl.BlockSpec((1, tq, 128), lambda bi, h, i: (bi, i, h)),
        out_shape=jax.ShapeDtypeStruct((b, n, A_HEADS * 128), F32),
        compiler_params=_cparams("parallel", "parallel", "arbitrary"),
        name="diff_attn",
    )(lam.reshape(1, 1), z, z, z, zc, zc, subln_g.reshape(1, 128))


NA_QROWS = 8
NA_KROWS = 16
NA_SEG = 4 * GRID_W


def _na_window_start(j, rows):
    return jnp.clip(NA_QROWS * j - NA_ROWS // 2, 0, rows - NA_KROWS)


def _na_bias_tables(rpb, rows):
    n_blocks = rows // NA_QROWS
    h = rpb.shape[0]
    ic = np.clip(np.arange(GRID_W)[None, :] - np.arange(GRID_W)[:, None] + (NA_COLS - 1), 0, 2 * NA_COLS - 2)
    toep = jnp.pad(rpb[:, :, ic], ((0, 0), (NA_KROWS, NA_KROWS), (0, 0), (0, 0)))
    tabs, oks = [], []
    for j in (0, 1, n_blocks - 1):
        ks = int(np.clip(NA_QROWS * j - NA_ROWS // 2, 0, rows - NA_KROWS))
        slabs = []
        for rq in range(NA_QROWS):
            dr0 = ks - (NA_QROWS * j + rq) + (NA_ROWS - 1) + NA_KROWS
            slab = toep[:, dr0:dr0 + NA_KROWS]
            slabs.append(slab.transpose(0, 2, 1, 3).reshape(h, GRID_W, NA_KROWS * GRID_W))
        tabs.append(jnp.concatenate(slabs, axis=1))
        r = (NA_QROWS * j + np.arange(NA_QROWS))[:, None, None, None]
        cq = np.arange(GRID_W)[None, :, None, None]
        kr = (ks + np.arange(NA_KROWS))[None, None, :, None]
        kc = np.arange(GRID_W)[None, None, None, :]
        r0 = np.clip(r - NA_ROWS // 2, 0, rows - NA_ROWS)
        c0 = np.clip(cq - NA_COLS // 2, 0, GRID_W - NA_COLS)
        ok = (kr >= r0) & (kr < r0 + NA_ROWS) & (kc >= c0) & (kc < c0 + NA_COLS)
        oks.append(ok.reshape(NA_QROWS * GRID_W, NA_KROWS * GRID_W))
    return jnp.where(np.stack(oks)[None], jnp.stack(tabs, axis=1), MASK_VALUE)


def _na_body(q_ref, k0, k1, k2, k3, v0, v1, v2, v3, kc_ref, vc_ref, bias_ref, o_ref):
    q = q_ref[0]
    lane = lax.broadcasted_iota(jnp.int32, q.shape, 1)
    nt = (((1,), (1,)), ((), ()))
    ks = [r[0].astype(BF16) for r in (k0, k1, k2, k3)]
    vs = [r[0].astype(BF16) for r in (v0, v1, v2, v3)]
    kc = kc_ref[0].astype(BF16)
    vc = vc_ref[0].astype(BF16)
    outs = []
    for hh in range(2):
        keep = (lane < B_HEAD_DIM) if hh == 0 else (lane >= B_HEAD_DIM)
        qh = jnp.where(keep, q, 0.0).astype(BF16)
        segs = [lax.dot_general(qh, ks[i], nt, preferred_element_type=F32)
                + bias_ref[hh, 0, :, i * NA_SEG:(i + 1) * NA_SEG] for i in range(4)]
        s_c = lax.dot_general(qh, kc, nt, preferred_element_type=F32)
        mx = jnp.max(s_c, axis=-1, keepdims=True)
        for s in segs:
            mx = jnp.maximum(mx, jnp.max(s, axis=-1, keepdims=True))
        e_c = jnp.exp(s_c - mx)
        den = jnp.sum(e_c, axis=-1, keepdims=True)
        es = []
        for s in segs:
            e = jnp.exp(s - mx)
            den = den + jnp.sum(e, axis=-1, keepdims=True)
            es.append(e)
        inv = 1.0 / den
        o = jnp.dot((e_c * inv).astype(BF16), vc, preferred_element_type=F32)
        for e, v in zip(es, vs):
            o = o + jnp.dot((e * inv).astype(BF16), v, preferred_element_type=F32)
        outs.append(o)
    o_ref[0] = jnp.where(lane < B_HEAD_DIM, outs[0], outs[1])


def _na_attn(z, zc, bias_tabs):
    b, n, _ = z.shape
    c = zc.shape[1]
    rows = n // GRID_W
    n_blocks = rows // NA_QROWS
    tq = NA_QROWS * GRID_W
    qcol, kcol, vcol = 12, 16, 20
    seg_rows = NA_SEG // GRID_W

    def kv_spec(col0, i):
        return pl.BlockSpec((1, NA_SEG, 128),
                            lambda hp, j, bi: (bi, _na_window_start(j, rows) // seg_rows + i, col0 + hp))

    def variant(j):
        return jnp.where(j == 0, 0, jnp.where(j == n_blocks - 1, 2, 1))

    in_specs = [pl.BlockSpec((1, tq, 128), lambda hp, j, bi: (bi, j, qcol + hp))]
    in_specs += [kv_spec(kcol, i) for i in range(4)]
    in_specs += [kv_spec(vcol, i) for i in range(4)]
    in_specs += [pl.BlockSpec((1, c, 128), lambda hp, j, bi: (bi, 0, kcol + hp)),
                 pl.BlockSpec((1, c, 128), lambda hp, j, bi: (bi, 0, vcol + hp)),
                 pl.BlockSpec((2, 1, tq, NA_KROWS * GRID_W), lambda hp, j, bi: (hp, variant(j), 0, 0))]
    return pl.pallas_call(
        _na_body,
        grid=(B_HEADS // 2, n_blocks, b),
        in_specs=in_specs,
        out_specs=pl.BlockSpec((1, tq, 128), lambda hp, j, bi: (bi, j, hp)),
        out_shape=jax.ShapeDtypeStruct((b, n, B_HEADS * B_HEAD_DIM), F32),
        compiler_params=_cparams("parallel", "parallel", "arbitrary"),
        name="na_attn",
    )(z, z, z, z, z, z, z, z, z, zc, zc, bias_tabs)


def _out_proj_body(h_ref, gate_ref, xa_ref, xb_ref, wa_ref, wb_ref, o_ref):
    y = (jnp.dot(xa_ref[...].astype(BF16), wa_ref[...], preferred_element_type=F32)
         + jnp.dot(xb_ref[...].astype(BF16), wb_ref[...], preferred_element_type=F32))
    o_ref[...] = h_ref[...] + gate_ref[0] * y


def _out_proj(h, gate, xa, xb, wa, wb, *, rows_per_group, tm=512):
    t, d = h.shape
    ka, kb = xa.shape[1], xb.shape[1]
    return pl.pallas_call(
        _out_proj_body,
        grid=(t // tm,),
        in_specs=[pl.BlockSpec((tm, d), lambda i: (i, 0)),
                  pl.BlockSpec((1, 1, d), lambda i: ((i * tm) // rows_per_group, 0, 0)),
                  pl.BlockSpec((tm, ka), lambda i: (i, 0)),
                  pl.BlockSpec((tm, kb), lambda i: (i, 0)),
                  pl.BlockSpec((ka, d), lambda i: (0, 0)),
                  pl.BlockSpec((kb, d), lambda i: (0, 0))],
        out_specs=pl.BlockSpec((tm, d), lambda i: (i, 0)),
        out_shape=jax.ShapeDtypeStruct((t, d), F32),
        compiler_params=_cparams("parallel"),
        name="out_proj",
    )(h, gate, xa, xb, wa, wb)


def _gmlp_tail_body(z_ref, h_ref, gate_ref, lng_ref, lnb_ref, ws_ref, bs_ref, wo_ref, o_ref, gated_ref, *, tm, width):
    gw = width // C_GROUPS
    for ci in range(tm // CHUNK):
        rows = slice(ci * CHUNK, (ci + 1) * CHUNK)
        v = z_ref[rows, width:]
        mu = jnp.mean(v, axis=-1, keepdims=True)
        var = jnp.mean(jnp.square(v - mu), axis=-1, keepdims=True)
        vn = ((v - mu) * lax.rsqrt(var + EPS)) * lng_ref[...] + lnb_ref[...]
        vb = vn.astype(BF16)
        for g in range(C_GROUPS):
            cols = slice(g * gw, (g + 1) * gw)
            sv = jnp.dot(ws_ref[g], vb[:, cols], preferred_element_type=F32) + bs_ref[g]
            gated_ref[rows, cols] = (z_ref[rows, cols] * sv).astype(BF16)
    y = jnp.dot(gated_ref[...], wo_ref[...], preferred_element_type=F32)
    o_ref[...] = h_ref[...] + gate_ref[0] * y


def _gmlp_tail(z, h, gate, ln_g, ln_b, w_s, b_s, w_out, *, rows_per_group, tm=512):
    t, d = h.shape
    width = z.shape[1] // 2
    gw = width // C_GROUPS
    bsb = jnp.broadcast_to(b_s[:, :, None], (C_GROUPS, CHUNK, gw))
    return pl.pallas_call(
        functools.partial(_gmlp_tail_body, tm=tm, width=width),
        grid=(t // tm,),
        in_specs=[pl.BlockSpec((tm, 2 * width), lambda i: (i, 0)),
                  pl.BlockSpec((tm, d), lambda i: (i, 0)),
                  pl.BlockSpec((1, 1, d), lambda i: ((i * tm) // rows_per_group, 0, 0)),
                  pl.BlockSpec((1, width), lambda i: (0, 0)),
                  pl.BlockSpec((1, width), lambda i: (0, 0)),
                  pl.BlockSpec((C_GROUPS, CHUNK, CHUNK), lambda i: (0, 0, 0)),
                  pl.BlockSpec((C_GROUPS, CHUNK, gw), lambda i: (0, 0, 0)),
                  pl.BlockSpec((width, d), lambda i: (0, 0))],
        out_specs=pl.BlockSpec((tm, d), lambda i: (i, 0)),
        out_shape=jax.ShapeDtypeStruct((t, d), F32),
        scratch_shapes=[pltpu.VMEM((tm, width), BF16)],
        compiler_params=_cparams("parallel"),
        name="gmlp_tail",
    )(z, h, gate, ln_g.reshape(1, width), ln_b.reshape(1, width), w_s.astype(BF16), bsb, w_out)


def _topk_rows(scores, payloads, k):
    n_rows = scores[0].shape[0]
    riota = lax.broadcasted_iota(jnp.int32, scores[0].shape, 0)
    scores = list(scores)
    vals = [[] for _ in scores]
    pays = [[] for _ in scores]
    for _ in range(k):
        for c, (s, payload) in enumerate(zip(scores, payloads)):
            m = jnp.max(s, axis=0, keepdims=True)
            first = jnp.min(jnp.where(s == m, riota, n_rows), axis=0, keepdims=True)
            sel = riota == first
            vals[c].append(m)
            pays[c].append(first if payload is None
                           else jnp.max(jnp.where(sel, payload, -1), axis=0, keepdims=True))
            scores[c] = jnp.where(sel, -jnp.inf, s)
    return [(jnp.concatenate(v, axis=0), jnp.concatenate(p, axis=0)) for v, p in zip(vals, pays)]


def _pair_candidates(x1, x2, combine, fill):
    k = PEER_TOPK
    sub = lax.broadcasted_iota(jnp.int32, (8, x1.shape[1]), 0)
    blocks = [combine(x1[0:1], x2)]
    for a in range(1, 8):
        blocks.append(jnp.where(sub < k // (a + 1), combine(x1[a:a + 1], x2[0:8]), fill))
    blocks.append(combine(x1[8:k], x2[0:1]))
    return jnp.concatenate(blocks, axis=0)


def _peer_topk_body(q_ref, keys_ref, e_ref, g_ref, *, tt):
    nt = (((1,), (1,)), ((), ()))
    es, gs = [], []
    for h0 in range(0, PEER_HEADS, 2):
        cands, experts = [], []
        for h in (h0, h0 + 1):
            scores = []
            for p in range(2):
                hp = 2 * h + p
                qhp = q_ref[:, hp * 128:(hp + 1) * 128].astype(BF16)
                scores.append(lax.dot_general(keys_ref[hp], qhp, nt, preferred_element_type=F32))
            (s1, i1), (s2, i2) = _topk_rows(scores, [None, None], PEER_TOPK)
            cands.append(_pair_candidates(s1, s2, lambda x, y: x + y, -jnp.inf))
            experts.append(_pair_candidates(i1, i2, lambda x, y: 2 * (x * N_KEYS + y), -1))
        for sc, e in _topk_rows(cands, experts, PEER_TOPK):
            ex = jnp.exp(sc - sc[0:1])
            es.append(e)
            gs.append(ex / jnp.sum(ex, axis=0, keepdims=True))
    e_ref[...] = jnp.concatenate(es, axis=0).T
    g_ref[...] = jnp.concatenate(gs, axis=0).T


def _peer_topk(q, sub_keys, *, tt=128):
    t, qw = q.shape
    keys = sub_keys.reshape(2 * PEER_HEADS, N_KEYS, qw // (2 * PEER_HEADS)).astype(BF16)
    return pl.pallas_call(
        functools.partial(_peer_topk_body, tt=tt),
        grid=(t // tt,),
        in_specs=[pl.BlockSpec((tt, qw), lambda i: (i, 0)),
                  pl.BlockSpec(keys.shape, lambda i: (0, 0, 0))],
        out_specs=[pl.BlockSpec((tt, PEER_SEL), lambda i: (i, 0)),
                   pl.BlockSpec((tt, PEER_SEL), lambda i: (i, 0))],
        out_shape=[jax.ShapeDtypeStruct((t, PEER_SEL), jnp.int32),
                   jax.ShapeDtypeStruct((t, PEER_SEL), F32)],
        compiler_params=_cparams("parallel"),
        name="peer_topk",
    )(q, keys)


SC_CORES = 2
SC_SUBCORES = 16
SC_LANES = 16
SC_TILES = SC_CORES * SC_SUBCORES
ROW_SUB = 8
ROW_LANE = 128
VECS_PER_SUB = ROW_LANE // SC_LANES
D_BLOCKS = 4
VECS_PER_BLOCK = ROW_SUB * VECS_PER_SUB // D_BLOCKS


def _sc_gelu(s):
    z = 0.7978845608028654 * (s + 0.044715 * (s * s * s))
    tanh_z = 1.0 - 2.0 / (jnp.exp(2.0 * z) + 1.0)
    return 0.5 * s * (1.0 + tanh_z)


def _vec_slot(jb, k):
    v = jb * VECS_PER_BLOCK + k
    return v // VECS_PER_SUB, pl.ds((v % VECS_PER_SUB) * SC_LANES, SC_LANES)


SC_TOKB = 8
PART_SUMS = 4
SC_RING = 3
SC_TOKEN_STEP = SC_TILES * SC_TOKB


def _peer_sc_body(e_hbm, g_hbm, x_hbm, uv_hbm, o_hbm, idx_v, idx1_v, g_v, x_v, out_v, acc_v, w_v, ubuf, vbuf,
                  sem_u, sem_v, sem_o, *, batches_per_tile, first_batch):
    wid = lax.axis_index("s") * SC_CORES + lax.axis_index("c")
    base = wid * batches_per_tile
    lane = lax.iota(jnp.int32, SC_LANES)
    n_groups = PEER_SEL // SC_LANES
    gpb = SC_TOKB * n_groups

    def sel16(ref, gi):
        return ref.at[gi // n_groups, pl.ds((gi % n_groups) * SC_LANES, SC_LANES)]

    def gather(gi, slot):
        return (pltpu.make_async_copy(uv_hbm.at[sel16(idx_v, gi)], ubuf.at[slot], sem_u.at[slot]),
                pltpu.make_async_copy(uv_hbm.at[sel16(idx1_v, gi)], vbuf.at[slot], sem_v.at[slot]))

    def start(gi, slot):
        for cp in gather(gi, slot):
            cp.start()

    def out_copy(batch, par):
        return pltpu.make_async_copy(out_v.at[par], o_hbm.at[batch], sem_o.at[par])

    def u_phase(slot, t, gi):
        for jb in range(D_BLOCKS):
            xs = []
            for k in range(VECS_PER_BLOCK):
                sub, ls = _vec_slot(jb, k)
                xs.append(x_v[sub, t, ls])

            @plsc.parallel_loop(0, SC_LANES, unroll=2)
            def _(r, jb=jb, xs=xs):
                parts = []
                per = VECS_PER_BLOCK // PART_SUMS
                for q in range(PART_SUMS):
                    p = xs[q * per] * ubuf[(slot, r) + _vec_slot(jb, q * per)]
                    for k in range(q * per + 1, (q + 1) * per):
                        p = p + xs[k] * ubuf[(slot, r) + _vec_slot(jb, k)]
                    parts.append(p)
                a = (parts[0] + parts[1]) + (parts[2] + parts[3])
                if jb > 0:
                    a = a + acc_v[r, :]
                acc_v[r, :] = a
        cols = [plsc.load_gather(acc_v, [lane, jnp.full((SC_LANES,), l, jnp.int32)]) for l in range(SC_LANES)]
        while len(cols) > 1:
            cols = [cols[i] + cols[i + 1] for i in range(0, len(cols), 2)]
        w_v[...] = sel16(g_v, gi)[...] * _sc_gelu(cols[0])

    def v_phase(slot, par, t):
        for jb in range(D_BLOCKS):
            slots = [_vec_slot(jb, k) for k in range(VECS_PER_BLOCK)]
            init = tuple(jnp.zeros((SC_LANES,), F32) for _ in slots)

            def add_row(r, os, jb=jb):
                wb = plsc.load_gather(w_v, [jnp.full((SC_LANES,), r, jnp.int32)])
                return tuple(os[k] + wb * vbuf[(slot, r) + _vec_slot(jb, k)] for k in range(VECS_PER_BLOCK))

            os = plsc.parallel_loop(0, SC_LANES, unroll=2, carry=init)(add_row)
            for (sub, ls), o in zip(slots, os):
                plsc.addupdate(out_v.at[par, sub, t, ls], o)

    @pl.loop(0, batches_per_tile)
    def _(bi):
        batch = base + bi
        src = first_batch + batch
        par = bi % 2
        pltpu.sync_copy(e_hbm.at[pl.ds(src * SC_TOKB, SC_TOKB)], idx_v)
        pltpu.sync_copy(g_hbm.at[pl.ds(src * SC_TOKB, SC_TOKB)], g_v)
        pltpu.sync_copy(x_hbm.at[src], x_v)
        for tt in range(SC_TOKB):
            for c in range(n_groups):
                cols = pl.ds(c * SC_LANES, SC_LANES)
                idx1_v[tt, cols] = idx_v[tt, cols] + 1
        for gi0 in range(SC_RING - 1):
            start(gi0, gi0)

        @pl.when(bi >= 2)
        def _():
            out_copy(batch, par).wait()

        def step(gi, slot):
            gi = jnp.asarray(gi, jnp.int32)
            t = gi // n_groups

            @pl.when(gi + SC_RING - 1 < gpb)
            def _():
                start(gi + SC_RING - 1, (slot + SC_RING - 1) % SC_RING)

            @pl.when(gi % n_groups == 0)
            def _():
                for j in range(ROW_SUB):
                    for l in range(VECS_PER_SUB):
                        out_v[par, j, t, pl.ds(l * SC_LANES, SC_LANES)] = jnp.zeros((SC_LANES,), F32)

            cu, cv = gather(gi, slot)
            cu.wait()
            u_phase(slot, t, gi)
            cv.wait()
            v_phase(slot, par, t)

        @pl.loop(0, gpb // SC_RING)
        def _(it):
            for slot in range(SC_RING):
                step(it * SC_RING + slot, slot)

        for gi in range(gpb - gpb % SC_RING, gpb):
            step(gi, gi % SC_RING)

        out_copy(batch, par).start()

    for par in range(2):
        out_copy(base, par).wait()


def _expert_rows(u, v):
    n_e = u.shape[0]
    rows = jnp.stack([u.reshape(n_e, ROW_SUB, ROW_LANE), v.reshape(n_e, ROW_SUB, ROW_LANE)], axis=1)
    return rows.reshape(2 * n_e, ROW_SUB, ROW_LANE)


def _peer_experts(e, g, xm, uv, first_token=0):
    t_all, d = xm.shape
    t = t_all - first_token
    assert d == ROW_SUB * ROW_LANE and e.shape == (t_all, PEER_SEL) and first_token % SC_TOKB == 0
    assert t % SC_TOKEN_STEP == 0 and t >= 2 * SC_TOKEN_STEP
    n_batches = t // SC_TOKB

    def tiled(a):
        return a.reshape(a.shape[0] // SC_TOKB, SC_TOKB, ROW_SUB, ROW_LANE).transpose(0, 2, 1, 3)

    call = pl.kernel(
        functools.partial(_peer_sc_body, batches_per_tile=n_batches // SC_TILES,
                          first_batch=first_token // SC_TOKB),
        out_type=jax.ShapeDtypeStruct((n_batches, ROW_SUB, SC_TOKB, ROW_LANE), F32),
        mesh=plsc.VectorSubcoreMesh(core_axis_name="c", subcore_axis_name="s"),
        scratch_types=[pltpu.VMEM((SC_TOKB, PEER_SEL), jnp.int32),
                       pltpu.VMEM((SC_TOKB, PEER_SEL), jnp.int32),
                       pltpu.VMEM((SC_TOKB, PEER_SEL), F32),
                       pltpu.VMEM((ROW_SUB, SC_TOKB, ROW_LANE), F32),
                       pltpu.VMEM((2, ROW_SUB, SC_TOKB, ROW_LANE), F32),
                       pltpu.VMEM((SC_LANES, SC_LANES), F32),
                       pltpu.VMEM((SC_LANES,), F32),
                       pltpu.VMEM((SC_RING, SC_LANES, ROW_SUB, ROW_LANE), F32),
                       pltpu.VMEM((SC_RING, SC_LANES, ROW_SUB, ROW_LANE), F32),
                       pltpu.SemaphoreType.DMA((SC_RING,)),
                       pltpu.SemaphoreType.DMA((SC_RING,)),
                       pltpu.SemaphoreType.DMA((2,))],
        compiler_params=pltpu.CompilerParams(needs_layout_passes=False),
        name="peer_experts_sc",
    )
    out = call(e, g, tiled(xm), uv)
    return out.transpose(0, 2, 1, 3).reshape(t, d)


PEER_TB = 8
DMA_PRIORITIES = 2
HALF_WORDS = ROW_SUB * ROW_LANE // 2
WORD_SUB = HALF_WORDS // ROW_LANE


def _expert_rows_bf16(u, v):
    def pack(a):
        bits = lax.bitcast_convert_type(a.astype(BF16), jnp.uint16).astype(jnp.uint32)
        return lax.bitcast_convert_type((bits[:, HALF_WORDS:] << 16) | bits[:, :HALF_WORDS], jnp.int32)

    n_e = u.shape[0]
    return jnp.stack([pack(u), pack(v)], axis=1).reshape(n_e, 2 * WORD_SUB, ROW_LANE)


def _bf16_pair(words):
    low = lax.bitcast_convert_type(words << 16, F32)
    high = lax.bitcast_convert_type(words & jnp.int32(-65536), F32)
    return low, high


def _peer_tc_body(e_cur, e_nxt, g_ref, x_ref, uv_hbm, o_ref, buf, sem, *, n_steps):
    i = pl.program_id(0)
    slot = i % 2

    def issue(e_ref, dst):
        for t in range(PEER_TB):
            def two(k2, carry, t=t):
                for prio in range(DMA_PRIORITIES):
                    k = DMA_PRIORITIES * k2 + prio
                    row = e_ref[t, k] >> 1
                    pltpu.make_async_copy(uv_hbm.at[row], buf.at[dst, :, t * PEER_SEL + k],
                                          sem.at[dst]).start(priority=prio)
                return carry
            lax.fori_loop(0, PEER_SEL // DMA_PRIORITIES, two, 0, unroll=4)

    @pl.when(i == 0)
    def _():
        issue(e_cur, slot)

    @pl.when(i + 1 < n_steps)
    def _():
        issue(e_nxt, 1 - slot)

    pltpu.make_async_copy(buf.at[slot], buf.at[slot], sem.at[slot]).wait()

    gates = g_ref[...].T
    for t in range(PEER_TB):
        rows = slice(t * PEER_SEL, (t + 1) * PEER_SEL)
        acc = None
        for j in range(WORD_SUB):
            lo_cols = slice(j * ROW_LANE, (j + 1) * ROW_LANE)
            hi_cols = slice(HALF_WORDS + j * ROW_LANE, HALF_WORDS + (j + 1) * ROW_LANE)
            low, high = _bf16_pair(buf[slot, j, rows, :])
            term = low * x_ref[t:t + 1, lo_cols] + high * x_ref[t:t + 1, hi_cols]
            acc = term if acc is None else acc + term
        w = gates[:, t:t + 1] * _gelu(jnp.sum(acc, axis=-1, keepdims=True))
        for j in range(WORD_SUB):
            lo_cols = slice(j * ROW_LANE, (j + 1) * ROW_LANE)
            hi_cols = slice(HALF_WORDS + j * ROW_LANE, HALF_WORDS + (j + 1) * ROW_LANE)
            low, high = _bf16_pair(buf[slot, WORD_SUB + j, rows, :])
            o_ref[t:t + 1, lo_cols] = jnp.sum(w * low, axis=0, keepdims=True)
            o_ref[t:t + 1, hi_cols] = jnp.sum(w * high, axis=0, keepdims=True)


def _peer_experts_tc(e, g, xm, uvb, n_tokens):
    t, d = n_tokens, xm.shape[1]
    assert d == 2 * HALF_WORDS and t % PEER_TB == 0 and t <= xm.shape[0]
    n_steps = t // PEER_TB
    last = n_steps - 1
    n_rows = PEER_TB * PEER_SEL
    return pl.pallas_call(
        functools.partial(_peer_tc_body, n_steps=n_steps),
        grid=(n_steps,),
        in_specs=[pl.BlockSpec((PEER_TB, PEER_SEL), lambda i: (i, 0), memory_space=pltpu.SMEM),
                  pl.BlockSpec((PEER_TB, PEER_SEL), lambda i: (jnp.minimum(i + 1, last), 0), memory_space=pltpu.SMEM),
                  pl.BlockSpec((PEER_TB, PEER_SEL), lambda i: (i, 0)),
                  pl.BlockSpec((PEER_TB, d), lambda i: (i, 0)),
                  pl.BlockSpec(memory_space=pl.ANY)],
        out_specs=pl.BlockSpec((PEER_TB, d), lambda i: (i, 0)),
        out_shape=jax.ShapeDtypeStruct((t, d), F32),
        scratch_shapes=[pltpu.VMEM((2, 2 * WORD_SUB, n_rows, ROW_LANE), jnp.int32),
                        pltpu.SemaphoreType.DMA((2,))],
        compiler_params=_cparams("arbitrary"),
        name="peer_experts_tc",
    )(e, e, g, xm, uvb)


def _peer_select(q, sub_keys):
    return tuple(_peer_topk(q, sub_keys))


def _final_body(*refs, delta_starts):
    h_ref, *delta_refs, gate_ref, g_ref, o_ref = refs
    h = h_ref[...] + gate_ref[0] * _pick_piece(delta_refs, delta_starts)
    o_ref[...] = _rms(h) * g_ref[...]


def _final(h, delta, gate, gain, *, rows_per_group, tm=256):
    t, d = h.shape
    row = pl.BlockSpec((tm, d), lambda i: (i, 0))
    piece_specs, delta_starts = _piece_specs(delta, tm, d)
    return pl.pallas_call(
        functools.partial(_final_body, delta_starts=delta_starts),
        grid=(t // tm,),
        in_specs=[row, *piece_specs, pl.BlockSpec((1, 1, d), lambda i: ((i * tm) // rows_per_group, 0, 0)),
                  pl.BlockSpec((1, d), lambda i: (0, 0))],
        out_specs=row,
        out_shape=jax.ShapeDtypeStruct((t, d), F32),
        compiler_params=_cparams("parallel"),
        name="final_norm",
    )(h, *delta, gate, gain.reshape(1, d))


def kernel(x, c, ctx, c_ctx, w_mod, b_mod, attn_w_in, attn_w_out, lambda_q1, lambda_k1, lambda_q2, lambda_k2,
           subln_g, na_rpb, chunk_w_in, chunk_b_in, chunk_ln_g, chunk_ln_b, chunk_w_s, chunk_b_s, chunk_w_out,
           peer_w_query, peer_sub_keys, peer_u, peer_v, final_norm_g):
    b, n, d = x.shape
    n_ctx = ctx.shape[1]

    cond = jnp.concatenate([c, c_ctx[None], jnp.zeros((-(b + 1) % 8, d), F32)], axis=0)
    mods = [_adaln(cond, w_mod[i], b_mod[i]) for i in range(2)]
    m0, m1 = mods

    w_in = attn_w_in[0].astype(BF16)
    w_out = attn_w_out[0].astype(BF16)
    w_q0 = peer_w_query[0].astype(BF16)
    w_q1 = peer_w_query[1].astype(BF16)
    w_gin = chunk_w_in[0].astype(BF16)
    w_gout = chunk_w_out[0].astype(BF16)
    rope = _rope_tables(n, 512)
    bias_tabs = _na_bias_tables(na_rpb[0], n // GRID_W)
    lam_init = 0.8 - 0.6 * math.exp(-0.3 * 0)
    lam = (jnp.exp(jnp.sum(lambda_q1[0] * lambda_k1[0])) - jnp.exp(jnp.sum(lambda_q2[0] * lambda_k2[0]))
           + lam_init).astype(F32)
    ka = A_HEADS * 2 * A_QK_DIM

    def cx(m, k):
        return m[b:b + 1, k * d:(k + 1) * d].reshape(1, 1, d)

    sizes = PIPE_BATCHES if sum(PIPE_BATCHES) == b else (b,)
    starts = [sum(sizes[:ci]) for ci in range(len(sizes))]
    n_chunks = len(sizes)

    def rows(a, ci):
        return a[starts[ci]:starts[ci] + sizes[ci]]

    def lat(ci, m, k):
        return rows(m, ci)[:, k * d:(k + 1) * d].reshape(sizes[ci], 1, d)

    def stage_a(ci, x2):
        bc = sizes[ci]
        zc = _mod_mm(rows(ctx, ci).reshape(bc * n_ctx, d), cx(m0, 0), cx(m0, 1), w_in, rows_per_group=n_ctx,
                     tm=256, name="attn_in_ctx")[0]
        z = _mod_mm(x2, lat(ci, m0, 0), lat(ci, m0, 1), w_in, rows_per_group=n, tm=256, rope=rope,
                    rope_chunks=(0, 1), scale_chunks=(0, 3), name="attn_in")[0]
        z3 = z.reshape(bc, n, -1)
        zc3 = zc.reshape(bc, n_ctx, -1)
        oa = _diff_attn(z3, zc3, lam, subln_g[0], lam_init)
        ob = _na_attn(z3, zc3, bias_tabs)
        h1 = _out_proj(x2, lat(ci, m0, 2), oa.reshape(bc * n, ka), ob.reshape(bc * n, -1), w_out[:ka], w_out[ka:],
                       rows_per_group=n)
        q0, xm0 = _mod_mm(h1, lat(ci, m0, 3), lat(ci, m0, 4), w_q0, rows_per_group=n, tm=256, emit_xm=True,
                          name="peer_q0")
        return h1, xm0, _peer_select(q0, peer_sub_keys[0])

    def stage_b(ci, h1, p0):
        zg, h1b = _mod_mm(h1, lat(ci, m1, 0), lat(ci, m1, 1), w_gin, rows_per_group=n, tm=256, delta=p0,
                          gate=lat(ci, m0, 5), bias=chunk_b_in[0], act="gelu", name="gmlp_in")
        h2 = _gmlp_tail(zg, h1b, lat(ci, m1, 2), chunk_ln_g[0], chunk_ln_b[0], chunk_w_s[0], chunk_b_s[0], w_gout,
                        rows_per_group=n)
        q1, xm1 = _mod_mm(h2, lat(ci, m1, 3), lat(ci, m1, 4), w_q1, rows_per_group=n, tm=256, emit_xm=True,
                          name="peer_q1")
        return h2, xm1, _peer_select(q1, peer_sub_keys[1])

    tables = [(_expert_rows(peer_u[i], peer_v[i]), _expert_rows_bf16(peer_u[i], peer_v[i])) for i in range(2)]

    def experts(ci, sel, xm, tabs):
        e, g = sel
        t = xm.shape[0]
        k = PEER_TC_BLOCKS[ci] * SC_TOKEN_STEP if sizes == PIPE_BATCHES else 0
        if k == 0 or (t - k) % SC_TOKEN_STEP:
            return (_peer_experts(e, g, xm, tabs[0]),)
        p_sc = _peer_experts(e, g, xm, tabs[0], first_token=k)
        p_tc = _peer_experts_tc(e, g, xm, tabs[1], n_tokens=k)
        return p_tc, p_sc

    h1s, p0s = [], []
    prev_sel = None
    for ci in range(n_chunks):
        x2 = rows(x, ci).reshape(sizes[ci] * n, d)
        if prev_sel is not None:
            x2, prev_sel, tables = lax.optimization_barrier((x2, prev_sel, tables))
            p0s.append(experts(ci - 1, prev_sel, xm_prev, tables[0]))
        h1, xm_prev, prev_sel = stage_a(ci, x2)
        h1s.append(h1)
    p0s.append(experts(n_chunks - 1, prev_sel, xm_prev, tables[0]))

    h2s, p1s = [], []
    prev_sel = None
    for ci in range(n_chunks):
        p0 = p0s[ci]
        if prev_sel is not None:
            p0, prev_sel = lax.optimization_barrier((p0, prev_sel))
            p1s.append(experts(ci - 1, prev_sel, xm_prev, tables[1]))
        h2, xm_prev, prev_sel = stage_b(ci, h1s[ci], p0)
        h2s.append(h2)
    p1s.append(experts(n_chunks - 1, prev_sel, xm_prev, tables[1]))

    outs = [_final(h2s[ci], p1s[ci], lat(ci, m1, 5), final_norm_g, rows_per_group=n).reshape(sizes[ci], n, d)
            for ci in range(n_chunks)]
    return outs[0] if n_chunks == 1 else jnp.concatenate(outs, axis=0)
```
